```python
import math
import jax, jax.numpy as jnp
from jax import lax
import numpy as np

D_MODEL = 1024
BATCH = 8
SEQ = 2048
DEPTH = 4
DEC_BATCH = 128
DEC_SEQ = 1
PAST_LEN = 16384
PAGE_SIZE = 128

HG_DK = 64
HG_DV = 64
HG_HEADS = D_MODEL // 256
HG_KW = HG_HEADS * HG_DK
HG_WIDTH = HG_HEADS * HG_DV
HG_CHUNK = 16
MB_HEADDIM = 64
MB_HEADS = D_MODEL // 128
MB_DINNER = MB_HEADS * MB_HEADDIM
MB_GROUPS = 2
MB_DSTATE = 64
MB_CONV = 4
MB_CONV_DIM = MB_DINNER + 2 * MB_GROUPS * MB_DSTATE
MB_CHUNK = 128
RW_HEAD = 64
RW_HEADS = D_MODEL // 256
RW_WIDTH = RW_HEADS * RW_HEAD
RW_W_LORA = 32
RW_A_LORA = 32
RW_G_LORA = 64
RW_SHIFT_DIM = 3 * RW_WIDTH + RW_W_LORA + RW_A_LORA + RW_G_LORA
RW_GN_EPS = 64e-5
MIX_WIDTH = HG_WIDTH + MB_DINNER + RW_WIDTH
HG_IN = 2 * HG_KW + 2 * HG_WIDTH
MB_IN = MB_DINNER + MB_CONV_DIM + MB_HEADS
IN_DIM = HG_IN + MB_IN + RW_SHIFT_DIM
D_FF = ((8 * D_MODEL // 3 + 255) // 256) * 256
N_MOD = 6
RMS_EPS = 1e-6

kernel_name = "hymba_hgrn2_ssd_rwkv7_adaln_step"


def rmsnorm(x, g, eps=RMS_EPS):
    x32 = x.astype(jnp.float32)
    return x32 * lax.rsqrt(jnp.mean(x32 * x32, -1, keepdims=True) + eps) * g.astype(jnp.float32)


def pad_time(a, t_pad):
    return jnp.pad(a, [(0, 0), (0, t_pad - a.shape[1])] + [(0, 0)] * (a.ndim - 2))


def hgrn2_lower_bounds(lb_logits):
    cs = jnp.cumsum(jax.nn.softmax(lb_logits.astype(jnp.float32), axis=0), axis=0)
    return cs - cs[:1]


def hgrn2_chunked(q, k, v, logf, s0):
    bsz, t_len, nh, _ = q.shape
    L = min(HG_CHUNK, t_len)
    tp = -(-t_len // L) * L
    q, k, v, logf = (pad_time(a, tp) for a in (q, k, v, logf))
    nc = tp // L
    q, k, v, logf = (a.reshape(bsz, nc, L, nh, a.shape[-1]) for a in (q, k, v, logf))
    b = jnp.cumsum(logf, axis=2)
    causal = jnp.tril(jnp.ones((L, L), bool))
    diff = b[:, :, :, None] - b[:, :, None, :]
    decay = jnp.exp(jnp.where(causal[None, None, :, :, None, None], diff, -jnp.inf))
    att = jnp.einsum('bcthk,bcshk,bctshk->bchts', q, k, decay)
    o_intra = jnp.einsum('bchts,bcshv->bcthv', att, v)
    btot = b[:, :, -1]
    u = jnp.einsum('bcshk,bcshv->bchkv', k * jnp.exp(btot[:, :, None] - b), v)

    def step(S, inp):
        dec, uc = inp
        return dec[..., None] * S + uc, S

    s_final, s_enter = lax.scan(step, s0, (jnp.moveaxis(jnp.exp(btot), 1, 0), jnp.moveaxis(u, 1, 0)))
    s_enter = jnp.moveaxis(s_enter, 0, 1)
    o_inter = jnp.einsum('bcthk,bchkv->bcthv', q * jnp.exp(b), s_enter)
    o = (o_intra + o_inter).reshape(bsz, tp, nh, -1)[:, :t_len]
    return o, s_final


def ssd_chunked(x, dt, A, Bm, Cm, h0):
    bsz, t_len, nh, hp = x.shape
    ng, ns = Bm.shape[2], Bm.shape[3]
    rg = nh // ng
    L = min(MB_CHUNK, t_len)
    tp = -(-t_len // L) * L
    x, dt, Bm, Cm = (pad_time(a, tp) for a in (x, dt, Bm, Cm))
    nc = tp // L
    x = x.reshape(bsz, nc, L, ng, rg, hp)
    dt = dt.reshape(bsz, nc, L, ng, rg)
    Bm = Bm.reshape(bsz, nc, L, ng, ns)
    Cm = Cm.reshape(bsz, nc, L, ng, ns)
    acum = jnp.cumsum(dt * A.reshape(ng, rg), axis=2)
    causal = jnp.tril(jnp.ones((L, L), bool))
    seg = acum[:, :, :, None] - acum[:, :, None, :]
    lmat = jnp.exp(jnp.where(causal[None, None, :, :, None, None], seg, -jnp.inf))
    cb = jnp.einsum('bctgn,bcsgn->bctsg', Cm, Bm)
    xdt = x * dt[..., None]
    y_intra = jnp.einsum('bctsg,bctsgr,bcsgrp->bctgrp', cb, lmat, xdt)
    atot = acum[:, :, -1]
    states = jnp.einsum('bcsgn,bcsgr,bcsgrp->bcgrpn', Bm, jnp.exp(atot[:, :, None] - acum), xdt)

    def step(h, inp):
        dec, sc = inp
        return dec[..., None, None] * h + sc, h

    h_final, h_enter = lax.scan(step, h0.reshape(bsz, ng, rg, hp, ns),
                                (jnp.moveaxis(jnp.exp(atot), 1, 0), jnp.moveaxis(states, 1, 0)))
    h_enter = jnp.moveaxis(h_enter, 0, 1)
    y_inter = jnp.einsum('bctgn,bcgrpn,bctgr->bctgrp', Cm, h_enter, jnp.exp(acum))
    y = (y_intra + y_inter).reshape(bsz, tp, nh, hp)[:, :t_len]
    return y, h_final.reshape(bsz, nh, hp, ns)


def rwkv7_scan(r, logw, k, v, kk, a, s0):
    def step(S, inp):
        r_t, w_t, k_t, v_t, kk_t, a_t = inp
        sa = jnp.einsum('bhvk,bhk->bhv', S, -kk_t)
        S = (S * jnp.exp(w_t)[:, :, None, :] + sa[..., :, None] * (kk_t * a_t)[..., None, :]
             + v_t[..., :, None] * k_t[..., None, :])
        return S, jnp.einsum('bhvk,bhk->bhv', S, r_t)

    xs = tuple(jnp.moveaxis(t, 1, 0) for t in (r, logw, k, v, kk, a))
    s_final, out = lax.scan(step, s0, xs)
    return jnp.moveaxis(out, 0, 1), s_final


def token_mix(h, s_hg, s_ssm, s_conv, s_wkv, s_shift, lb, l, P):
    f32 = jnp.float32
    bsz, t_len, _ = h.shape
    heads = lambda t, n: t.reshape(bsz, t_len, -1, n)
    proj = h @ P['w_in'][l]
    cuts = np.cumsum([HG_KW, HG_KW, HG_WIDTH, HG_WIDTH, MB_DINNER, MB_CONV_DIM, MB_HEADS]).tolist()
    hg_q, hg_f, hg_i, hg_g, mb_z, mb_xbc, mb_dt, rw_p = jnp.split(proj, cuts, axis=-1)

    logf = jnp.logaddexp(jnp.log(lb), jnp.log1p(-lb) + jax.nn.log_sigmoid(hg_f))
    hk = (1.0 - lb) * jax.nn.sigmoid(-hg_f)
    o_hg, s_hg_new = hgrn2_chunked(heads(hg_q, HG_DK), heads(hk, HG_DK), heads(hg_i, HG_DV),
                                   heads(logf, HG_DK), s_hg.astype(f32))
    o_hg = rmsnorm(o_hg, P['hg_norm_g'][l].reshape(HG_HEADS, HG_DV)).reshape(bsz, t_len, HG_WIDTH) * jax.nn.silu(hg_g)

    xbc_full = jnp.concatenate([s_conv.astype(f32), mb_xbc], axis=1)
    conv_w = P['mb_conv_w'][l]
    xbc = sum(xbc_full[:, j:j + t_len] * conv_w[j] for j in range(MB_CONV)) + P['mb_conv_b'][l]
    xbc = jax.nn.silu(xbc)
    m_x, m_b, m_c = jnp.split(xbc, [MB_DINNER, MB_DINNER + MB_GROUPS * MB_DSTATE], axis=-1)
    dt = jax.nn.softplus(mb_dt + P['mb_dt_bias'][l])
    A = -jnp.exp(P['mb_A_log'][l].astype(f32))
    m_x = heads(m_x, MB_HEADDIM)
    y_mb, s_ssm_new = ssd_chunked(m_x, dt, A, heads(m_b, MB_DSTATE), heads(m_c, MB_DSTATE), s_ssm.astype(f32))
    y_mb = (y_mb + P['mb_D'][l][:, None] * m_x).reshape(bsz, t_len, MB_DINNER) * jax.nn.silu(mb_z)
    y_mb = rmsnorm(y_mb.reshape(bsz, t_len, MB_GROUPS, -1),
                   P['mb_norm_g'][l].reshape(MB_GROUPS, -1)).reshape(bsz, t_len, MB_DINNER)
    conv_new = xbc_full[:, t_len:]

    prev = jnp.concatenate([s_shift[:, None].astype(f32), rw_p[:, :-1]], axis=1)
    mixd = rw_p + (prev - rw_p) * P['rw_mu'][l]
    cuts_rw = np.cumsum([RW_WIDTH, RW_WIDTH, RW_WIDTH, RW_W_LORA, RW_A_LORA]).tolist()
    r, k, v, wl, al, gl = jnp.split(mixd, cuts_rw, axis=-1)
    logw = -jnp.exp(-jax.nn.softplus(-(P['rw_w0'][l] + jnp.tanh(wl) @ P['rw_w2'][l])) - 0.5)
    a = jax.nn.sigmoid(P['rw_a0'][l] + al @ P['rw_a2'][l])
    g = jax.nn.sigmoid(gl) @ P['rw_g2'][l]
    kk = heads(k * P['rw_k_k'][l], RW_HEAD)
    kk = kk * lax.rsqrt(jnp.maximum(jnp.sum(kk * kk, -1, keepdims=True), 1e-24))
    k = k * (1.0 + (a - 1.0) * P['rw_k_a'][l])
    r_h, k_h, v_h, a_h, w_h = (heads(t, RW_HEAD) for t in (r, k, v, a, logw))
    o_rw, s_wkv_new = rwkv7_scan(r_h, w_h, k_h, v_h, kk, a_h, s_wkv.astype(f32))
    mu = jnp.mean(o_rw, -1, keepdims=True)
    var = jnp.mean(jnp.square(o_rw - mu), -1, keepdims=True)
    o_rw = ((o_rw - mu) * lax.rsqrt(var + RW_GN_EPS) * P['rw_ln_g'][l].reshape(RW_HEADS, RW_HEAD)
            + P['rw_ln_b'][l].reshape(RW_HEADS, RW_HEAD))
    o_rw = o_rw + jnp.sum(r_h * k_h * P['rw_r_k'][l], -1, keepdims=True) * v_h
    o_rw = o_rw.reshape(bsz, t_len, RW_WIDTH) * g
    shift_new = rw_p[:, -1]

    mix = jnp.concatenate([o_hg, y_mb, o_rw], axis=-1) @ P['w_out'][l]
    new_states = (s_hg_new.astype(s_hg.dtype), s_ssm_new.astype(s_ssm.dtype), conv_new.astype(s_conv.dtype),
                  s_wkv_new.astype(s_wkv.dtype), shift_new.astype(s_shift.dtype))
    return mix, new_states


def trunk(x, c, st_hg, st_ssm, st_conv, st_wkv, st_shift, lb_all, P):
    outs = ([], [], [], [], [])
    c_act = jax.nn.silu(c.astype(jnp.float32))
    for l in range(DEPTH):
        mod = c_act @ P['w_ada'][l] + P['b_ada'][l]
        sh1, sc1, g1, sh2, sc2, g2 = (m[:, None, :] for m in jnp.split(mod, N_MOD, axis=-1))
        h = rmsnorm(x, P['g_norm1'][l]) * (1.0 + sc1) + sh1
        mix, states = token_mix(h, st_hg[l], st_ssm[l], st_conv[l], st_wkv[l], st_shift[l], lb_all[l], l, P)
        x = x + (g1 * mix).astype(x.dtype)
        h = rmsnorm(x, P['g_norm2'][l]) * (1.0 + sc2) + sh2
        ff = (jax.nn.silu(h @ P['w_gate'][l]) * (h @ P['w_up'][l])) @ P['w_down'][l]
        x = x + (g2 * ff).astype(x.dtype)
        for lst, s in zip(outs, states):
            lst.append(s)
    y = rmsnorm(x, P['g_final']).astype(x.dtype)
    return y, [jnp.stack(lst) for lst in outs]


def setup_inputs(seed: int = 0) -> dict:
    key = jax.random.key(seed)
    ks = iter(jax.random.split(key, 64))
    f32 = jnp.float32
    nrm = lambda shape, scale: scale * jax.random.normal(next(ks), shape, f32)
    unif = lambda shape, lo, hi: jax.random.uniform(next(ks), shape, f32, lo, hi)
    gain = lambda shape: 1.0 + nrm(shape, 0.05)
    d = {}
    d['x_prompt'] = nrm((BATCH, SEQ, D_MODEL), 1.0)
    d['x_sample'] = nrm((DEC_BATCH, DEC_SEQ, D_MODEL), 1.0)
    d['state_hgrn'] = nrm((DEPTH, DEC_BATCH, HG_HEADS, HG_DK, HG_DV), 0.5)
    d['state_ssm'] = nrm((DEPTH, DEC_BATCH, MB_HEADS, MB_HEADDIM, MB_DSTATE), 0.5)
    d['state_conv'] = nrm((DEPTH, DEC_BATCH, MB_CONV - 1, MB_CONV_DIM), 1.0)
    d['state_wkv'] = nrm((DEPTH, DEC_BATCH, RW_HEADS, RW_HEAD, RW_HEAD), 0.5)
    d['state_shift'] = nrm((DEPTH, DEC_BATCH, RW_SHIFT_DIM), 1.0)
    d['c_prompt'] = nrm((BATCH, D_MODEL), 1.0)
    d['c_sample'] = nrm((DEC_BATCH, D_MODEL), 1.0)
    d['w_ada'] = nrm((DEPTH, D_MODEL, N_MOD * D_MODEL), 0.3 * D_MODEL ** -0.5)
    d['b_ada'] = nrm((DEPTH, N_MOD * D_MODEL), 0.05)
    d['g_norm1'] = gain((DEPTH, D_MODEL))
    d['w_in'] = nrm((DEPTH, D_MODEL, IN_DIM), D_MODEL ** -0.5)
    d['hg_lb_logits'] = nrm((DEPTH, HG_KW), 1.0)
    d['hg_norm_g'] = gain((DEPTH, HG_WIDTH))
    d['mb_conv_w'] = nrm((DEPTH, MB_CONV, MB_CONV_DIM), 0.5)
    d['mb_conv_b'] = nrm((DEPTH, MB_CONV_DIM), 0.05)
    dt0 = jnp.exp(unif((DEPTH, MB_HEADS), math.log(1e-3), math.log(1e-1)))
    d['mb_dt_bias'] = dt0 + jnp.log(-jnp.expm1(-dt0))
    d['mb_A_log'] = jnp.log(unif((DEPTH, MB_HEADS), 1.0, 16.0))
    d['mb_D'] = gain((DEPTH, MB_HEADS))
    d['mb_norm_g'] = gain((DEPTH, MB_DINNER))
    d['rw_mu'] = unif((DEPTH, RW_SHIFT_DIM), 0.0, 1.0)
    d['rw_w0'] = unif((DEPTH, RW_WIDTH), -6.5, -1.5)
    d['rw_w2'] = nrm((DEPTH, RW_W_LORA, RW_WIDTH), 0.5 * RW_W_LORA ** -0.5)
    d['rw_a0'] = nrm((DEPTH, RW_WIDTH), 0.1)
    d['rw_a2'] = nrm((DEPTH, RW_A_LORA, RW_WIDTH), 0.5 * RW_A_LORA ** -0.5)
    d['rw_g2'] = nrm((DEPTH, RW_G_LORA, RW_WIDTH), RW_G_LORA ** -0.5)
    d['rw_k_k'] = 0.85 + nrm((DEPTH, RW_WIDTH), 0.05)
    d['rw_k_a'] = gain((DEPTH, RW_WIDTH))
    d['rw_r_k'] = nrm((DEPTH, RW_HEADS, RW_HEAD), 0.1)
    d['rw_ln_g'] = gain((DEPTH, RW_WIDTH))
    d['rw_ln_b'] = nrm((DEPTH, RW_WIDTH), 0.02)
    d['w_out'] = nrm((DEPTH, MIX_WIDTH, D_MODEL), MIX_WIDTH ** -0.5)
    d['g_norm2'] = gain((DEPTH, D_MODEL))
    d['w_gate'] = nrm((DEPTH, D_MODEL, D_FF), D_MODEL ** -0.5)
    d['w_up'] = nrm((DEPTH, D_MODEL, D_FF), D_MODEL ** -0.5)
    d['w_down'] = nrm((DEPTH, D_FF, D_MODEL), D_FF ** -0.5)
    d['g_final'] = gain((D_MODEL,))
    return d


def reference(x_prompt, x_sample, state_hgrn, state_ssm, state_conv, state_wkv, state_shift, c_prompt, c_sample,
              w_ada, b_ada, g_norm1, w_in, hg_lb_logits, hg_norm_g, mb_conv_w, mb_conv_b, mb_dt_bias, mb_A_log,
              mb_D, mb_norm_g, rw_mu, rw_w0, rw_w2, rw_a0, rw_a2, rw_g2, rw_k_k, rw_k_a, rw_r_k, rw_ln_g, rw_ln_b,
              w_out, g_norm2, w_gate, w_up, w_down, g_final):
    P = dict(w_ada=w_ada, b_ada=b_ada, g_norm1=g_norm1, w_in=w_in, hg_norm_g=hg_norm_g,
             mb_conv_w=mb_conv_w, mb_conv_b=mb_conv_b, mb_dt_bias=mb_dt_bias, mb_A_log=mb_A_log, mb_D=mb_D,
             mb_norm_g=mb_norm_g, rw_mu=rw_mu, rw_w0=rw_w0, rw_w2=rw_w2, rw_a0=rw_a0, rw_a2=rw_a2, rw_g2=rw_g2,
             rw_k_k=rw_k_k, rw_k_a=rw_k_a, rw_r_k=rw_r_k, rw_ln_g=rw_ln_g, rw_ln_b=rw_ln_b, w_out=w_out,
             g_norm2=g_norm2, w_gate=w_gate, w_up=w_up, w_down=w_down, g_final=g_final)
    lb_all = hgrn2_lower_bounds(hg_lb_logits)
    dtp = x_prompt.dtype
    z_hg = jnp.zeros((DEPTH, BATCH, HG_HEADS, HG_DK, HG_DV), dtp)
    z_ssm = jnp.zeros((DEPTH, BATCH, MB_HEADS, MB_HEADDIM, MB_DSTATE), dtp)
    z_conv = jnp.zeros((DEPTH, BATCH, MB_CONV - 1, MB_CONV_DIM), dtp)
    z_wkv = jnp.zeros((DEPTH, BATCH, RW_HEADS, RW_HEAD, RW_HEAD), dtp)
    z_shift = jnp.zeros((DEPTH, BATCH, RW_SHIFT_DIM), dtp)
    y_prompt, (hg_p, ssm_p, conv_p, wkv_p, shift_p) = trunk(
        x_prompt, c_prompt, z_hg, z_ssm, z_conv, z_wkv, z_shift, lb_all, P)
    y_sample, (hg_s, ssm_s, conv_s, wkv_s, shift_s) = trunk(
        x_sample, c_sample, state_hgrn, state_ssm, state_conv, state_wkv, state_shift, lb_all, P)
    return (y_prompt, y_sample, hg_p, hg_s, ssm_p, ssm_s, conv_p, conv_s, wkv_p, wkv_s, shift_p, shift_s)
```

```python
import functools

import jax
import jax.numpy as jnp
from jax import lax
from jax.experimental import pallas as pl
from jax.experimental.pallas import tpu as pltpu

F32 = jnp.float32
BF16 = jnp.bfloat16
HIGHEST = lax.Precision.HIGHEST

LANES = 128
SUBLANES = 8
HEAD = 64
RMS_EPS = 1e-6
RW_GN_EPS = 64e-5
VMEM_LIMIT = 48 * 1024 * 1024
SEQ_PER_STEP = 4


def _cparams(sem):
    return pltpu.CompilerParams(dimension_semantics=sem, vmem_limit_bytes=VMEM_LIMIT)


def _sigmoid(x):
    return 1.0 / (1.0 + jnp.exp(-x))


def _silu(x):
    return x * _sigmoid(x)


def _softplus(x):
    return jnp.maximum(x, 0.0) + jnp.log1p(jnp.exp(-jnp.abs(x)))


def _seg_sum(x, seg=HEAD):
    n = x.shape[-1]
    i = lax.broadcasted_iota(jnp.int32, (n, n), 0) // seg
    j = lax.broadcasted_iota(jnp.int32, (n, n), 1) // seg
    ones = jnp.where(i == j, 1.0, 0.0).astype(F32)
    return jnp.dot(x, ones, precision=HIGHEST, preferred_element_type=F32)


def _rmsnorm_mod(x, g, sc, sh):
    h = x * lax.rsqrt(jnp.mean(x * x, axis=-1, keepdims=True) + RMS_EPS) * g
    return h * (1.0 + sc) + sh


def _ada_kernel(c_ref, w_ref, b_ref, o_ref):
    a = _silu(c_ref[...]).astype(BF16)
    o_ref[...] = jnp.dot(a, w_ref[...].astype(BF16), preferred_element_type=F32) + b_ref[...]


def _ada(c_all, w_ada, b_ada):
    depth, d, n = w_ada.shape
    rows = c_all.shape[0]
    tn = n // 4
    return pl.pallas_call(
        _ada_kernel,
        grid=(depth, n // tn),
        in_specs=[pl.BlockSpec((rows, d), lambda l, j: (0, 0)),
                  pl.BlockSpec((None, d, tn), lambda l, j: (l, 0, j)),
                  pl.BlockSpec((None, 1, tn), lambda l, j: (l, 0, j))],
        out_specs=pl.BlockSpec((None, rows, tn), lambda l, j: (l, 0, j)),
        out_shape=jax.ShapeDtypeStruct((depth, rows, n), F32),
        compiler_params=_cparams(("parallel", "parallel")),
        name="ada_mod",
    )(c_all, w_ada, b_ada.reshape(depth, 1, n))


def _row_spec(tm, width, col=0):
    return pl.BlockSpec((None, tm, width), lambda g, i: (g, i, col))


def _mod_spec(mod, tm, d, idx):
    if mod.shape[1] == 1:
        return pl.BlockSpec((None, 1, d), lambda g, i: (g, 0, idx))
    return pl.BlockSpec((None, tm, d), lambda g, i: (g, i, idx))


def _const_spec(a):
    nd = a.ndim
    return pl.BlockSpec(a.shape, lambda g, i: (0,) * nd)


def _in_proj_kernel(x_ref, g_ref, sc_ref, sh_ref, w_ref, *rest, epilogue, n_extra):
    h = _rmsnorm_mod(x_ref[...], g_ref[...], sc_ref[...], sh_ref[...])
    y = jnp.dot(h.astype(BF16), w_ref[...], preferred_element_type=F32)
    epilogue(y, rest[:n_extra], rest[n_extra:])


def _split_epilogue(y, extra, outs):
    col = 0
    for o in outs:
        wd = o.shape[-1]
        o[...] = y[:, col:col + wd]
        col += wd


def _hg_epilogue(y, extra, outs):
    lb = extra[0][...]
    q_ref, w_ref, k_ref, i_ref, g_ref = outs
    c = lb.shape[-1]
    zf = y[:, c:2 * c]
    q_ref[...] = y[:, 0:c]
    w_ref[...] = lb + (1.0 - lb) * _sigmoid(zf)
    k_ref[...] = (1.0 - lb) * _sigmoid(-zf)
    i_ref[...] = y[:, 2 * c:3 * c]
    g_ref[...] = y[:, 3 * c:4 * c]


def _in_proj(x, g, mod, w, epilogue, extra, out_widths, tm, name):
    grp, rows, d = x.shape
    kern = functools.partial(_in_proj_kernel, epilogue=epilogue, n_extra=len(extra))
    return pl.pallas_call(
        kern,
        grid=(grp, rows // tm),
        in_specs=[_row_spec(tm, d), _const_spec(g), _mod_spec(mod, tm, d, 1), _mod_spec(mod, tm, d, 0),
                  _const_spec(w)] + [_const_spec(e) for e in extra],
        out_specs=[_row_spec(tm, wd) for wd in out_widths],
        out_shape=[jax.ShapeDtypeStruct((grp, rows, wd), F32) for wd in out_widths],
        compiler_params=_cparams(("parallel", "parallel")),
        name=name,
    )(x, g, mod, mod, w, *extra)


def _shift_rows(cur, prev8, j):
    rolled = pltpu.roll(cur, j, 0)
    row8 = lax.broadcasted_iota(jnp.int32, (SUBLANES, 1), 0)
    head = jnp.where(row8 < j, pltpu.roll(prev8, j, 0), rolled[0:SUBLANES])
    if cur.shape[0] == SUBLANES:
        return head
    return jnp.concatenate([head, rolled[SUBLANES:]], axis=0)


def _prev8_spec(tm, width):
    nb8 = tm // SUBLANES
    return pl.BlockSpec((None, SUBLANES, width), lambda g, i: (g, jnp.maximum(i * nb8 - 1, 0), 0))


def _state8_spec(width):
    return pl.BlockSpec((None, SUBLANES, width), lambda g, i: (g, 0, 0))


def _rw_math(p, prev, mu, w0, a0, kk_s, ka_s, rk, w2, a2, g2):
    c = w0.shape[-1]
    lo = w2.shape[0]
    mixd = p + (prev - p) * mu
    r = mixd[:, 0:c]
    k0 = mixd[:, c:2 * c]
    v = mixd[:, 2 * c:3 * c]
    wl = mixd[:, 3 * c:3 * c + lo]
    al = mixd[:, 3 * c + lo:3 * c + 2 * lo]
    gl = mixd[:, 3 * c + 2 * lo:3 * c + 3 * lo]
    z = -(w0 + jnp.dot(jnp.tanh(wl).astype(BF16), w2, preferred_element_type=F32))
    w = jnp.exp(-jnp.exp(-_softplus(z) - 0.5))
    a = _sigmoid(a0 + jnp.dot(al.astype(BF16), a2, preferred_element_type=F32))
    g = jnp.dot(_sigmoid(gl).astype(BF16), g2, preferred_element_type=F32)
    kk = k0 * kk_s
    kk = kk * lax.rsqrt(jnp.maximum(_seg_sum(kk * kk), 1e-24))
    k = k0 * (1.0 + (a - 1.0) * ka_s)
    bv = _seg_sum(r * k * rk) * v
    return r, w, k, v, -kk, kk * a, g, bv


def _rw_pre_seq_kernel(p_ref, p8_ref, s8_ref, *rest):
    params, outs = rest[:9], rest[9:]
    prev8 = jnp.where(pl.program_id(1) == 0, s8_ref[...], p8_ref[...])
    p = p_ref[...]
    res = _rw_math(p, _shift_rows(p, prev8, 1), *[a[...] for a in params])
    for o, val in zip(outs, res):
        o[...] = val


def _rw_pre_step_kernel(p_ref, prev_ref, *rest):
    params, outs = rest[:9], rest[9:]
    res = _rw_math(p_ref[...], prev_ref[...], *[a[...] for a in params])
    for o, val in zip(outs, res):
        o[...] = val


def _rw_pre(p, shift_state, params, tm, seq):
    grp, rows, cp = p.shape
    c = params[1].shape[-1]
    consts = [_const_spec(a) for a in params]
    outs = dict(out_specs=[_row_spec(tm, c)] * 8, out_shape=[jax.ShapeDtypeStruct((grp, rows, c), F32)] * 8,
                compiler_params=_cparams(("parallel", "parallel")))
    if seq:
        return pl.pallas_call(
            _rw_pre_seq_kernel, grid=(grp, rows // tm),
            in_specs=[_row_spec(tm, cp), _prev8_spec(tm, cp), _state8_spec(cp)] + consts,
            name="rw_pre_seq", **outs)(p, p, shift_state, *params)
    return pl.pallas_call(
        _rw_pre_step_kernel, grid=(grp, rows // tm),
        in_specs=[_row_spec(tm, cp), _row_spec(tm, cp)] + consts,
        name="rw_pre_step", **outs)(p, shift_state, *params)


def _mb_math(cur, s1, s2, s3, z, dtr, cw, cb, dtb, a_exp):
    n_x = z.shape[-1]
    conv = s3 * cw[0:1] + s2 * cw[1:2] + s1 * cw[2:3] + cur * cw[3:4] + cb
    act = _silu(conv)
    mx = act[:, 0:n_x]
    half = (act.shape[-1] - n_x) // 2
    mb = act[:, n_x:n_x + half]
    mc = act[:, n_x + half:]
    dt = _softplus(dtr + dtb)
    i = lax.broadcasted_iota(jnp.int32, (dt.shape[-1], n_x), 0)
    j = lax.broadcasted_iota(jnp.int32, (dt.shape[-1], n_x), 1)
    dte = jnp.dot(dt, jnp.where(j // HEAD == i, 1.0, 0.0).astype(F32), precision=HIGHEST,
                  preferred_element_type=F32)
    i = lax.broadcasted_iota(jnp.int32, (half, 2 * half), 0)
    j = lax.broadcasted_iota(jnp.int32, (half, 2 * half), 1)
    rep = jnp.where(i == (j // LANES) * HEAD + j % HEAD, 1.0, 0.0).astype(F32)
    k = jnp.dot(mb, rep, precision=HIGHEST, preferred_element_type=F32)
    r = jnp.dot(mc, rep, precision=HIGHEST, preferred_element_type=F32)
    return jnp.exp(dte * a_exp), mx * dte, k, r, mx, _silu(z)


def _mb_pre_seq_kernel(xbc_ref, x8_ref, s8_ref, z_ref, dt_ref, cw, cb, dtb, a_exp, *outs):
    prev8 = jnp.where(pl.program_id(1) == 0, s8_ref[...], x8_ref[...])
    cur = xbc_ref[...]
    res = _mb_math(cur, _shift_rows(cur, prev8, 1), _shift_rows(cur, prev8, 2), _shift_rows(cur, prev8, 3),
                   z_ref[...], dt_ref[...], cw[...], cb[...], dtb[...], a_exp[...])
    for o, val in zip(outs, res):
        o[...] = val


def _mb_pre_step_kernel(xbc_ref, s1_ref, s2_ref, s3_ref, z_ref, dt_ref, cw, cb, dtb, a_exp, *outs):
    res = _mb_math(xbc_ref[...], s1_ref[...], s2_ref[...], s3_ref[...], z_ref[...], dt_ref[...],
                   cw[...], cb[...], dtb[...], a_exp[...])
    for o, val in zip(outs, res):
        o[...] = val


def _mb_pre(xbc, z, dtr, conv_state, params, tm, seq):
    grp, rows, cx = xbc.shape
    n_x = z.shape[-1]
    n_k = cx - n_x
    consts = [_const_spec(a) for a in params]
    widths = [n_x, n_x, n_k, n_k, n_x, n_x]
    outs = dict(out_specs=[_row_spec(tm, wd) for wd in widths],
                out_shape=[jax.ShapeDtypeStruct((grp, rows, wd), F32) for wd in widths],
                compiler_params=_cparams(("parallel", "parallel")))
    tail = [_row_spec(tm, n_x), _row_spec(tm, dtr.shape[-1])] + consts
    if seq:
        return pl.pallas_call(
            _mb_pre_seq_kernel, grid=(grp, rows // tm),
            in_specs=[_row_spec(tm, cx), _prev8_spec(tm, cx), _state8_spec(cx)] + tail,
            name="mb_pre_seq", **outs)(xbc, xbc, conv_state, z, dtr, *params)
    return pl.pallas_call(
        _mb_pre_step_kernel, grid=(grp, rows // tm),
        in_specs=[_row_spec(tm, cx)] * 4 + tail,
        name="mb_pre_step", **outs)(xbc, *conv_state, z, dtr, *params)


def _pair_mask():
    lane = lax.broadcasted_iota(jnp.int32, (1, LANES), 1)
    return lane, lane < HEAD


def _seg_cols(p, m0):
    c0 = jnp.sum(jnp.where(m0, p, 0.0), axis=-1, keepdims=True)
    c1 = jnp.sum(jnp.where(m0, 0.0, p), axis=-1, keepdims=True)
    return c0, c1


def _rec_update(s, w, k, r, vb, nkk, kka, m0):
    if nkk is not None:
        c0, c1 = _seg_cols(s * nkk, m0)
        s = s * w + jnp.where(m0, c0, c1) * kka + vb * k
    else:
        s = s * w + vb * k
    o0, o1 = _seg_cols(s * r, m0)
    return s, o0, o1


def _value_cols(vt, mt, m0):
    col = jnp.sum(jnp.where(mt, vt, 0.0), axis=-1, keepdims=True)
    return jnp.where(m0, col[0:HEAD], col[HEAD:2 * HEAD])


def _store_cols(ot_ref, lead, row0, o0, o1, mt):
    shape = (HEAD, LANES)
    mask = jnp.broadcast_to(mt, shape)
    pltpu.store(ot_ref.at[lead + (pl.ds(row0, HEAD), slice(None))], jnp.broadcast_to(o0, shape), mask=mask)
    pltpu.store(ot_ref.at[lead + (pl.ds(row0 + HEAD, HEAD), slice(None))], jnp.broadcast_to(o1, shape), mask=mask)


def _split_rec_refs(refs, delta):
    if delta:
        return refs
    return refs[:3] + (None, None) + refs[3:]


def _rec_seq_kernel(*refs, nb, nhp, tc, delta, divs):
    w_ref, k_ref, r_ref, nkk_ref, kka_ref, vt_ref, s0_ref, ot_ref, so_ref, s_scr = _split_rec_refs(refs, delta)
    dw, dk, dr = divs
    c = pl.program_id(1)

    @pl.when(c == 0)
    def _():
        for b in range(nb):
            for hp in range(nhp):
                s_scr[b, hp] = jnp.concatenate([s0_ref[b, 2 * hp], s0_ref[b, 2 * hp + 1]], axis=-1)

    lane, m0 = _pair_mask()

    def body(t8, carry):
        base = pl.multiple_of(t8 * SUBLANES, SUBLANES)
        for j in range(SUBLANES):
            mt = lane == base + j
            for b in range(nb):
                for hp in range(nhp):
                    def row(ref, d):
                        tile = ref[b, pl.ds(base, SUBLANES), (hp // d) * LANES:(hp // d + 1) * LANES]
                        return tile[j:j + 1]
                    vb = _value_cols(vt_ref[b, hp * LANES:(hp + 1) * LANES, :], mt, m0)
                    s, o0, o1 = _rec_update(s_scr[b, hp], row(w_ref, dw), row(k_ref, dk), row(r_ref, dr), vb,
                                            row(nkk_ref, 1) if delta else None,
                                            row(kka_ref, 1) if delta else None, m0)
                    s_scr[b, hp] = s
                    _store_cols(ot_ref, (b,), hp * LANES, o0, o1, mt)
        return carry

    lax.fori_loop(0, tc // SUBLANES, body, 0)

    @pl.when(c == pl.num_programs(1) - 1)
    def _():
        for b in range(nb):
            for hp in range(nhp):
                s = s_scr[b, hp]
                so_ref[b, 2 * hp] = s[:, 0:HEAD]
                so_ref[b, 2 * hp + 1] = s[:, HEAD:2 * HEAD]


def _rec_step_kernel(*refs, nb, nhp, delta, divs):
    w_ref, k_ref, r_ref, nkk_ref, kka_ref, vt_ref, s0_ref, ot_ref, so_ref = _split_rec_refs(refs, delta)
    dw, dk, dr = divs
    i = pl.program_id(0)
    lane, m0 = _pair_mask()
    for b in range(nb):
        mt = lane == i * nb + b
        for hp in range(nhp):
            def row(ref, d):
                return ref[b:b + 1, (hp // d) * LANES:(hp // d + 1) * LANES]
            vb = _value_cols(vt_ref[hp * LANES:(hp + 1) * LANES, :], mt, m0)
            s = jnp.concatenate([s0_ref[b, 2 * hp], s0_ref[b, 2 * hp + 1]], axis=-1)
            s, o0, o1 = _rec_update(s, row(w_ref, dw), row(k_ref, dk), row(r_ref, dr), vb,
                                    row(nkk_ref, 1) if delta else None,
                                    row(kka_ref, 1) if delta else None, m0)
            so_ref[b, 2 * hp] = s[:, 0:HEAD]
            so_ref[b, 2 * hp + 1] = s[:, HEAD:2 * HEAD]
            _store_cols(ot_ref, (), hp * LANES, o0, o1, mt)


def _recurrence(rows_in, v, s0, divs, seq):
    delta = len(rows_in) == 5
    grp, rows, c = v.shape
    ns, nh = s0.shape[:2]
    nhp = nh // 2
    state_shape = jax.ShapeDtypeStruct(s0.shape, F32)
    if seq:
        tc = min(LANES, rows)
        vt = jnp.swapaxes(v, 1, 2)
        nb = min(SEQ_PER_STEP, grp)
        sblk = (nb,) + s0.shape[1:]
        kern = functools.partial(_rec_seq_kernel, nb=nb, nhp=nhp, tc=tc, delta=delta, divs=divs)
        ot, s_new = pl.pallas_call(
            kern, grid=(grp // nb, rows // tc),
            in_specs=[pl.BlockSpec((nb, tc, a.shape[-1]), lambda g, t: (g, t, 0)) for a in rows_in]
            + [pl.BlockSpec((nb, c, tc), lambda g, t: (g, 0, t)),
               pl.BlockSpec(sblk, lambda g, t: (g, 0, 0, 0))],
            out_specs=[pl.BlockSpec((nb, c, tc), lambda g, t: (g, 0, t)),
                       pl.BlockSpec(sblk, lambda g, t: (g, 0, 0, 0))],
            out_shape=[jax.ShapeDtypeStruct((grp, c, rows), F32), state_shape],
            scratch_shapes=[pltpu.VMEM((nb, nhp, HEAD, LANES), F32)],
            compiler_params=_cparams(("parallel", "arbitrary")),
            name="rec_seq",
        )(*rows_in, vt, s0)
        return jnp.swapaxes(ot, 1, 2), s_new
    assert grp == 1 and rows == ns and rows <= LANES
    nb = SUBLANES
    pad = LANES - rows
    vt = jnp.pad(v[0].T, ((0, 0), (0, pad)))
    kern = functools.partial(_rec_step_kernel, nb=nb, nhp=nhp, delta=delta, divs=divs)
    ot, s_new = pl.pallas_call(
        kern, grid=(rows // nb,),
        in_specs=[pl.BlockSpec((nb, a.shape[-1]), lambda i: (i, 0)) for a in rows_in]
        + [pl.BlockSpec((c, LANES), lambda i: (0, 0)),
           pl.BlockSpec((nb,) + s0.shape[1:], lambda i: (i, 0, 0, 0))],
        out_specs=[pl.BlockSpec((c, LANES), lambda i: (0, 0)),
                   pl.BlockSpec((nb,) + s0.shape[1:], lambda i: (i, 0, 0, 0))],
        out_shape=[jax.ShapeDtypeStruct((c, LANES), F32), state_shape],
        compiler_params=_cparams(("arbitrary",)),
        name="rec_step",
    )(*[a[0] for a in rows_in], vt, s0)
    return ot[:, :rows].T[None], s_new


def _out_proj_kernel(x_ref, gate_ref, ohg_ref, ghg_ref, omb_ref, mx_ref, zs_ref, orw_ref, grw_ref, bv_ref,
                     hg_g, d_exp, mb_g, ln_g, ln_b, wo_ref, o_ref, *, n_grp):
    ohg = ohg_ref[...]
    n_hg = ohg.shape[-1]
    inv = 1.0 / HEAD
    a = ohg * lax.rsqrt(_seg_sum(ohg * ohg) * inv + RMS_EPS) * hg_g[...] * _silu(ghg_ref[...])
    y = (omb_ref[...] + d_exp[...] * mx_ref[...]) * zs_ref[...]
    n_mb = y.shape[-1]
    gw = n_mb // n_grp
    y = jnp.concatenate(
        [y[:, g * gw:(g + 1) * gw]
         * lax.rsqrt(jnp.mean(y[:, g * gw:(g + 1) * gw] ** 2, axis=-1, keepdims=True) + RMS_EPS)
         for g in range(n_grp)], axis=-1) * mb_g[...]
    orw = orw_ref[...]
    mu = _seg_sum(orw) * inv
    dlt = orw - mu
    var = _seg_sum(dlt * dlt) * inv
    c = (dlt * lax.rsqrt(var + RW_GN_EPS) * ln_g[...] + ln_b[...] + bv_ref[...]) * grw_ref[...]
    mix = (jnp.dot(a.astype(BF16), wo_ref[0:n_hg, :], preferred_element_type=F32)
           + jnp.dot(y.astype(BF16), wo_ref[n_hg:n_hg + n_mb, :], preferred_element_type=F32)
           + jnp.dot(c.astype(BF16), wo_ref[n_hg + n_mb:, :], preferred_element_type=F32))
    o_ref[...] = x_ref[...] + gate_ref[...] * mix


def _out_proj(x, mod, acts, params, wo, tm, n_grp):
    grp, rows, d = x.shape
    return pl.pallas_call(
        functools.partial(_out_proj_kernel, n_grp=n_grp),
        grid=(grp, rows // tm),
        in_specs=[_row_spec(tm, d), _mod_spec(mod, tm, d, 2)] + [_row_spec(tm, a.shape[-1]) for a in acts]
        + [_const_spec(a) for a in params] + [_const_spec(wo)],
        out_specs=_row_spec(tm, d),
        out_shape=jax.ShapeDtypeStruct(x.shape, F32),
        compiler_params=_cparams(("parallel", "parallel")),
        name="out_proj",
    )(x, mod, *acts, *params, wo)


def _ffn_kernel(x_ref, g_ref, sc_ref, sh_ref, gate_ref, wg_ref, wu_ref, wd_ref, gf_ref, o_ref, h_scr, acc_scr,
                *, final_norm):
    f = pl.program_id(2)

    @pl.when(f == 0)
    def _():
        h_scr[...] = _rmsnorm_mod(x_ref[...], g_ref[...], sc_ref[...], sh_ref[...]).astype(BF16)
        acc_scr[...] = jnp.zeros_like(acc_scr)

    h = h_scr[...]
    u = _silu(jnp.dot(h, wg_ref[...], preferred_element_type=F32)) * jnp.dot(h, wu_ref[...],
                                                                            preferred_element_type=F32)
    acc_scr[...] += jnp.dot(u.astype(BF16), wd_ref[...], preferred_element_type=F32)

    @pl.when(f == pl.num_programs(2) - 1)
    def _():
        y = x_ref[...] + gate_ref[...] * acc_scr[...]
        if final_norm:
            y = y * lax.rsqrt(jnp.mean(y * y, axis=-1, keepdims=True) + RMS_EPS) * gf_ref[...]
        o_ref[...] = y


def _ffn(x, g, mod, wg, wu, wd, g_final, tm, tf, final_norm):
    grp, rows, d = x.shape
    nf = wg.shape[-1]

    def rs(width):
        return pl.BlockSpec((None, tm, width), lambda gi, i, f: (gi, i, 0))

    def ms(idx):
        if mod.shape[1] == 1:
            return pl.BlockSpec((None, 1, d), lambda gi, i, f: (gi, 0, idx))
        return pl.BlockSpec((None, tm, d), lambda gi, i, f: (gi, i, idx))

    vec = pl.BlockSpec((1, d), lambda gi, i, f: (0, 0))
    return pl.pallas_call(
        functools.partial(_ffn_kernel, final_norm=final_norm),
        grid=(grp, rows // tm, nf // tf),
        in_specs=[rs(d), vec, ms(4), ms(3), ms(5),
                  pl.BlockSpec((d, tf), lambda gi, i, f: (0, f)),
                  pl.BlockSpec((d, tf), lambda gi, i, f: (0, f)),
                  pl.BlockSpec((tf, d), lambda gi, i, f: (f, 0)), vec],
        out_specs=rs(d),
        out_shape=jax.ShapeDtypeStruct(x.shape, F32),
        scratch_shapes=[pltpu.VMEM((tm, d), BF16), pltpu.VMEM((tm, d), F32)],
        compiler_params=_cparams(("parallel", "parallel", "arbitrary")),
        name="ffn",
    )(x, g, mod, mod, mod, wg, wu, wd, g_final)


def _pad_cols(a, width):
    return jnp.pad(a, [(0, 0)] * (a.ndim - 1) + [(0, width - a.shape[-1])])


def _pad_rows(a, height):
    return jnp.pad(a, [(0, 0)] * (a.ndim - 2) + [(0, height - a.shape[-2]), (0, 0)])


def _prep_layer(l, P, dims):
    hgw, n_x, n_bc, nh_mb, rww, lw, la, lg = dims
    w_in = P['w_in'][l]
    c0 = 4 * hgw
    c1 = c0 + n_x
    c2 = c1 + n_x + n_bc
    c3 = c2 + nh_mb
    w_hg = w_in[:, :c0].astype(BF16)
    w_mb = jnp.concatenate([w_in[:, c1:c2], w_in[:, c0:c1], _pad_cols(w_in[:, c2:c3], LANES)], axis=1).astype(BF16)
    w_rw_raw = w_in[:, c3:]

    def rw_cols(a):
        o = 3 * rww
        return jnp.concatenate([a[..., :o], _pad_cols(a[..., o:o + lw], LANES),
                                _pad_cols(a[..., o + lw:o + lw + la], LANES),
                                _pad_cols(a[..., o + lw + la:], LANES)], axis=-1)

    row = lambda a: a.reshape(1, -1).astype(F32)
    return dict(
        w_hg=w_hg, w_mb=w_mb, w_rw=rw_cols(w_rw_raw).astype(BF16), rw_cols=rw_cols,
        g1=row(P['g_norm1'][l]), g2=row(P['g_norm2'][l]),
        rw_params=(rw_cols(row(P['rw_mu'][l])), row(P['rw_w0'][l]), row(P['rw_a0'][l]), row(P['rw_k_k'][l]),
                   row(P['rw_k_a'][l]), row(P['rw_r_k'][l]),
                   _pad_rows(P['rw_w2'][l], LANES).astype(BF16), _pad_rows(P['rw_a2'][l], LANES).astype(BF16),
                   _pad_rows(P['rw_g2'][l], LANES).astype(BF16)),
        mb_params=(P['mb_conv_w'][l].astype(F32), row(P['mb_conv_b'][l]), _pad_cols(row(P['mb_dt_bias'][l]), LANES),
                   row(jnp.repeat(-jnp.exp(P['mb_A_log'][l].astype(F32)), HEAD))),
        out_params=(row(P['hg_norm_g'][l]), row(jnp.repeat(P['mb_D'][l], HEAD)), row(P['mb_norm_g'][l]),
                    row(P['rw_ln_g'][l]), row(P['rw_ln_b'][l])),
        wo=P['w_out'][l].astype(BF16), wg=P['w_gate'][l].astype(BF16), wu=P['w_up'][l].astype(BF16),
        wd=P['w_down'][l].astype(BF16),
    )


def _trunk(x, mods, st_hg, st_ssm, st_conv, st_wkv, st_shift, lb_all, layers, g_final, dims, seq, tm, tf):
    hgw, n_x, n_bc, nh_mb, rww, lw, la, lg = dims
    grp, rows, d = x.shape
    depth = len(layers)
    outs = ([], [], [], [], [])
    for l in range(depth):
        L = layers[l]
        mod = mods[l]
        lb = lb_all[l].reshape(1, -1)
        q, w_hg, k_hg, i_hg, g_hg = _in_proj(x, L['g1'], mod, L['w_hg'], _hg_epilogue, [lb], [hgw] * 5, tm,
                                             "in_proj_hg")
        xbc, z, dtr = _in_proj(x, L['g1'], mod, L['w_mb'], _split_epilogue, [], [n_x + n_bc, n_x, LANES], tm,
                               "in_proj_mb")
        (rw_p,) = _in_proj(x, L['g1'], mod, L['w_rw'], _split_epilogue, [], [L['w_rw'].shape[-1]], tm,
                           "in_proj_rw")
        if seq:
            shift8 = jnp.pad(L['rw_cols'](st_shift[l])[:, None, :], ((0, 0), (SUBLANES - 1, 0), (0, 0)))
            conv8 = jnp.pad(st_conv[l], ((0, 0), (SUBLANES - st_conv[l].shape[1], 0), (0, 0)))
            conv_new = xbc[:, rows - 3:, :]
            shift_new = rw_p[:, rows - 1, :]
        else:
            shift8 = L['rw_cols'](st_shift[l])[None]
            conv8 = tuple(st_conv[l][None, :, 2 - j, :] for j in range(3))
            conv_new = jnp.concatenate([st_conv[l][:, 1:, :], xbc[0][:, None, :]], axis=1)
            shift_new = rw_p[0]
        r_rw, w_rw, k_rw, v_rw, nkk, kka, g_rw, bv = _rw_pre(rw_p, shift8, L['rw_params'], tm, seq)
        w_mb, v_mb, k_mb, r_mb, mx, zs = _mb_pre(xbc, z, dtr, conv8, L['mb_params'], tm, seq)
        o_hg, s_hg = _recurrence((w_hg, k_hg, q), i_hg, jnp.swapaxes(st_hg[l], -1, -2), (1, 1, 1), seq)
        o_mb, s_ssm = _recurrence((w_mb, k_mb, r_mb), v_mb, st_ssm[l], (1, 2, 2), seq)
        o_rw, s_wkv = _recurrence((w_rw, k_rw, r_rw, nkk, kka), v_rw, st_wkv[l], (1, 1, 1), seq)
        x = _out_proj(x, mod, (o_hg, g_hg, o_mb, mx, zs, o_rw, g_rw, bv), L['out_params'], L['wo'], tm,
                      n_bc // (2 * HEAD))
        x = _ffn(x, L['g2'], mod, L['wg'], L['wu'], L['wd'], g_final, tm, tf, final_norm=(l == depth - 1))
        o3 = 3 * rww
        shift_new = jnp.concatenate([shift_new[:, :o3], shift_new[:, o3:o3 + lw],
                                     shift_new[:, o3 + LANES:o3 + LANES + la],
                                     shift_new[:, o3 + 2 * LANES:o3 + 2 * LANES + lg]], axis=-1)
        for lst, s in zip(outs, (jnp.swapaxes(s_hg, -1, -2), s_ssm, conv_new, s_wkv, shift_new)):
            lst.append(s)
    return x, [jnp.stack(lst) for lst in outs]


def kernel(x_prompt, x_sample, state_hgrn, state_ssm, state_conv, state_wkv, state_shift, c_prompt, c_sample,
           w_ada, b_ada, g_norm1, w_in, hg_lb_logits, hg_norm_g, mb_conv_w, mb_conv_b, mb_dt_bias, mb_A_log,
           mb_D, mb_norm_g, rw_mu, rw_w0, rw_w2, rw_a0, rw_a2, rw_g2, rw_k_k, rw_k_a, rw_r_k, rw_ln_g, rw_ln_b,
           w_out, g_norm2, w_gate, w_up, w_down, g_final):
    P = dict(g_norm1=g_norm1, w_in=w_in, hg_norm_g=hg_norm_g, mb_conv_w=mb_conv_w, mb_conv_b=mb_conv_b,
             mb_dt_bias=mb_dt_bias, mb_A_log=mb_A_log, mb_D=mb_D, mb_norm_g=mb_norm_g, rw_mu=rw_mu, rw_w0=rw_w0,
             rw_w2=rw_w2, rw_a0=rw_a0, rw_a2=rw_a2, rw_g2=rw_g2, rw_k_k=rw_k_k, rw_k_a=rw_k_a,
             rw_r_k=rw_r_k.reshape(rw_r_k.shape[0], -1), rw_ln_g=rw_ln_g, rw_ln_b=rw_ln_b, w_out=w_out,
             g_norm2=g_norm2, w_gate=w_gate, w_up=w_up, w_down=w_down)
    depth = w_in.shape[0]
    bsz, t_len, d = x_prompt.shape
    dec = x_sample.shape[0]
    hgw = hg_lb_logits.shape[-1]
    n_x = mb_norm_g.shape[-1]
    n_bc = mb_conv_w.shape[-1] - n_x
    nh_mb = mb_dt_bias.shape[-1]
    rww = rw_w0.shape[-1]
    lw, la, lg = rw_w2.shape[1], rw_a2.shape[1], rw_g2.shape[1]
    dims = (hgw, n_x, n_bc, nh_mb, rww, lw, la, lg)
    layers = [_prep_layer(l, P, dims) for l in range(depth)]
    gf = g_final.reshape(1, d)

    cs = jnp.cumsum(jax.nn.softmax(hg_lb_logits.astype(F32), axis=0), axis=0)
    lb_all = cs - cs[:1]

    n_c = bsz + dec
    pad_c = -n_c % (2 * SUBLANES)
    mods = _ada(jnp.pad(jnp.concatenate([c_prompt, c_sample], axis=0), ((0, pad_c), (0, 0))), w_ada, b_ada)
    mods_p = [mods[l, :bsz][:, None, :] for l in range(depth)]
    mods_s = [mods[l, bsz:n_c][None] for l in range(depth)]

    dt = x_prompt.dtype
    zeros = lambda a: jnp.zeros((depth, bsz) + a.shape[2:], dt)
    y_p, st_p = _trunk(x_prompt, mods_p, zeros(state_hgrn), zeros(state_ssm), zeros(state_conv), zeros(state_wkv),
                       zeros(state_shift), lb_all, layers, gf, dims, True, min(512, t_len), 256)
    y_s, st_s = _trunk(x_sample.reshape(1, dec, d), mods_s, state_hgrn, state_ssm, state_conv, state_wkv,
                       state_shift, lb_all, layers, gf, dims, False, dec, 256)
    hg_p, ssm_p, conv_p, wkv_p, shift_p = st_p
    hg_s, ssm_s, conv_s, wkv_s, shift_s = st_s
    return (y_p, y_s.reshape(x_sample.shape), hg_p, hg_s, ssm_p, ssm_s, conv_p, conv_s, wkv_p, wkv_s,
            shift_p, shift_s)
```

```python
import functools

import jax
import jax.numpy as jnp
from jax import lax
from jax.experimental import pallas as pl
from jax.experimental.pallas import tpu as pltpu

F32 = jnp.float32
BF16 = jnp.bfloat16
HIGHEST = lax.Precision.HIGHEST

LANES = 128
SUBLANES = 8
HEAD = 64
RMS_EPS = 1e-6
RW_GN_EPS = 64e-5
VMEM_LIMIT = 48 * 1024 * 1024
SSD_CHUNK = 128
HG_BLOCK = 128
HG_SUB = 16
RW_CHUNK = 64
RW_INV_BLOCK = 16


def _cparams(sem):
    return pltpu.CompilerParams(dimension_semantics=sem, vmem_limit_bytes=VMEM_LIMIT)


def _sigmoid(x):
    return 1.0 / (1.0 + jnp.exp(-x))


def _silu(x):
    return x * _sigmoid(x)


def _softplus(x):
    return jnp.maximum(x, 0.0) + jnp.log1p(jnp.exp(-jnp.abs(x)))


def _seg_sum(x, seg=HEAD):
    n = x.shape[-1]
    i = lax.broadcasted_iota(jnp.int32, (n, n), 0) // seg
    j = lax.broadcasted_iota(jnp.int32, (n, n), 1) // seg
    ones = jnp.where(i == j, 1.0, 0.0).astype(F32)
    return jnp.dot(x, ones, precision=HIGHEST, preferred_element_type=F32)


def _rmsnorm_mod(x, g, sc, sh):
    h = x * lax.rsqrt(jnp.mean(x * x, axis=-1, keepdims=True) + RMS_EPS) * g
    return h * (1.0 + sc) + sh


def _ada_kernel(c_ref, w_ref, b_ref, o_ref):
    a = _silu(c_ref[...]).astype(BF16)
    o_ref[...] = jnp.dot(a, w_ref[...].astype(BF16), preferred_element_type=F32) + b_ref[...]


def _ada(c_all, w_ada, b_ada):
    depth, d, n = w_ada.shape
    rows = c_all.shape[0]
    tn = n // 4
    return pl.pallas_call(
        _ada_kernel,
        grid=(depth, n // tn),
        in_specs=[pl.BlockSpec((rows, d), lambda l, j: (0, 0)),
                  pl.BlockSpec((None, d, tn), lambda l, j: (l, 0, j)),
                  pl.BlockSpec((None, 1, tn), lambda l, j: (l, 0, j))],
        out_specs=pl.BlockSpec((None, rows, tn), lambda l, j: (l, 0, j)),
        out_shape=jax.ShapeDtypeStruct((depth, rows, n), F32),
        compiler_params=_cparams(("parallel", "parallel")),
        name="ada_mod",
    )(c_all, w_ada, b_ada.reshape(depth, 1, n))


def _row_spec(tm, width, col=0):
    return pl.BlockSpec((None, tm, width), lambda g, i: (g, i, col))


def _mod_spec(mod, tm, d, idx):
    if mod.shape[1] == 1:
        return pl.BlockSpec((None, 1, d), lambda g, i: (g, 0, idx))
    return pl.BlockSpec((None, tm, d), lambda g, i: (g, i, idx))


def _const_spec(a):
    nd = a.ndim
    return pl.BlockSpec(a.shape, lambda g, i: (0,) * nd)


def _in_proj_kernel(x_ref, g_ref, sc_ref, sh_ref, w_ref, *rest, epilogue, n_extra):
    h = _rmsnorm_mod(x_ref[...], g_ref[...], sc_ref[...], sh_ref[...])
    y = jnp.dot(h.astype(BF16), w_ref[...], preferred_element_type=F32)
    epilogue(y, rest[:n_extra], rest[n_extra:])


def _split_epilogue(y, extra, outs):
    col = 0
    for o in outs:
        wd = o.shape[-1]
        o[...] = y[:, col:col + wd]
        col += wd


def _hg_epilogue(y, extra, outs, log_space):
    lb = extra[0][...]
    q_ref, w_ref, k_ref, i_ref, g_ref = outs
    c = lb.shape[-1]
    zf = y[:, c:2 * c]
    q_ref[...] = y[:, 0:c]
    if log_space:
        a = extra[1][...]
        b = extra[2][...] - _softplus(-zf)
        w_ref[...] = jnp.maximum(a, b) + jnp.log1p(jnp.exp(-jnp.abs(a - b)))
    else:
        w_ref[...] = lb + (1.0 - lb) * _sigmoid(zf)
    k_ref[...] = (1.0 - lb) * _sigmoid(-zf)
    i_ref[...] = y[:, 2 * c:3 * c]
    g_ref[...] = y[:, 3 * c:4 * c]


def _in_proj(x, g, mod, w, epilogue, extra, out_widths, tm, name):
    grp, rows, d = x.shape
    kern = functools.partial(_in_proj_kernel, epilogue=epilogue, n_extra=len(extra))
    return pl.pallas_call(
        kern,
        grid=(grp, rows // tm),
        in_specs=[_row_spec(tm, d), _const_spec(g), _mod_spec(mod, tm, d, 1), _mod_spec(mod, tm, d, 0),
                  _const_spec(w)] + [_const_spec(e) for e in extra],
        out_specs=[_row_spec(tm, wd) for wd in out_widths],
        out_shape=[jax.ShapeDtypeStruct((grp, rows, wd), F32) for wd in out_widths],
        compiler_params=_cparams(("parallel", "parallel")),
        name=name,
    )(x, g, mod, mod, w, *extra)


def _shift_rows(cur, prev8, j):
    rolled = pltpu.roll(cur, j, 0)
    row8 = lax.broadcasted_iota(jnp.int32, (SUBLANES, 1), 0)
    head = jnp.where(row8 < j, pltpu.roll(prev8, j, 0), rolled[0:SUBLANES])
    if cur.shape[0] == SUBLANES:
        return head
    return jnp.concatenate([head, rolled[SUBLANES:]], axis=0)


def _prev8_spec(tm, width):
    nb8 = tm // SUBLANES
    return pl.BlockSpec((None, SUBLANES, width), lambda g, i: (g, jnp.maximum(i * nb8 - 1, 0), 0))


def _state8_spec(width):
    return pl.BlockSpec((None, SUBLANES, width), lambda g, i: (g, 0, 0))


def _rw_math(p, prev, mu, w0, a0, kk_s, ka_s, rk, w2, a2, g2, log_space):
    c = w0.shape[-1]
    lo = w2.shape[0]
    mixd = p + (prev - p) * mu
    r = mixd[:, 0:c]
    k0 = mixd[:, c:2 * c]
    v = mixd[:, 2 * c:3 * c]
    wl = mixd[:, 3 * c:3 * c + lo]
    al = mixd[:, 3 * c + lo:3 * c + 2 * lo]
    gl = mixd[:, 3 * c + 2 * lo:3 * c + 3 * lo]
    z = -(w0 + jnp.dot(jnp.tanh(wl).astype(BF16), w2, preferred_element_type=F32))
    w = -jnp.exp(-_softplus(z) - 0.5)
    if not log_space:
        w = jnp.exp(w)
    a = _sigmoid(a0 + jnp.dot(al.astype(BF16), a2, preferred_element_type=F32))
    g = jnp.dot(_sigmoid(gl).astype(BF16), g2, preferred_element_type=F32)
    kk = k0 * kk_s
    kk = kk * lax.rsqrt(jnp.maximum(_seg_sum(kk * kk), 1e-24))
    k = k0 * (1.0 + (a - 1.0) * ka_s)
    bv = _seg_sum(r * k * rk) * v
    return r, w, k, v, -kk, kk * a, g, bv


def _rw_pre_seq_kernel(p_ref, p8_ref, s8_ref, *rest):
    params, outs = rest[:9], rest[9:]
    prev8 = jnp.where(pl.program_id(1) == 0, s8_ref[...], p8_ref[...])
    p = p_ref[...]
    res = _rw_math(p, _shift_rows(p, prev8, 1), *[a[...] for a in params], True)
    for o, val in zip(outs, res):
        o[...] = val


def _rw_pre_step_kernel(p_ref, prev_ref, *rest):
    params, outs = rest[:9], rest[9:]
    res = _rw_math(p_ref[...], prev_ref[...], *[a[...] for a in params], False)
    for o, val in zip(outs, res):
        o[...] = val


def _rw_pre(p, shift_state, params, tm, seq):
    grp, rows, cp = p.shape
    c = params[1].shape[-1]
    consts = [_const_spec(a) for a in params]
    outs = dict(out_specs=[_row_spec(tm, c)] * 8, out_shape=[jax.ShapeDtypeStruct((grp, rows, c), F32)] * 8,
                compiler_params=_cparams(("parallel", "parallel")))
    if seq:
        return pl.pallas_call(
            _rw_pre_seq_kernel, grid=(grp, rows // tm),
            in_specs=[_row_spec(tm, cp), _prev8_spec(tm, cp), _state8_spec(cp)] + consts,
            name="rw_pre_seq", **outs)(p, p, shift_state, *params)
    return pl.pallas_call(
        _rw_pre_step_kernel, grid=(grp, rows // tm),
        in_specs=[_row_spec(tm, cp), _row_spec(tm, cp)] + consts,
        name="rw_pre_step", **outs)(p, shift_state, *params)


def _mb_math(cur, s1, s2, s3, z, dtr, cw, cb, dtb, a_exp, a_pad, chunked):
    n_x = z.shape[-1]
    conv = s3 * cw[0:1] + s2 * cw[1:2] + s1 * cw[2:3] + cur * cw[3:4] + cb
    act = _silu(conv)
    mx = act[:, 0:n_x]
    half = (act.shape[-1] - n_x) // 2
    mb = act[:, n_x:n_x + half]
    mc = act[:, n_x + half:]
    dt = _softplus(dtr + dtb)
    i = lax.broadcasted_iota(jnp.int32, (dt.shape[-1], n_x), 0)
    j = lax.broadcasted_iota(jnp.int32, (dt.shape[-1], n_x), 1)
    dte = jnp.dot(dt, jnp.where(j // HEAD == i, 1.0, 0.0).astype(F32), precision=HIGHEST,
                  preferred_element_type=F32)
    if chunked:
        return dt * a_pad, mx * dte, act[:, n_x:], mx, _silu(z)
    i = lax.broadcasted_iota(jnp.int32, (half, 2 * half), 0)
    j = lax.broadcasted_iota(jnp.int32, (half, 2 * half), 1)
    rep = jnp.where(i == (j // LANES) * HEAD + j % HEAD, 1.0, 0.0).astype(F32)
    k = jnp.dot(mb, rep, precision=HIGHEST, preferred_element_type=F32)
    r = jnp.dot(mc, rep, precision=HIGHEST, preferred_element_type=F32)
    return jnp.exp(dte * a_exp), mx * dte, k, r, mx, _silu(z)


def _mb_pre_seq_kernel(xbc_ref, x8_ref, s8_ref, z_ref, dt_ref, cw, cb, dtb, a_exp, a_pad, *outs):
    prev8 = jnp.where(pl.program_id(1) == 0, s8_ref[...], x8_ref[...])
    cur = xbc_ref[...]
    res = _mb_math(cur, _shift_rows(cur, prev8, 1), _shift_rows(cur, prev8, 2), _shift_rows(cur, prev8, 3),
                   z_ref[...], dt_ref[...], cw[...], cb[...], dtb[...], a_exp[...], a_pad[...], True)
    for o, val in zip(outs, res):
        o[...] = val


def _mb_pre_step_kernel(xbc_ref, s1_ref, s2_ref, s3_ref, z_ref, dt_ref, cw, cb, dtb, a_exp, a_pad, *outs):
    res = _mb_math(xbc_ref[...], s1_ref[...], s2_ref[...], s3_ref[...], z_ref[...], dt_ref[...],
                   cw[...], cb[...], dtb[...], a_exp[...], a_pad[...], False)
    for o, val in zip(outs, res):
        o[...] = val


def _mb_pre(xbc, z, dtr, conv_state, params, tm, seq):
    grp, rows, cx = xbc.shape
    n_x = z.shape[-1]
    n_k = cx - n_x
    consts = [_const_spec(a) for a in params]
    widths = [dtr.shape[-1], n_x, n_k, n_x, n_x] if seq else [n_x, n_x, n_k, n_k, n_x, n_x]
    outs = dict(out_specs=[_row_spec(tm, wd) for wd in widths],
                out_shape=[jax.ShapeDtypeStruct((grp, rows, wd), F32) for wd in widths],
                compiler_params=_cparams(("parallel", "parallel")))
    tail = [_row_spec(tm, n_x), _row_spec(tm, dtr.shape[-1])] + consts
    if seq:
        return pl.pallas_call(
            _mb_pre_seq_kernel, grid=(grp, rows // tm),
            in_specs=[_row_spec(tm, cx), _prev8_spec(tm, cx), _state8_spec(cx)] + tail,
            name="mb_pre_seq", **outs)(xbc, xbc, conv_state, z, dtr, *params)
    return pl.pallas_call(
        _mb_pre_step_kernel, grid=(grp, rows // tm),
        in_specs=[_row_spec(tm, cx)] * 4 + tail,
        name="mb_pre_step", **outs)(xbc, *conv_state, z, dtr, *params)


_NT = (((1,), (1,)), ((), ()))
_TN = (((0,), (0,)), ((), ()))


def _ssd_chunk_kernel(dta_ref, xdt_ref, bc_ref, s0_ref, y_ref, so_ref, h_scr, *, nh, ng):
    c = pl.program_id(1)

    @pl.when(c == 0)
    def _():
        h_scr[...] = s0_ref[...]

    n = dta_ref.shape[0]
    row = lax.broadcasted_iota(jnp.int32, (n, n), 0)
    col = lax.broadcasted_iota(jnp.int32, (n, n), 1)
    tril = row >= col
    acum = jnp.dot(jnp.where(tril, 1.0, 0.0).astype(F32), dta_ref[...], precision=HIGHEST,
                   preferred_element_type=F32)
    acum_t = acum.T
    bc = bc_ref[...]
    rg = nh // ng
    for g in range(ng):
        bg = bc[:, g * HEAD:(g + 1) * HEAD].astype(BF16)
        cg = bc[:, (ng + g) * HEAD:(ng + g + 1) * HEAD].astype(BF16)
        cb = lax.dot_general(cg, bg, _NT, preferred_element_type=F32)
        for r in range(rg):
            h = g * rg + r
            a_col = acum[:, h:h + 1]
            lmat = jnp.exp(jnp.where(tril, a_col - acum_t[h:h + 1, :], -jnp.inf))
            xh = xdt_ref[:, h * HEAD:(h + 1) * HEAD]
            sh = h_scr[h]
            y = jnp.dot((cb * lmat).astype(BF16), xh.astype(BF16), preferred_element_type=F32)
            y = y + jnp.exp(a_col) * lax.dot_general(cg, sh.astype(BF16), _NT, preferred_element_type=F32)
            y_ref[:, h * HEAD:(h + 1) * HEAD] = y
            a_tot = acum[n - 1:n, h:h + 1]
            xd = (xh * jnp.exp(a_tot - a_col)).astype(BF16)
            h_scr[h] = jnp.exp(a_tot) * sh + lax.dot_general(xd, bg, _TN, preferred_element_type=F32)

    @pl.when(c == pl.num_programs(1) - 1)
    def _():
        so_ref[...] = h_scr[...]


def _ssd_chunked(dta, xdt, bc, s0, chunk):
    grp, rows, n_x = xdt.shape
    nh = s0.shape[1]
    ng = bc.shape[-1] // (2 * HEAD)
    chunk = min(chunk, rows)
    sspec = pl.BlockSpec((None,) + s0.shape[1:], lambda g, i: (g, 0, 0, 0))
    return pl.pallas_call(
        functools.partial(_ssd_chunk_kernel, nh=nh, ng=ng),
        grid=(grp, rows // chunk),
        in_specs=[_row_spec(chunk, dta.shape[-1]), _row_spec(chunk, n_x), _row_spec(chunk, bc.shape[-1]), sspec],
        out_specs=[_row_spec(chunk, n_x), sspec],
        out_shape=[jax.ShapeDtypeStruct(xdt.shape, F32), jax.ShapeDtypeStruct(s0.shape, F32)],
        scratch_shapes=[pltpu.VMEM(s0.shape[1:], F32)],
        compiler_params=_cparams(("parallel", "arbitrary")),
        name="ssd_chunk",
    )(dta, xdt, bc, s0)


def _block_diag_mask():
    i = lax.broadcasted_iota(jnp.int32, (LANES, LANES), 0) // HEAD
    j = lax.broadcasted_iota(jnp.int32, (LANES, LANES), 1) // HEAD
    return i == j


def _pair_state_load(s0_ref, s_scr, nhp):
    for p in range(nhp):
        s_scr[p] = jnp.zeros((LANES, LANES), F32)
        s_scr[p, 0:HEAD, 0:HEAD] = s0_ref[2 * p]
        s_scr[p, HEAD:LANES, HEAD:LANES] = s0_ref[2 * p + 1]


def _pair_state_store(s_scr, so_ref, nhp):
    for p in range(nhp):
        so_ref[2 * p] = s_scr[p, 0:HEAD, 0:HEAD]
        so_ref[2 * p + 1] = s_scr[p, HEAD:LANES, HEAD:LANES]


def _chunk_call(kern, rows_in, s0, tb, name):
    grp, rows, c = rows_in[0].shape
    sspec = pl.BlockSpec((None,) + s0.shape[1:], lambda g, i: (g, 0, 0, 0))
    return pl.pallas_call(
        kern, grid=(grp, rows // tb),
        in_specs=[_row_spec(tb, a.shape[-1]) for a in rows_in] + [sspec],
        out_specs=[_row_spec(tb, c), sspec],
        out_shape=[jax.ShapeDtypeStruct((grp, rows, c), F32), jax.ShapeDtypeStruct(s0.shape, F32)],
        scratch_shapes=[pltpu.VMEM((s0.shape[1] // 2, LANES, LANES), F32)],
        compiler_params=_cparams(("parallel", "arbitrary")),
        name=name,
    )(*rows_in, s0)


def _hg_chunk_kernel(q_ref, lf_ref, k_ref, v_ref, s0_ref, o_ref, so_ref, s_scr, *, nhp, sub):
    c = pl.program_id(1)

    @pl.when(c == 0)
    def _():
        _pair_state_load(s0_ref, s_scr, nhp)

    tb = q_ref.shape[0]
    pos = lax.broadcasted_iota(jnp.int32, (tb, 1), 0) % sub
    b = lf_ref[...]
    s = 1
    while s < sub:
        b = b + jnp.where(pos >= s, pltpu.roll(b, s, 0), 0.0)
        s *= 2
    bd = _block_diag_mask()
    seg = jnp.where(bd, 1.0, 0.0).astype(BF16)
    srow = lax.broadcasted_iota(jnp.int32, (sub, 1), 0)
    for p in range(nhp):
        ls = slice(p * LANES, (p + 1) * LANES)
        for ci in range(tb // sub):
            rs = slice(ci * sub, (ci + 1) * sub)
            bc = b[rs, ls]
            qc = q_ref[rs, ls]
            kc = k_ref[rs, ls]
            vc = v_ref[rs, ls]
            btot = bc[sub - 1:sub]
            st = s_scr[p]
            o = lax.dot_general((qc * jnp.exp(bc)).astype(BF16), st.astype(BF16), _NT, preferred_element_type=F32)
            for t in range(sub):
                e = jnp.exp(jnp.where(srow <= t, bc[t:t + 1] - bc, -jnp.inf))
                att = jnp.dot((e * kc * qc[t:t + 1]).astype(BF16), seg, preferred_element_type=F32)
                o = o + jnp.where(srow == t, jnp.sum(att * vc, axis=0, keepdims=True), 0.0)
            o_ref[rs, ls] = o
            kd = (kc * jnp.exp(btot - bc)).astype(BF16)
            upd = lax.dot_general(vc.astype(BF16), kd, _TN, preferred_element_type=F32)
            s_scr[p] = st * jnp.exp(btot) + jnp.where(bd, upd, 0.0)

    @pl.when(c == pl.num_programs(1) - 1)
    def _():
        _pair_state_store(s_scr, so_ref, nhp)


def _hg_chunked(q, lf, k, v, s0):
    nhp = s0.shape[1] // 2
    tb = min(HG_BLOCK, q.shape[1])
    return _chunk_call(functools.partial(_hg_chunk_kernel, nhp=nhp, sub=HG_SUB), (q, lf, k, v), s0, tb, "hg_chunk")


def _hdot(a, b, dims=None):
    if dims is None:
        return jnp.dot(a, b, precision=HIGHEST, preferred_element_type=F32)
    return lax.dot_general(a, b, dims, precision=HIGHEST, preferred_element_type=F32)


def _tri_inverse(n_mat, blk):
    n = n_mat.shape[0]
    ri = lax.broadcasted_iota(jnp.int32, (n, n), 0)
    ci = lax.broadcasted_iota(jnp.int32, (n, n), 1)
    nd = jnp.where(ri // blk == ci // blk, n_mat, 0.0)
    t = jnp.where(ri == ci, 1.0, 0.0) + nd
    pw = nd
    s = 2
    while s < blk:
        pw = _hdot(pw, pw)
        t = t + _hdot(t, pw)
        s *= 2
    size = blk
    while size < n:
        off = jnp.where(ri // (2 * size) == ci // (2 * size), n_mat, 0.0) - jnp.where(
            ri // size == ci // size, n_mat, 0.0)
        t = t + _hdot(_hdot(t, off), t)
        size *= 2
    return t


def _rw_chunk_kernel(r_ref, lw_ref, k_ref, v_ref, a_ref, b_ref, s0_ref, o_ref, so_ref, s_scr, *, nhp):
    c = pl.program_id(1)

    @pl.when(c == 0)
    def _():
        _pair_state_load(s0_ref, s_scr, nhp)

    n = r_ref.shape[0]
    ri = lax.broadcasted_iota(jnp.int32, (n, n), 0)
    ci = lax.broadcasted_iota(jnp.int32, (n, n), 1)
    tril = ri >= ci
    strict = ri > ci
    lw = lw_ref[...]
    cum = _hdot(jnp.where(tril, 1.0, 0.0), lw)
    e_n = jnp.exp(-cum)
    ctot = cum[n - 1:n]
    e_rel = jnp.exp(ctot - cum)
    at = a_ref[...] * jnp.exp(cum - lw)
    rt = r_ref[...] * jnp.exp(cum)
    bt = b_ref[...] * e_n
    kt = k_ref[...] * e_n
    bh = b_ref[...] * e_rel
    kh = k_ref[...] * e_rel
    v = v_ref[...]
    m0 = lax.broadcasted_iota(jnp.int32, (1, LANES), 1) < HEAD
    bd = _block_diag_mask()
    for p in range(nhp):
        ls = slice(p * LANES, (p + 1) * LANES)
        st = s_scr[p]
        ar = jnp.concatenate([at[:, ls], rt[:, ls]], axis=0)
        a_s = _hdot(ar, st, _NT)
        vp = v[:, ls]
        us, os_ = [], []
        for h in range(2):
            arh = jnp.where(m0 if h == 0 else jnp.logical_not(m0), ar, 0.0)
            gb = _hdot(arh, bt[:, ls], _NT)
            gk = _hdot(arh, kt[:, ls], _NT)
            t_inv = _tri_inverse(jnp.where(strict, gb[0:n], 0.0), RW_INV_BLOCK)
            u = _hdot(t_inv, a_s[0:n] + _hdot(jnp.where(strict, gk[0:n], 0.0), vp))
            pm = jnp.concatenate([jnp.where(tril, gb[n:], 0.0), jnp.where(tril, gk[n:], 0.0)], axis=1)
            os_.append(a_s[n:] + _hdot(pm, jnp.concatenate([u, vp], axis=0)))
            us.append(u)
        uv = jnp.concatenate([jnp.where(m0, us[0], us[1]), vp], axis=0)
        o_ref[:, ls] = jnp.where(m0, os_[0], os_[1])
        upd = _hdot(uv, jnp.concatenate([bh[:, ls], kh[:, ls]], axis=0), _TN)
        s_scr[p] = st * jnp.exp(ctot[:, ls]) + jnp.where(bd, upd, 0.0)

    @pl.when(c == pl.num_programs(1) - 1)
    def _():
        _pair_state_store(s_scr, so_ref, nhp)


def _rw_chunked(r, lw, k, v, a, b, s0):
    nhp = s0.shape[1] // 2
    tb = min(RW_CHUNK, r.shape[1])
    return _chunk_call(functools.partial(_rw_chunk_kernel, nhp=nhp), (r, lw, k, v, a, b), s0, tb, "rw_chunk")


def _pair_mask():
    lane = lax.broadcasted_iota(jnp.int32, (1, LANES), 1)
    return lane, lane < HEAD


def _seg_cols(p, m0):
    c0 = jnp.sum(jnp.where(m0, p, 0.0), axis=-1, keepdims=True)
    c1 = jnp.sum(jnp.where(m0, 0.0, p), axis=-1, keepdims=True)
    return c0, c1


def _rec_update(s, w, k, r, vb, nkk, kka, m0):
    if nkk is not None:
        c0, c1 = _seg_cols(s * nkk, m0)
        s = s * w + jnp.where(m0, c0, c1) * kka + vb * k
    else:
        s = s * w + vb * k
    o0, o1 = _seg_cols(s * r, m0)
    return s, o0, o1


def _value_cols(vt, mt, m0):
    col = jnp.sum(jnp.where(mt, vt, 0.0), axis=-1, keepdims=True)
    return jnp.where(m0, col[0:HEAD], col[HEAD:2 * HEAD])


def _store_cols(ot_ref, lead, row0, o0, o1, mt):
    shape = (HEAD, LANES)
    mask = jnp.broadcast_to(mt, shape)
    pltpu.store(ot_ref.at[lead + (pl.ds(row0, HEAD), slice(None))], jnp.broadcast_to(o0, shape), mask=mask)
    pltpu.store(ot_ref.at[lead + (pl.ds(row0 + HEAD, HEAD), slice(None))], jnp.broadcast_to(o1, shape), mask=mask)


def _split_rec_refs(refs, delta):
    if delta:
        return refs
    return refs[:3] + (None, None) + refs[3:]


def _rec_step_kernel(*refs, nb, nhp, delta, divs):
    w_ref, k_ref, r_ref, nkk_ref, kka_ref, vt_ref, s0_ref, ot_ref, so_ref = _split_rec_refs(refs, delta)
    dw, dk, dr = divs
    i = pl.program_id(0)
    lane, m0 = _pair_mask()
    for b in range(nb):
        mt = lane == i * nb + b
        for hp in range(nhp):
            def row(ref, d):
                return ref[b:b + 1, (hp // d) * LANES:(hp // d + 1) * LANES]
            vb = _value_cols(vt_ref[hp * LANES:(hp + 1) * LANES, :], mt, m0)
            s = jnp.concatenate([s0_ref[b, 2 * hp], s0_ref[b, 2 * hp + 1]], axis=-1)
            s, o0, o1 = _rec_update(s, row(w_ref, dw), row(k_ref, dk), row(r_ref, dr), vb,
                                    row(nkk_ref, 1) if delta else None,
                                    row(kka_ref, 1) if delta else None, m0)
            so_ref[b, 2 * hp] = s[:, 0:HEAD]
            so_ref[b, 2 * hp + 1] = s[:, HEAD:2 * HEAD]
            _store_cols(ot_ref, (), hp * LANES, o0, o1, mt)


def _recurrence(rows_in, v, s0, divs):
    delta = len(rows_in) == 5
    grp, rows, c = v.shape
    ns, nh = s0.shape[:2]
    nhp = nh // 2
    state_shape = jax.ShapeDtypeStruct(s0.shape, F32)
    assert grp == 1 and rows == ns and rows <= LANES
    nb = SUBLANES
    pad = LANES - rows
    vt = jnp.pad(v[0].T, ((0, 0), (0, pad)))
    kern = functools.partial(_rec_step_kernel, nb=nb, nhp=nhp, delta=delta, divs=divs)
    ot, s_new = pl.pallas_call(
        kern, grid=(rows // nb,),
        in_specs=[pl.BlockSpec((nb, a.shape[-1]), lambda i: (i, 0)) for a in rows_in]
        + [pl.BlockSpec((c, LANES), lambda i: (0, 0)),
           pl.BlockSpec((nb,) + s0.shape[1:], lambda i: (i, 0, 0, 0))],
        out_specs=[pl.BlockSpec((c, LANES), lambda i: (0, 0)),
                   pl.BlockSpec((nb,) + s0.shape[1:], lambda i: (i, 0, 0, 0))],
        out_shape=[jax.ShapeDtypeStruct((c, LANES), F32), state_shape],
        compiler_params=_cparams(("arbitrary",)),
        name="rec_step",
    )(*[a[0] for a in rows_in], vt, s0)
    return ot[:, :rows].T[None], s_new


def _out_proj_kernel(x_ref, gate_ref, ohg_ref, ghg_ref, omb_ref, mx_ref, zs_ref, orw_ref, grw_ref, bv_ref,
                     hg_g, d_exp, mb_g, ln_g, ln_b, wo_ref, o_ref, *, n_grp):
    ohg = ohg_ref[...]
    n_hg = ohg.shape[-1]
    inv = 1.0 / HEAD
    a = ohg * lax.rsqrt(_seg_sum(ohg * ohg) * inv + RMS_EPS) * hg_g[...] * _silu(ghg_ref[...])
    y = (omb_ref[...] + d_exp[...] * mx_ref[...]) * zs_ref[...]
    n_mb = y.shape[-1]
    gw = n_mb // n_grp
    y = jnp.concatenate(
        [y[:, g * gw:(g + 1) * gw]
         * lax.rsqrt(jnp.mean(y[:, g * gw:(g + 1) * gw] ** 2, axis=-1, keepdims=True) + RMS_EPS)
         for g in range(n_grp)], axis=-1) * mb_g[...]
    orw = orw_ref[...]
    mu = _seg_sum(orw) * inv
    dlt = orw - mu
    var = _seg_sum(dlt * dlt) * inv
    c = (dlt * lax.rsqrt(var + RW_GN_EPS) * ln_g[...] + ln_b[...] + bv_ref[...]) * grw_ref[...]
    mix = (jnp.dot(a.astype(BF16), wo_ref[0:n_hg, :], preferred_element_type=F32)
           + jnp.dot(y.astype(BF16), wo_ref[n_hg:n_hg + n_mb, :], preferred_element_type=F32)
           + jnp.dot(c.astype(BF16), wo_ref[n_hg + n_mb:, :], preferred_element_type=F32))
    o_ref[...] = x_ref[...] + gate_ref[...] * mix


def _out_proj(x, mod, acts, params, wo, tm, n_grp):
    grp, rows, d = x.shape
    return pl.pallas_call(
        functools.partial(_out_proj_kernel, n_grp=n_grp),
        grid=(grp, rows // tm),
        in_specs=[_row_spec(tm, d), _mod_spec(mod, tm, d, 2)] + [_row_spec(tm, a.shape[-1]) for a in acts]
        + [_const_spec(a) for a in params] + [_const_spec(wo)],
        out_specs=_row_spec(tm, d),
        out_shape=jax.ShapeDtypeStruct(x.shape, F32),
        compiler_params=_cparams(("parallel", "parallel")),
        name="out_proj",
    )(x, mod, *acts, *params, wo)


def _ffn_kernel(x_ref, g_ref, sc_ref, sh_ref, gate_ref, wg_ref, wu_ref, wd_ref, gf_ref, o_ref, h_scr, acc_scr,
                *, final_norm):
    f = pl.program_id(2)

    @pl.when(f == 0)
    def _():
        h_scr[...] = _rmsnorm_mod(x_ref[...], g_ref[...], sc_ref[...], sh_ref[...]).astype(BF16)
        acc_scr[...] = jnp.zeros_like(acc_scr)

    h = h_scr[...]
    u = _silu(jnp.dot(h, wg_ref[...], preferred_element_type=F32)) * jnp.dot(h, wu_ref[...],
                                                                            preferred_element_type=F32)
    acc_scr[...] += jnp.dot(u.astype(BF16), wd_ref[...], preferred_element_type=F32)

    @pl.when(f == pl.num_programs(2) - 1)
    def _():
        y = x_ref[...] + gate_ref[...] * acc_scr[...]
        if final_norm:
            y = y * lax.rsqrt(jnp.mean(y * y, axis=-1, keepdims=True) + RMS_EPS) * gf_ref[...]
        o_ref[...] = y


def _ffn(x, g, mod, wg, wu, wd, g_final, tm, tf, final_norm):
    grp, rows, d = x.shape
    nf = wg.shape[-1]

    def rs(width):
        return pl.BlockSpec((None, tm, width), lambda gi, i, f: (gi, i, 0))

    def ms(idx):
        if mod.shape[1] == 1:
            return pl.BlockSpec((None, 1, d), lambda gi, i, f: (gi, 0, idx))
        return pl.BlockSpec((None, tm, d), lambda gi, i, f: (gi, i, idx))

    vec = pl.BlockSpec((1, d), lambda gi, i, f: (0, 0))
    return pl.pallas_call(
        functools.partial(_ffn_kernel, final_norm=final_norm),
        grid=(grp, rows // tm, nf // tf),
        in_specs=[rs(d), vec, ms(4), ms(3), ms(5),
                  pl.BlockSpec((d, tf), lambda gi, i, f: (0, f)),
                  pl.BlockSpec((d, tf), lambda gi, i, f: (0, f)),
                  pl.BlockSpec((tf, d), lambda gi, i, f: (f, 0)), vec],
        out_specs=rs(d),
        out_shape=jax.ShapeDtypeStruct(x.shape, F32),
        scratch_shapes=[pltpu.VMEM((tm, d), BF16), pltpu.VMEM((tm, d), F32)],
        compiler_params=_cparams(("parallel", "parallel", "arbitrary")),
        name="ffn",
    )(x, g, mod, mod, mod, wg, wu, wd, g_final)


def _pad_cols(a, width):
    return jnp.pad(a, [(0, 0)] * (a.ndim - 1) + [(0, width - a.shape[-1])])


def _pad_rows(a, height):
    return jnp.pad(a, [(0, 0)] * (a.ndim - 2) + [(0, height - a.shape[-2]), (0, 0)])


def _prep_layer(l, P, dims):
    hgw, n_x, n_bc, nh_mb, rww, lw, la, lg = dims
    w_in = P['w_in'][l]
    c0 = 4 * hgw
    c1 = c0 + n_x
    c2 = c1 + n_x + n_bc
    c3 = c2 + nh_mb
    w_hg = w_in[:, :c0].astype(BF16)
    w_mb = jnp.concatenate([w_in[:, c1:c2], w_in[:, c0:c1], _pad_cols(w_in[:, c2:c3], LANES)], axis=1).astype(BF16)
    w_rw_raw = w_in[:, c3:]

    def rw_cols(a):
        o = 3 * rww
        return jnp.concatenate([a[..., :o], _pad_cols(a[..., o:o + lw], LANES),
                                _pad_cols(a[..., o + lw:o + lw + la], LANES),
                                _pad_cols(a[..., o + lw + la:], LANES)], axis=-1)

    row = lambda a: a.reshape(1, -1).astype(F32)
    return dict(
        w_hg=w_hg, w_mb=w_mb, w_rw=rw_cols(w_rw_raw).astype(BF16), rw_cols=rw_cols,
        g1=row(P['g_norm1'][l]), g2=row(P['g_norm2'][l]),
        rw_params=(rw_cols(row(P['rw_mu'][l])), row(P['rw_w0'][l]), row(P['rw_a0'][l]), row(P['rw_k_k'][l]),
                   row(P['rw_k_a'][l]), row(P['rw_r_k'][l]),
                   _pad_rows(P['rw_w2'][l], LANES).astype(BF16), _pad_rows(P['rw_a2'][l], LANES).astype(BF16),
                   _pad_rows(P['rw_g2'][l], LANES).astype(BF16)),
        mb_params=(P['mb_conv_w'][l].astype(F32), row(P['mb_conv_b'][l]), _pad_cols(row(P['mb_dt_bias'][l]), LANES),
                   row(jnp.repeat(-jnp.exp(P['mb_A_log'][l].astype(F32)), HEAD)),
                   _pad_cols(row(-jnp.exp(P['mb_A_log'][l].astype(F32))), LANES)),
        out_params=(row(P['hg_norm_g'][l]), row(jnp.repeat(P['mb_D'][l], HEAD)), row(P['mb_norm_g'][l]),
                    row(P['rw_ln_g'][l]), row(P['rw_ln_b'][l])),
        wo=P['w_out'][l].astype(BF16), wg=P['w_gate'][l].astype(BF16), wu=P['w_up'][l].astype(BF16),
        wd=P['w_down'][l].astype(BF16),
    )


def _trunk(x, mods, st_hg, st_ssm, st_conv, st_wkv, st_shift, lb_all, layers, g_final, dims, seq, tm, tf):
    hgw, n_x, n_bc, nh_mb, rww, lw, la, lg = dims
    grp, rows, d = x.shape
    depth = len(layers)
    outs = ([], [], [], [], [])
    for l in range(depth):
        L = layers[l]
        mod = mods[l]
        lb = lb_all[l].reshape(1, -1)
        q, w_hg, k_hg, i_hg, g_hg = _in_proj(x, L['g1'], mod, L['w_hg'],
                                             functools.partial(_hg_epilogue, log_space=seq),
                                             [lb, jnp.log(lb), jnp.log1p(-lb)], [hgw] * 5, tm, "in_proj_hg")
        xbc, z, dtr = _in_proj(x, L['g1'], mod, L['w_mb'], _split_epilogue, [], [n_x + n_bc, n_x, LANES], tm,
                               "in_proj_mb")
        (rw_p,) = _in_proj(x, L['g1'], mod, L['w_rw'], _split_epilogue, [], [L['w_rw'].shape[-1]], tm,
                           "in_proj_rw")
        if seq:
            shift8 = jnp.pad(L['rw_cols'](st_shift[l])[:, None, :], ((0, 0), (SUBLANES - 1, 0), (0, 0)))
            conv8 = jnp.pad(st_conv[l], ((0, 0), (SUBLANES - st_conv[l].shape[1], 0), (0, 0)))
            conv_new = xbc[:, rows - 3:, :]
            shift_new = rw_p[:, rows - 1, :]
        else:
            shift8 = L['rw_cols'](st_shift[l])[None]
            conv8 = tuple(st_conv[l][None, :, 2 - j, :] for j in range(3))
            conv_new = jnp.concatenate([st_conv[l][:, 1:, :], xbc[0][:, None, :]], axis=1)
            shift_new = rw_p[0]
        r_rw, w_rw, k_rw, v_rw, nkk, kka, g_rw, bv = _rw_pre(rw_p, shift8, L['rw_params'], tm, seq)
        s_hg_in = jnp.swapaxes(st_hg[l], -1, -2)
        if seq:
            o_hg, s_hg = _hg_chunked(q, w_hg, k_hg, i_hg, s_hg_in)
            dta, xdt, bc, mx, zs = _mb_pre(xbc, z, dtr, conv8, L['mb_params'], tm, seq)
            o_mb, s_ssm = _ssd_chunked(dta, xdt, bc, st_ssm[l], SSD_CHUNK)
            o_rw, s_wkv = _rw_chunked(r_rw, w_rw, k_rw, v_rw, nkk, kka, st_wkv[l])
        else:
            o_hg, s_hg = _recurrence((w_hg, k_hg, q), i_hg, s_hg_in, (1, 1, 1))
            w_mb, v_mb, k_mb, r_mb, mx, zs = _mb_pre(xbc, z, dtr, conv8, L['mb_params'], tm, seq)
            o_mb, s_ssm = _recurrence((w_mb, k_mb, r_mb), v_mb, st_ssm[l], (1, 2, 2))
            o_rw, s_wkv = _recurrence((w_rw, k_rw, r_rw, nkk, kka), v_rw, st_wkv[l], (1, 1, 1))
        x = _out_proj(x, mod, (o_hg, g_hg, o_mb, mx, zs, o_rw, g_rw, bv), L['out_params'], L['wo'], tm,
                      n_bc // (2 * HEAD))
        x = _ffn(x, L['g2'], mod, L['wg'], L['wu'], L['wd'], g_final, tm, tf, final_norm=(l == depth - 1))
        o3 = 3 * rww
        shift_new = jnp.concatenate([shift_new[:, :o3], shift_new[:, o3:o3 + lw],
                                     shift_new[:, o3 + LANES:o3 + LANES + la],
                                     shift_new[:, o3 + 2 * LANES:o3 + 2 * LANES + lg]], axis=-1)
        for lst, s in zip(outs, (jnp.swapaxes(s_hg, -1, -2), s_ssm, conv_new, s_wkv, shift_new)):
            lst.append(s)
    return x, [jnp.stack(lst) for lst in outs]


def kernel(x_prompt, x_sample, state_hgrn, state_ssm, state_conv, state_wkv, state_shift, c_prompt, c_sample,
           w_ada, b_ada, g_norm1, w_in, hg_lb_logits, hg_norm_g, mb_conv_w, mb_conv_b, mb_dt_bias, mb_A_log,
           mb_D, mb_norm_g, rw_mu, rw_w0, rw_w2, rw_a0, rw_a2, rw_g2, rw_k_k, rw_k_a, rw_r_k, rw_ln_g, rw_ln_b,
           w_out, g_norm2, w_gate, w_up, w_down, g_final):
    P = dict(g_norm1=g_norm1, w_in=w_in, hg_norm_g=hg_norm_g, mb_conv_w=mb_conv_w, mb_conv_b=mb_conv_b,
             mb_dt_bias=mb_dt_bias, mb_A_log=mb_A_log, mb_D=mb_D, mb_norm_g=mb_norm_g, rw_mu=rw_mu, rw_w0=rw_w0,
             rw_w2=rw_w2, rw_a0=rw_a0, rw_a2=rw_a2, rw_g2=rw_g2, rw_k_k=rw_k_k, rw_k_a=rw_k_a,
             rw_r_k=rw_r_k.reshape(rw_r_k.shape[0], -1), rw_ln_g=rw_ln_g, rw_ln_b=rw_ln_b, w_out=w_out,
             g_norm2=g_norm2, w_gate=w_gate, w_up=w_up, w_down=w_down)
    depth = w_in.shape[0]
    bsz, t_len, d = x_prompt.shape
    dec = x_sample.shape[0]
    hgw = hg_lb_logits.shape[-1]
    n_x = mb_norm_g.shape[-1]
    n_bc = mb_conv_w.shape[-1] - n_x
    nh_mb = mb_dt_bias.shape[-1]
    rww = rw_w0.shape[-1]
    lw, la, lg = rw_w2.shape[1], rw_a2.shape[1], rw_g2.shape[1]
    dims = (hgw, n_x, n_bc, nh_mb, rww, lw, la, lg)
    layers = [_prep_layer(l, P, dims) for l in range(depth)]
    gf = g_final.reshape(1, d)

    cs = jnp.cumsum(jax.nn.softmax(hg_lb_logits.astype(F32), axis=0), axis=0)
    lb_all = cs - cs[:1]

    n_c = bsz + dec
    pad_c = -n_c % (2 * SUBLANES)
    mods = _ada(jnp.pad(jnp.concatenate([c_prompt, c_sample], axis=0), ((0, pad_c), (0, 0))), w_ada, b_ada)
    mods_p = [mods[l, :bsz][:, None, :] for l in range(depth)]
    mods_s = [mods[l, bsz:n_c][None] for l in range(depth)]

    dt = x_prompt.dtype
    zeros = lambda a: jnp.zeros((depth, bsz) + a.shape[2:], dt)
    y_p, st_p = _trunk(x_prompt, mods_p, zeros(state_hgrn), zeros(state_ssm), zeros(state_conv), zeros(state_wkv),
                       zeros(state_shift), lb_all, layers, gf, dims, True, min(512, t_len), 256)
    y_s, st_s = _trunk(x_sample.reshape(1, dec, d), mods_s, state_hgrn, state_ssm, state_conv, state_wkv,
                       state_shift, lb_all, layers, gf, dims, False, dec, 256)
    hg_p, ssm_p, conv_p, wkv_p, shift_p = st_p
    hg_s, ssm_s, conv_s, wkv_s, shift_s = st_s
    return (y_p, y_s.reshape(x_sample.shape), hg_p, hg_s, ssm_p, ssm_s, conv_p, conv_s, wkv_p, wkv_s,
            shift_p, shift_s)
```

```python
import functools

import jax
import jax.numpy as jnp
from jax import lax
from jax.experimental import pallas as pl
from jax.experimental.pallas import tpu as pltpu

F32 = jnp.float32
BF16 = jnp.bfloat16
HIGHEST = lax.Precision.HIGHEST

LANES = 128
SUBLANES = 8
HEAD = 64
RMS_EPS = 1e-6
RW_GN_EPS = 64e-5
VMEM_LIMIT = 48 * 1024 * 1024
SSD_CHUNK = 128
HG_BLOCK = 128
HG_SUB = 16
RW_CHUNK = 64
RW_INV_BLOCK = 16
SEQ_PER_STEP = 2
RW_SEQ_PER_STEP = 4


def _cparams(sem):
    return pltpu.CompilerParams(dimension_semantics=sem, vmem_limit_bytes=VMEM_LIMIT)


def _sigmoid(x):
    return 1.0 / (1.0 + jnp.exp(-x))


def _silu(x):
    return x * _sigmoid(x)


def _softplus(x):
    return jnp.maximum(x, 0.0) + jnp.log1p(jnp.exp(-jnp.abs(x)))


def _seg_sum(x, seg=HEAD):
    n = x.shape[-1]
    i = lax.broadcasted_iota(jnp.int32, (n, n), 0) // seg
    j = lax.broadcasted_iota(jnp.int32, (n, n), 1) // seg
    ones = jnp.where(i == j, 1.0, 0.0).astype(F32)
    return jnp.dot(x, ones, precision=HIGHEST, preferred_element_type=F32)


def _rmsnorm_mod(x, g, sc, sh):
    h = x * lax.rsqrt(jnp.mean(x * x, axis=-1, keepdims=True) + RMS_EPS) * g
    return h * (1.0 + sc) + sh


def _ada_kernel(c_ref, w_ref, b_ref, o_ref):
    a = _silu(c_ref[...]).astype(BF16)
    o_ref[...] = jnp.dot(a, w_ref[...].astype(BF16), preferred_element_type=F32) + b_ref[...]


def _ada(c_all, w_ada, b_ada):
    depth, d, n = w_ada.shape
    rows = c_all.shape[0]
    tn = n // 4
    return pl.pallas_call(
        _ada_kernel,
        grid=(depth, n // tn),
        in_specs=[pl.BlockSpec((rows, d), lambda l, j: (0, 0)),
                  pl.BlockSpec((None, d, tn), lambda l, j: (l, 0, j)),
                  pl.BlockSpec((None, 1, tn), lambda l, j: (l, 0, j))],
        out_specs=pl.BlockSpec((None, rows, tn), lambda l, j: (l, 0, j)),
        out_shape=jax.ShapeDtypeStruct((depth, rows, n), F32),
        compiler_params=_cparams(("parallel", "parallel")),
        name="ada_mod",
    )(c_all, w_ada, b_ada.reshape(depth, 1, n))


def _row_spec(tm, width, col=0):
    return pl.BlockSpec((None, tm, width), lambda g, i: (g, i, col))


def _mod_spec(mod, tm, d, idx):
    if mod.shape[1] == 1:
        return pl.BlockSpec((None, 1, d), lambda g, i: (g, 0, idx))
    return pl.BlockSpec((None, tm, d), lambda g, i: (g, i, idx))


def _const_spec(a):
    nd = a.ndim
    return pl.BlockSpec(a.shape, lambda g, i: (0,) * nd)


def _in_proj_kernel(x_ref, g_ref, sc_ref, sh_ref, w_ref, *rest, epilogue, n_extra):
    h = _rmsnorm_mod(x_ref[...], g_ref[...], sc_ref[...], sh_ref[...])
    y = jnp.dot(h.astype(BF16), w_ref[...], preferred_element_type=F32)
    epilogue(y, rest[:n_extra], rest[n_extra:])


def _split_epilogue(y, extra, outs):
    col = 0
    for o in outs:
        wd = o.shape[-1]
        o[...] = y[:, col:col + wd]
        col += wd


def _hg_epilogue(y, extra, outs, log_space):
    lb = extra[0][...]
    q_ref, w_ref, k_ref, i_ref, g_ref = outs
    c = lb.shape[-1]
    zf = y[:, c:2 * c]
    q_ref[...] = y[:, 0:c]
    if log_space:
        a = extra[1][...]
        b = extra[2][...] - _softplus(-zf)
        w_ref[...] = jnp.maximum(a, b) + jnp.log1p(jnp.exp(-jnp.abs(a - b)))
    else:
        w_ref[...] = lb + (1.0 - lb) * _sigmoid(zf)
    k_ref[...] = (1.0 - lb) * _sigmoid(-zf)
    i_ref[...] = y[:, 2 * c:3 * c]
    g_ref[...] = y[:, 3 * c:4 * c]


def _in_proj(x, g, mod, w, epilogue, extra, out_widths, tm, name):
    grp, rows, d = x.shape
    kern = functools.partial(_in_proj_kernel, epilogue=epilogue, n_extra=len(extra))
    return pl.pallas_call(
        kern,
        grid=(grp, rows // tm),
        in_specs=[_row_spec(tm, d), _const_spec(g), _mod_spec(mod, tm, d, 1), _mod_spec(mod, tm, d, 0),
                  _const_spec(w)] + [_const_spec(e) for e in extra],
        out_specs=[_row_spec(tm, wd) for wd in out_widths],
        out_shape=[jax.ShapeDtypeStruct((grp, rows, wd), F32) for wd in out_widths],
        compiler_params=_cparams(("parallel", "parallel")),
        name=name,
    )(x, g, mod, mod, w, *extra)


def _shift_rows(cur, prev8, j):
    rolled = pltpu.roll(cur, j, 0)
    row8 = lax.broadcasted_iota(jnp.int32, (SUBLANES, 1), 0)
    head = jnp.where(row8 < j, pltpu.roll(prev8, j, 0), rolled[0:SUBLANES])
    if cur.shape[0] == SUBLANES:
        return head
    return jnp.concatenate([head, rolled[SUBLANES:]], axis=0)


def _prev8_spec(tm, width):
    nb8 = tm // SUBLANES
    return pl.BlockSpec((None, SUBLANES, width), lambda g, i: (g, jnp.maximum(i * nb8 - 1, 0), 0))


def _state8_spec(width):
    return pl.BlockSpec((None, SUBLANES, width), lambda g, i: (g, 0, 0))


def _rw_math(p, prev, mu, w0, a0, kk_s, ka_s, rk, w2, a2, g2, log_space):
    c = w0.shape[-1]
    lo = w2.shape[0]
    mixd = p + (prev - p) * mu
    r = mixd[:, 0:c]
    k0 = mixd[:, c:2 * c]
    v = mixd[:, 2 * c:3 * c]
    wl = mixd[:, 3 * c:3 * c + lo]
    al = mixd[:, 3 * c + lo:3 * c + 2 * lo]
    gl = mixd[:, 3 * c + 2 * lo:3 * c + 3 * lo]
    z = -(w0 + jnp.dot(jnp.tanh(wl).astype(BF16), w2, preferred_element_type=F32))
    w = -jnp.exp(-_softplus(z) - 0.5)
    if not log_space:
        w = jnp.exp(w)
    a = _sigmoid(a0 + jnp.dot(al.astype(BF16), a2, preferred_element_type=F32))
    g = jnp.dot(_sigmoid(gl).astype(BF16), g2, preferred_element_type=F32)
    kk = k0 * kk_s
    kk = kk * lax.rsqrt(jnp.maximum(_seg_sum(kk * kk), 1e-24))
    k = k0 * (1.0 + (a - 1.0) * ka_s)
    bv = _seg_sum(r * k * rk) * v
    return r, w, k, v, -kk, kk * a, g, bv


def _rw_pre_seq_kernel(p_ref, p8_ref, s8_ref, *rest):
    params, outs = rest[:9], rest[9:]
    prev8 = jnp.where(pl.program_id(1) == 0, s8_ref[...], p8_ref[...])
    p = p_ref[...]
    res = _rw_math(p, _shift_rows(p, prev8, 1), *[a[...] for a in params], True)
    for o, val in zip(outs, res):
        o[...] = val


def _rw_pre_step_kernel(p_ref, prev_ref, *rest):
    params, outs = rest[:9], rest[9:]
    res = _rw_math(p_ref[...], prev_ref[...], *[a[...] for a in params], False)
    for o, val in zip(outs, res):
        o[...] = val


def _rw_pre(p, shift_state, params, tm, seq):
    grp, rows, cp = p.shape
    c = params[1].shape[-1]
    consts = [_const_spec(a) for a in params]
    outs = dict(out_specs=[_row_spec(tm, c)] * 8, out_shape=[jax.ShapeDtypeStruct((grp, rows, c), F32)] * 8,
                compiler_params=_cparams(("parallel", "parallel")))
    if seq:
        return pl.pallas_call(
            _rw_pre_seq_kernel, grid=(grp, rows // tm),
            in_specs=[_row_spec(tm, cp), _prev8_spec(tm, cp), _state8_spec(cp)] + consts,
            name="rw_pre_seq", **outs)(p, p, shift_state, *params)
    return pl.pallas_call(
        _rw_pre_step_kernel, grid=(grp, rows // tm),
        in_specs=[_row_spec(tm, cp), _row_spec(tm, cp)] + consts,
        name="rw_pre_step", **outs)(p, shift_state, *params)


def _mb_math(cur, s1, s2, s3, z, dtr, cw, cb, dtb, a_exp, a_pad, chunked):
    n_x = z.shape[-1]
    conv = s3 * cw[0:1] + s2 * cw[1:2] + s1 * cw[2:3] + cur * cw[3:4] + cb
    act = _silu(conv)
    mx = act[:, 0:n_x]
    half = (act.shape[-1] - n_x) // 2
    mb = act[:, n_x:n_x + half]
    mc = act[:, n_x + half:]
    dt = _softplus(dtr + dtb)
    i = lax.broadcasted_iota(jnp.int32, (dt.shape[-1], n_x), 0)
    j = lax.broadcasted_iota(jnp.int32, (dt.shape[-1], n_x), 1)
    dte = jnp.dot(dt, jnp.where(j // HEAD == i, 1.0, 0.0).astype(F32), precision=HIGHEST,
                  preferred_element_type=F32)
    if chunked:
        return dt * a_pad, mx * dte, act[:, n_x:], mx, _silu(z)
    i = lax.broadcasted_iota(jnp.int32, (half, 2 * half), 0)
    j = lax.broadcasted_iota(jnp.int32, (half, 2 * half), 1)
    rep = jnp.where(i == (j // LANES) * HEAD + j % HEAD, 1.0, 0.0).astype(F32)
    k = jnp.dot(mb, rep, precision=HIGHEST, preferred_element_type=F32)
    r = jnp.dot(mc, rep, precision=HIGHEST, preferred_element_type=F32)
    return jnp.exp(dte * a_exp), mx * dte, k, r, mx, _silu(z)


def _mb_pre_seq_kernel(xbc_ref, x8_ref, s8_ref, z_ref, dt_ref, cw, cb, dtb, a_exp, a_pad, *outs):
    prev8 = jnp.where(pl.program_id(1) == 0, s8_ref[...], x8_ref[...])
    cur = xbc_ref[...]
    res = _mb_math(cur, _shift_rows(cur, prev8, 1), _shift_rows(cur, prev8, 2), _shift_rows(cur, prev8, 3),
                   z_ref[...], dt_ref[...], cw[...], cb[...], dtb[...], a_exp[...], a_pad[...], True)
    for o, val in zip(outs, res):
        o[...] = val


def _mb_pre_step_kernel(xbc_ref, s1_ref, s2_ref, s3_ref, z_ref, dt_ref, cw, cb, dtb, a_exp, a_pad, *outs):
    res = _mb_math(xbc_ref[...], s1_ref[...], s2_ref[...], s3_ref[...], z_ref[...], dt_ref[...],
                   cw[...], cb[...], dtb[...], a_exp[...], a_pad[...], False)
    for o, val in zip(outs, res):
        o[...] = val


def _mb_pre(xbc, z, dtr, conv_state, params, tm, seq):
    grp, rows, cx = xbc.shape
    n_x = z.shape[-1]
    n_k = cx - n_x
    consts = [_const_spec(a) for a in params]
    widths = [dtr.shape[-1], n_x, n_k, n_x, n_x] if seq else [n_x, n_x, n_k, n_k, n_x, n_x]
    outs = dict(out_specs=[_row_spec(tm, wd) for wd in widths],
                out_shape=[jax.ShapeDtypeStruct((grp, rows, wd), F32) for wd in widths],
                compiler_params=_cparams(("parallel", "parallel")))
    tail = [_row_spec(tm, n_x), _row_spec(tm, dtr.shape[-1])] + consts
    if seq:
        return pl.pallas_call(
            _mb_pre_seq_kernel, grid=(grp, rows // tm),
            in_specs=[_row_spec(tm, cx), _prev8_spec(tm, cx), _state8_spec(cx)] + tail,
            name="mb_pre_seq", **outs)(xbc, xbc, conv_state, z, dtr, *params)
    return pl.pallas_call(
        _mb_pre_step_kernel, grid=(grp, rows // tm),
        in_specs=[_row_spec(tm, cx)] * 4 + tail,
        name="mb_pre_step", **outs)(xbc, *conv_state, z, dtr, *params)


_NT = (((1,), (1,)), ((), ()))
_TN = (((0,), (0,)), ((), ()))


def _ssd_chunk_kernel(dta_ref, xdt_ref, bc_ref, s0_ref, y_ref, so_ref, h_scr, *, nh, ng):
    c = pl.program_id(1)

    @pl.when(c == 0)
    def _():
        h_scr[...] = s0_ref[...]

    n = dta_ref.shape[0]
    row = lax.broadcasted_iota(jnp.int32, (n, n), 0)
    col = lax.broadcasted_iota(jnp.int32, (n, n), 1)
    tril = row >= col
    acum = jnp.dot(jnp.where(tril, 1.0, 0.0).astype(F32), dta_ref[...], precision=HIGHEST,
                   preferred_element_type=F32)
    acum_t = acum.T
    bc = bc_ref[...]
    rg = nh // ng
    bgs = [bc[:, g * HEAD:(g + 1) * HEAD].astype(BF16) for g in range(ng)]
    cgs = [bc[:, (ng + g) * HEAD:(ng + g + 1) * HEAD].astype(BF16) for g in range(ng)]
    cbs = [lax.dot_general(cg, bg, _NT, preferred_element_type=F32) for cg, bg in zip(cgs, bgs)]
    work = []
    for h in range(nh):
        g = h // rg
        a_col = acum[:, h:h + 1]
        a_tot = acum[n - 1:n, h:h + 1]
        lmat = jnp.exp(jnp.where(tril, a_col - acum_t[h:h + 1, :], -jnp.inf))
        xh = xdt_ref[:, h * HEAD:(h + 1) * HEAD]
        work.append(dict(h=h, g=g, a_col=a_col, a_tot=a_tot, sh=h_scr[h], xh=xh.astype(BF16),
                         gm=(cbs[g] * lmat).astype(BF16), xd=(xh * jnp.exp(a_tot - a_col)).astype(BF16)))
    for q in work:
        q['intra'] = jnp.dot(q['gm'], q['xh'], preferred_element_type=F32)
        q['inter'] = lax.dot_general(cgs[q['g']], q['sh'].astype(BF16), _NT, preferred_element_type=F32)
        q['upd'] = lax.dot_general(q['xd'], bgs[q['g']], _TN, preferred_element_type=F32)
    for q in work:
        h = q['h']
        y_ref[:, h * HEAD:(h + 1) * HEAD] = q['intra'] + jnp.exp(q['a_col']) * q['inter']
        h_scr[h] = jnp.exp(q['a_tot']) * q['sh'] + q['upd']

    @pl.when(c == pl.num_programs(1) - 1)
    def _():
        so_ref[...] = h_scr[...]


def _ssd_chunked(dta, xdt, bc, s0, chunk):
    grp, rows, n_x = xdt.shape
    nh = s0.shape[1]
    ng = bc.shape[-1] // (2 * HEAD)
    chunk = min(chunk, rows)
    sspec = pl.BlockSpec((None,) + s0.shape[1:], lambda g, i: (g, 0, 0, 0))
    return pl.pallas_call(
        functools.partial(_ssd_chunk_kernel, nh=nh, ng=ng),
        grid=(grp, rows // chunk),
        in_specs=[_row_spec(chunk, dta.shape[-1]), _row_spec(chunk, n_x), _row_spec(chunk, bc.shape[-1]), sspec],
        out_specs=[_row_spec(chunk, n_x), sspec],
        out_shape=[jax.ShapeDtypeStruct(xdt.shape, F32), jax.ShapeDtypeStruct(s0.shape, F32)],
        scratch_shapes=[pltpu.VMEM(s0.shape[1:], F32)],
        compiler_params=_cparams(("parallel", "arbitrary")),
        name="ssd_chunk",
    )(dta, xdt, bc, s0)


def _block_diag_mask():
    i = lax.broadcasted_iota(jnp.int32, (LANES, LANES), 0) // HEAD
    j = lax.broadcasted_iota(jnp.int32, (LANES, LANES), 1) // HEAD
    return i == j


def _pair_state_load(s0_ref, s_scr, nhp):
    for p in range(nhp):
        s_scr[p] = jnp.zeros((LANES, LANES), F32)
        s_scr[p, 0:HEAD, 0:HEAD] = s0_ref[2 * p]
        s_scr[p, HEAD:LANES, HEAD:LANES] = s0_ref[2 * p + 1]


def _pair_state_store(s_scr, so_ref, nhp):
    for p in range(nhp):
        so_ref[2 * p] = s_scr[p, 0:HEAD, 0:HEAD]
        so_ref[2 * p + 1] = s_scr[p, HEAD:LANES, HEAD:LANES]


def _chunk_call(kern, rows_in, s0, tb, nb, name):
    grp, rows, c = rows_in[0].shape
    nb = min(nb, grp)
    sspec = pl.BlockSpec((nb,) + s0.shape[1:], lambda g, i: (g, 0, 0, 0))

    def rspec(width):
        return pl.BlockSpec((nb, tb, width), lambda g, i: (g, i, 0))

    return pl.pallas_call(
        functools.partial(kern, nb=nb), grid=(grp // nb, rows // tb),
        in_specs=[rspec(a.shape[-1]) for a in rows_in] + [sspec],
        out_specs=[rspec(c), sspec],
        out_shape=[jax.ShapeDtypeStruct((grp, rows, c), F32), jax.ShapeDtypeStruct(s0.shape, F32)],
        scratch_shapes=[pltpu.VMEM((nb, s0.shape[1] // 2, LANES, LANES), F32)],
        compiler_params=_cparams(("parallel", "arbitrary")),
        name=name,
    )(*rows_in, s0)


def _hg_chunk_kernel(q_ref, lf_ref, k_ref, v_ref, s0_ref, o_ref, so_ref, s_scr, *, nb, nhp, sub):
    c = pl.program_id(1)

    @pl.when(c == 0)
    def _():
        for s in range(nb):
            _pair_state_load(s0_ref.at[s], s_scr.at[s], nhp)

    tb = q_ref.shape[1]
    pos = lax.broadcasted_iota(jnp.int32, (tb, 1), 0) % sub
    bd = _block_diag_mask()
    seg = jnp.where(bd, 1.0, 0.0).astype(BF16)
    causal = (lax.broadcasted_iota(jnp.int32, (sub, sub, 1), 1) <= lax.broadcasted_iota(jnp.int32, (sub, sub, 1), 0))
    bs = []
    for s in range(nb):
        b = lf_ref[s]
        step = 1
        while step < sub:
            b = b + jnp.where(pos >= step, pltpu.roll(b, step, 0), 0.0)
            step *= 2
        bs.append(b)
    probs = [(s, p) for s in range(nb) for p in range(nhp)]
    states = {sp: s_scr[sp[0], sp[1]] for sp in probs}
    for ci in range(tb // sub):
        rs = slice(ci * sub, (ci + 1) * sub)
        work = []
        for s, p in probs:
            ls = slice(p * LANES, (p + 1) * LANES)
            bc = bs[s][rs, ls]
            qc = q_ref[s, rs, ls]
            kc = k_ref[s, rs, ls]
            vc = v_ref[s, rs, ls]
            btot = bc[sub - 1:sub]
            o = lax.dot_general((qc * jnp.exp(bc)).astype(BF16), states[s, p].astype(BF16), _NT,
                                preferred_element_type=F32)
            e = jnp.exp(jnp.where(causal, bc[:, None, :] - bc[None, :, :], -jnp.inf))
            x = (e * kc[None, :, :] * qc[:, None, :]).astype(BF16).reshape(sub * sub, LANES)
            att = jnp.dot(x, seg, preferred_element_type=F32).reshape(sub, sub, LANES)
            kd = (kc * jnp.exp(btot - bc)).astype(BF16)
            upd = lax.dot_general(vc.astype(BF16), kd, _TN, preferred_element_type=F32)
            work.append((s, p, ls, o, att, vc, jnp.exp(btot), upd))
        for s, p, ls, o, att, vc, dec, upd in work:
            o_ref[s, rs, ls] = o + jnp.sum(att * vc[None, :, :], axis=1)
            states[s, p] = states[s, p] * dec + jnp.where(bd, upd, 0.0)
    for s, p in probs:
        s_scr[s, p] = states[s, p]

    @pl.when(c == pl.num_programs(1) - 1)
    def _():
        for s in range(nb):
            _pair_state_store(s_scr.at[s], so_ref.at[s], nhp)


def _hg_chunked(q, lf, k, v, s0):
    nhp = s0.shape[1] // 2
    tb = min(HG_BLOCK, q.shape[1])
    return _chunk_call(functools.partial(_hg_chunk_kernel, nhp=nhp, sub=HG_SUB), (q, lf, k, v), s0, tb,
                       SEQ_PER_STEP, "hg_chunk")


_NN = (((1,), (0,)), ((), ()))


def _split2(x):
    hi = x.astype(BF16)
    return hi, (x - hi.astype(F32)).astype(BF16)


def _dot3(a, b, dims=_NN):
    ah, am = _split2(a)
    bh, bm = _split2(b)
    (ca,), (cb,) = dims[0]
    return lax.dot_general(jnp.concatenate([ah, ah, am], axis=ca), jnp.concatenate([bh, bm, bh], axis=cb), dims,
                           preferred_element_type=F32)


def _cumsum_rows(x):
    n = x.shape[0]
    tril3 = (lax.broadcasted_iota(jnp.int32, (n, 3 * n), 0) >= lax.broadcasted_iota(jnp.int32, (n, 3 * n), 1) % n)
    hi = x.astype(BF16)
    r1 = x - hi.astype(F32)
    mid = r1.astype(BF16)
    lo = (r1 - mid.astype(F32)).astype(BF16)
    return jnp.dot(jnp.where(tril3, 1.0, 0.0).astype(BF16), jnp.concatenate([hi, mid, lo], axis=0),
                   preferred_element_type=F32)


def _tri_inverse(n_mats, blk, top):
    n = n_mats[0].shape[0]
    ri = lax.broadcasted_iota(jnp.int32, (n, n), 0)
    ci = lax.broadcasted_iota(jnp.int32, (n, n), 1)
    eye = jnp.where(ri == ci, 1.0, 0.0)
    pws = [jnp.where(ri // blk == ci // blk, m, 0.0) for m in n_mats]
    ts = [eye + pw for pw in pws]
    s = 2
    while s < blk:
        pws = [_dot3(pw, pw) for pw in pws]
        ts = [t + _dot3(t, pw) for t, pw in zip(ts, pws)]
        s *= 2
    size = blk
    while size < top:
        in_2blk = ri // (2 * size) == ci // (2 * size)
        in_blk = ri // size == ci // size
        offs = [jnp.where(in_2blk, m, 0.0) - jnp.where(in_blk, m, 0.0) for m in n_mats]
        mids = [_dot3(t, off) for t, off in zip(ts, offs)]
        ts = [t + _dot3(mid, t) for t, mid in zip(ts, mids)]
        size *= 2
    return ts


def _rw_chunk_kernel(r_ref, lw_ref, k_ref, v_ref, a_ref, b_ref, s0_ref, o_ref, so_ref, s_scr, *, nb, nhp):
    c = pl.program_id(1)

    @pl.when(c == 0)
    def _():
        for s in range(nb):
            _pair_state_load(s0_ref.at[s], s_scr.at[s], nhp)

    n = r_ref.shape[1]
    n2 = 2 * n
    ri = lax.broadcasted_iota(jnp.int32, (n2, n2), 0)
    ci = lax.broadcasted_iota(jnp.int32, (n2, n2), 1)
    tril = ri >= ci
    strict = ri > ci
    m0 = lax.broadcasted_iota(jnp.int32, (1, LANES), 1) < HEAD
    bd = _block_diag_mask()

    def stack(x):
        return jnp.concatenate([jnp.where(m0, x, 0.0), jnp.where(m0, 0.0, x)], axis=0)

    probs = []
    for s in range(nb):
        lw = lw_ref[s]
        cum = _cumsum_rows(lw)
        e_n = jnp.exp(-cum)
        ctot = cum[n - 1:n]
        e_rel = jnp.exp(ctot - cum)
        at = a_ref[s] * jnp.exp(cum - lw)
        rt = r_ref[s] * jnp.exp(cum)
        bt = b_ref[s] * e_n
        kt = k_ref[s] * e_n
        bh = b_ref[s] * e_rel
        kh = k_ref[s] * e_rel
        v = v_ref[s]
        for p in range(nhp):
            ls = slice(p * LANES, (p + 1) * LANES)
            probs.append(dict(
                s=s, p=p, ls=ls, vp=v[:, ls], dec=jnp.exp(ctot[:, ls]),
                ar=jnp.concatenate([stack(at[:, ls]), stack(rt[:, ls])], axis=0),
                bk=jnp.concatenate([stack(bt[:, ls]), stack(kt[:, ls])], axis=0),
                bkh=jnp.concatenate([bh[:, ls], kh[:, ls]], axis=0)))
    for q in probs:
        q['st'] = s_scr[q['s'], q['p']]
        q['a_s'] = _dot3(q['ar'], q['st'], _NT)
    for q in probs:
        q['g'] = _dot3(q['ar'], q['bk'], _NT)
    t_invs = _tri_inverse([jnp.where(strict, q['g'][0:n2, 0:n2], 0.0) for q in probs], RW_INV_BLOCK, n)
    for q in probs:
        q['v2'] = stack(q['vp'])
        q['rhs'] = q['a_s'][0:n2] + _dot3(jnp.where(strict, q['g'][0:n2, n2:], 0.0), q['v2'])
    for q, t_inv in zip(probs, t_invs):
        q['u2'] = _dot3(t_inv, q['rhs'])
    for q in probs:
        g = q['g']
        pm = jnp.concatenate([jnp.where(tril, g[n2:, 0:n2], 0.0), jnp.where(tril, g[n2:, n2:], 0.0)], axis=1)
        o2 = q['a_s'][n2:] + _dot3(pm, jnp.concatenate([q['u2'], q['v2']], axis=0))
        o_ref[q['s'], :, q['ls']] = o2[0:n] + o2[n:]
    for q in probs:
        u2 = q['u2']
        upd = _dot3(jnp.concatenate([u2[0:n] + u2[n:], q['vp']], axis=0), q['bkh'], _TN)
        s_scr[q['s'], q['p']] = q['st'] * q['dec'] + jnp.where(bd, upd, 0.0)

    @pl.when(c == pl.num_programs(1) - 1)
    def _():
        for s in range(nb):
            _pair_state_store(s_scr.at[s], so_ref.at[s], nhp)


def _rw_chunked(r, lw, k, v, a, b, s0):
    nhp = s0.shape[1] // 2
    tb = min(RW_CHUNK, r.shape[1])
    return _chunk_call(functools.partial(_rw_chunk_kernel, nhp=nhp), (r, lw, k, v, a, b), s0, tb, RW_SEQ_PER_STEP,
                       "rw_chunk")


def _pair_mask():
    lane = lax.broadcasted_iota(jnp.int32, (1, LANES), 1)
    return lane, lane < HEAD


def _seg_cols(p, m0):
    c0 = jnp.sum(jnp.where(m0, p, 0.0), axis=-1, keepdims=True)
    c1 = jnp.sum(jnp.where(m0, 0.0, p), axis=-1, keepdims=True)
    return c0, c1


def _value_cols(vt, mt, m0):
    col = jnp.sum(jnp.where(mt, vt, 0.0), axis=-1, keepdims=True)
    return jnp.where(m0, col[0:HEAD], col[HEAD:2 * HEAD])


def _store_cols(ot_ref, lead, row0, o0, o1, mt):
    shape = (HEAD, LANES)
    mask = jnp.broadcast_to(mt, shape)
    pltpu.store(ot_ref.at[lead + (pl.ds(row0, HEAD), slice(None))], jnp.broadcast_to(o0, shape), mask=mask)
    pltpu.store(ot_ref.at[lead + (pl.ds(row0 + HEAD, HEAD), slice(None))], jnp.broadcast_to(o1, shape), mask=mask)


def _split_rec_refs(refs, delta):
    if delta:
        return refs
    return refs[:3] + (None, None) + refs[3:]


def _rec_step_kernel(*refs, nb, nhp, delta, divs):
    w_ref, k_ref, r_ref, nkk_ref, kka_ref, vt_ref, s0_ref, ot_ref, so_ref = _split_rec_refs(refs, delta)
    dw, dk, dr = divs
    i = pl.program_id(0)
    lane, m0 = _pair_mask()

    def row(ref, b, hp, d):
        return ref[b:b + 1, (hp // d) * LANES:(hp // d + 1) * LANES]

    probs = [(b, hp) for b in range(nb) for hp in range(nhp)]
    mts = [lane == i * nb + b for b in range(nb)]
    ss = [jnp.concatenate([s0_ref[b, 2 * hp], s0_ref[b, 2 * hp + 1]], axis=-1) for b, hp in probs]
    vbs = [_value_cols(vt_ref[hp * LANES:(hp + 1) * LANES, :], mts[b], m0) for b, hp in probs]
    if delta:
        cols = [_seg_cols(s * row(nkk_ref, b, hp, 1), m0) for s, (b, hp) in zip(ss, probs)]
        ss = [s * row(w_ref, b, hp, dw) + jnp.where(m0, c0, c1) * row(kka_ref, b, hp, 1) + vb * row(k_ref, b, hp, dk)
              for s, (c0, c1), vb, (b, hp) in zip(ss, cols, vbs, probs)]
    else:
        ss = [s * row(w_ref, b, hp, dw) + vb * row(k_ref, b, hp, dk) for s, vb, (b, hp) in zip(ss, vbs, probs)]
    outs = [_seg_cols(s * row(r_ref, b, hp, dr), m0) for s, (b, hp) in zip(ss, probs)]
    for s, (o0, o1), (b, hp) in zip(ss, outs, probs):
        so_ref[b, 2 * hp] = s[:, 0:HEAD]
        so_ref[b, 2 * hp + 1] = s[:, HEAD:2 * HEAD]
        _store_cols(ot_ref, (), hp * LANES, o0, o1, mts[b])


def _recurrence(rows_in, v, s0, divs):
    delta = len(rows_in) == 5
    grp, rows, c = v.shape
    ns, nh = s0.shape[:2]
    nhp = nh // 2
    state_shape = jax.ShapeDtypeStruct(s0.shape, F32)
    assert grp == 1 and rows == ns and rows <= LANES
    nb = SUBLANES
    pad = LANES - rows
    vt = jnp.pad(v[0].T, ((0, 0), (0, pad)))
    kern = functools.partial(_rec_step_kernel, nb=nb, nhp=nhp, delta=delta, divs=divs)
    ot, s_new = pl.pallas_call(
        kern, grid=(rows // nb,),
        in_specs=[pl.BlockSpec((nb, a.shape[-1]), lambda i: (i, 0)) for a in rows_in]
        + [pl.BlockSpec((c, LANES), lambda i: (0, 0)),
           pl.BlockSpec((nb,) + s0.shape[1:], lambda i: (i, 0, 0, 0))],
        out_specs=[pl.BlockSpec((c, LANES), lambda i: (0, 0)),
                   pl.BlockSpec((nb,) + s0.shape[1:], lambda i: (i, 0, 0, 0))],
        out_shape=[jax.ShapeDtypeStruct((c, LANES), F32), state_shape],
        compiler_params=_cparams(("arbitrary",)),
        name="rec_step",
    )(*[a[0] for a in rows_in], vt, s0)
    return ot[:, :rows].T[None], s_new


def _out_proj_kernel(x_ref, gate_ref, ohg_ref, ghg_ref, omb_ref, mx_ref, zs_ref, orw_ref, grw_ref, bv_ref,
                     hg_g, d_exp, mb_g, ln_g, ln_b, wo_ref, o_ref, *, n_grp):
    ohg = ohg_ref[...]
    n_hg = ohg.shape[-1]
    inv = 1.0 / HEAD
    a = ohg * lax.rsqrt(_seg_sum(ohg * ohg) * inv + RMS_EPS) * hg_g[...] * _silu(ghg_ref[...])
    y = (omb_ref[...] + d_exp[...] * mx_ref[...]) * zs_ref[...]
    n_mb = y.shape[-1]
    gw = n_mb // n_grp
    y = jnp.concatenate(
        [y[:, g * gw:(g + 1) * gw]
         * lax.rsqrt(jnp.mean(y[:, g * gw:(g + 1) * gw] ** 2, axis=-1, keepdims=True) + RMS_EPS)
         for g in range(n_grp)], axis=-1) * mb_g[...]
    orw = orw_ref[...]
    mu = _seg_sum(orw) * inv
    dlt = orw - mu
    var = _seg_sum(dlt * dlt) * inv
    c = (dlt * lax.rsqrt(var + RW_GN_EPS) * ln_g[...] + ln_b[...] + bv_ref[...]) * grw_ref[...]
    mix = (jnp.dot(a.astype(BF16), wo_ref[0:n_hg, :], preferred_element_type=F32)
           + jnp.dot(y.astype(BF16), wo_ref[n_hg:n_hg + n_mb, :], preferred_element_type=F32)
           + jnp.dot(c.astype(BF16), wo_ref[n_hg + n_mb:, :], preferred_element_type=F32))
    o_ref[...] = x_ref[...] + gate_ref[...] * mix


def _out_proj(x, mod, acts, params, wo, tm, n_grp):
    grp, rows, d = x.shape
    return pl.pallas_call(
        functools.partial(_out_proj_kernel, n_grp=n_grp),
        grid=(grp, rows // tm),
        in_specs=[_row_spec(tm, d), _mod_spec(mod, tm, d, 2)] + [_row_spec(tm, a.shape[-1]) for a in acts]
        + [_const_spec(a) for a in params] + [_const_spec(wo)],
        out_specs=_row_spec(tm, d),
        out_shape=jax.ShapeDtypeStruct(x.shape, F32),
        compiler_params=_cparams(("parallel", "parallel")),
        name="out_proj",
    )(x, mod, *acts, *params, wo)


def _ffn_kernel(x_ref, g_ref, sc_ref, sh_ref, gate_ref, wg_ref, wu_ref, wd_ref, gf_ref, o_ref, u_scr,
                *, tf, final_norm):
    x = x_ref[...]
    h = _rmsnorm_mod(x, g_ref[...], sc_ref[...], sh_ref[...]).astype(BF16)
    for f in range(wg_ref.shape[-1] // tf):
        fs = slice(f * tf, (f + 1) * tf)
        u = _silu(jnp.dot(h, wg_ref[:, fs], preferred_element_type=F32)) * jnp.dot(h, wu_ref[:, fs],
                                                                                   preferred_element_type=F32)
        u_scr[:, fs] = u.astype(BF16)
    y = x + gate_ref[...] * jnp.dot(u_scr[...], wd_ref[...], preferred_element_type=F32)
    if final_norm:
        y = y * lax.rsqrt(jnp.mean(y * y, axis=-1, keepdims=True) + RMS_EPS) * gf_ref[...]
    o_ref[...] = y


def _ffn(x, g, mod, wg, wu, wd, g_final, tm, tf, final_norm):
    grp, rows, d = x.shape
    nf = wg.shape[-1]
    resident = dict(pipeline_mode=pl.Buffered(1))
    return pl.pallas_call(
        functools.partial(_ffn_kernel, tf=tf, final_norm=final_norm),
        grid=(grp, rows // tm),
        in_specs=[_row_spec(tm, d), _const_spec(g), _mod_spec(mod, tm, d, 4), _mod_spec(mod, tm, d, 3),
                  _mod_spec(mod, tm, d, 5),
                  pl.BlockSpec(wg.shape, lambda gi, i: (0, 0), **resident),
                  pl.BlockSpec(wu.shape, lambda gi, i: (0, 0), **resident),
                  pl.BlockSpec(wd.shape, lambda gi, i: (0, 0), **resident), _const_spec(g_final)],
        out_specs=_row_spec(tm, d),
        out_shape=jax.ShapeDtypeStruct(x.shape, F32),
        scratch_shapes=[pltpu.VMEM((tm, nf), BF16)],
        compiler_params=_cparams(("parallel", "parallel")),
        name="ffn",
    )(x, g, mod, mod, mod, wg, wu, wd, g_final)


def _pad_cols(a, width):
    return jnp.pad(a, [(0, 0)] * (a.ndim - 1) + [(0, width - a.shape[-1])])


def _pad_rows(a, height):
    return jnp.pad(a, [(0, 0)] * (a.ndim - 2) + [(0, height - a.shape[-2]), (0, 0)])


def _prep_layer(l, P, dims):
    hgw, n_x, n_bc, nh_mb, rww, lw, la, lg = dims
    w_in = P['w_in'][l]
    c0 = 4 * hgw
    c1 = c0 + n_x
    c2 = c1 + n_x + n_bc
    c3 = c2 + nh_mb
    w_hg = w_in[:, :c0].astype(BF16)
    w_mb = jnp.concatenate([w_in[:, c1:c2], w_in[:, c0:c1], _pad_cols(w_in[:, c2:c3], LANES)], axis=1).astype(BF16)
    w_rw_raw = w_in[:, c3:]

    def rw_cols(a):
        o = 3 * rww
        return jnp.concatenate([a[..., :o], _pad_cols(a[..., o:o + lw], LANES),
                                _pad_cols(a[..., o + lw:o + lw + la], LANES),
                                _pad_cols(a[..., o + lw + la:], LANES)], axis=-1)

    row = lambda a: a.reshape(1, -1).astype(F32)
    return dict(
        w_hg=w_hg, w_mb=w_mb, w_rw=rw_cols(w_rw_raw).astype(BF16), rw_cols=rw_cols,
        g1=row(P['g_norm1'][l]), g2=row(P['g_norm2'][l]),
        rw_params=(rw_cols(row(P['rw_mu'][l])), row(P['rw_w0'][l]), row(P['rw_a0'][l]), row(P['rw_k_k'][l]),
                   row(P['rw_k_a'][l]), row(P['rw_r_k'][l]),
                   _pad_rows(P['rw_w2'][l], LANES).astype(BF16), _pad_rows(P['rw_a2'][l], LANES).astype(BF16),
                   _pad_rows(P['rw_g2'][l], LANES).astype(BF16)),
        mb_params=(P['mb_conv_w'][l].astype(F32), row(P['mb_conv_b'][l]), _pad_cols(row(P['mb_dt_bias'][l]), LANES),
                   row(jnp.repeat(-jnp.exp(P['mb_A_log'][l].astype(F32)), HEAD)),
                   _pad_cols(row(-jnp.exp(P['mb_A_log'][l].astype(F32))), LANES)),
        out_params=(row(P['hg_norm_g'][l]), row(jnp.repeat(P['mb_D'][l], HEAD)), row(P['mb_norm_g'][l]),
                    row(P['rw_ln_g'][l]), row(P['rw_ln_b'][l])),
        wo=P['w_out'][l].astype(BF16), wg=P['w_gate'][l].astype(BF16), wu=P['w_up'][l].astype(BF16),
        wd=P['w_down'][l].astype(BF16),
    )


def _trunk(x, mods, st_hg, st_ssm, st_conv, st_wkv, st_shift, lb_all, layers, g_final, dims, seq, tm, tf):
    hgw, n_x, n_bc, nh_mb, rww, lw, la, lg = dims
    grp, rows, d = x.shape
    depth = len(layers)
    outs = ([], [], [], [], [])
    for l in range(depth):
        L = layers[l]
        mod = mods[l]
        lb = lb_all[l].reshape(1, -1)
        q, w_hg, k_hg, i_hg, g_hg = _in_proj(x, L['g1'], mod, L['w_hg'],
                                             functools.partial(_hg_epilogue, log_space=seq),
                                             [lb, jnp.log(lb), jnp.log1p(-lb)], [hgw] * 5, tm, "in_proj_hg")
        xbc, z, dtr = _in_proj(x, L['g1'], mod, L['w_mb'], _split_epilogue, [], [n_x + n_bc, n_x, LANES], tm,
                               "in_proj_mb")
        (rw_p,) = _in_proj(x, L['g1'], mod, L['w_rw'], _split_epilogue, [], [L['w_rw'].shape[-1]], tm,
                           "in_proj_rw")
        if seq:
            shift8 = jnp.pad(L['rw_cols'](st_shift[l])[:, None, :], ((0, 0), (SUBLANES - 1, 0), (0, 0)))
            conv8 = jnp.pad(st_conv[l], ((0, 0), (SUBLANES - st_conv[l].shape[1], 0), (0, 0)))
            conv_new = xbc[:, rows - 3:, :]
            shift_new = rw_p[:, rows - 1, :]
        else:
            shift8 = L['rw_cols'](st_shift[l])[None]
            conv8 = tuple(st_conv[l][None, :, 2 - j, :] for j in range(3))
            conv_new = jnp.concatenate([st_conv[l][:, 1:, :], xbc[0][:, None, :]], axis=1)
            shift_new = rw_p[0]
        r_rw, w_rw, k_rw, v_rw, nkk, kka, g_rw, bv = _rw_pre(rw_p, shift8, L['rw_params'], tm, seq)
        s_hg_in = jnp.swapaxes(st_hg[l], -1, -2)
        if seq:
            o_hg, s_hg = _hg_chunked(q, w_hg, k_hg, i_hg, s_hg_in)
            dta, xdt, bc, mx, zs = _mb_pre(xbc, z, dtr, conv8, L['mb_params'], tm, seq)
            o_mb, s_ssm = _ssd_chunked(dta, xdt, bc, st_ssm[l], SSD_CHUNK)
            o_rw, s_wkv = _rw_chunked(r_rw, w_rw, k_rw, v_rw, nkk, kka, st_wkv[l])
        else:
            o_hg, s_hg = _recurrence((w_hg, k_hg, q), i_hg, s_hg_in, (1, 1, 1))
            w_mb, v_mb, k_mb, r_mb, mx, zs = _mb_pre(xbc, z, dtr, conv8, L['mb_params'], tm, seq)
            o_mb, s_ssm = _recurrence((w_mb, k_mb, r_mb), v_mb, st_ssm[l], (1, 2, 2))
            o_rw, s_wkv = _recurrence((w_rw, k_rw, r_rw, nkk, kka), v_rw, st_wkv[l], (1, 1, 1))
        x = _out_proj(x, mod, (o_hg, g_hg, o_mb, mx, zs, o_rw, g_rw, bv), L['out_params'], L['wo'], tm,
                      n_bc // (2 * HEAD))
        x = _ffn(x, L['g2'], mod, L['wg'], L['wu'], L['wd'], g_final, tm, tf, final_norm=(l == depth - 1))
        o3 = 3 * rww
        shift_new = jnp.concatenate([shift_new[:, :o3], shift_new[:, o3:o3 + lw],
                                     shift_new[:, o3 + LANES:o3 + LANES + la],
                                     shift_new[:, o3 + 2 * LANES:o3 + 2 * LANES + lg]], axis=-1)
        for lst, s in zip(outs, (jnp.swapaxes(s_hg, -1, -2), s_ssm, conv_new, s_wkv, shift_new)):
            lst.append(s)
    return x, [jnp.stack(lst) for lst in outs]


def kernel(x_prompt, x_sample, state_hgrn, state_ssm, state_conv, state_wkv, state_shift, c_prompt, c_sample,
           w_ada, b_ada, g_norm1, w_in, hg_lb_logits, hg_norm_g, mb_conv_w, mb_conv_b, mb_dt_bias, mb_A_log,
           mb_D, mb_norm_g, rw_mu, rw_w0, rw_w2, rw_a0, rw_a2, rw_g2, rw_k_k, rw_k_a, rw_r_k, rw_ln_g, rw_ln_b,
           w_out, g_norm2, w_gate, w_up, w_down, g_final):
    P = dict(g_norm1=g_norm1, w_in=w_in, hg_norm_g=hg_norm_g, mb_conv_w=mb_conv_w, mb_conv_b=mb_conv_b,
             mb_dt_bias=mb_dt_bias, mb_A_log=mb_A_log, mb_D=mb_D, mb_norm_g=mb_norm_g, rw_mu=rw_mu, rw_w0=rw_w0,
             rw_w2=rw_w2, rw_a0=rw_a0, rw_a2=rw_a2, rw_g2=rw_g2, rw_k_k=rw_k_k, rw_k_a=rw_k_a,
             rw_r_k=rw_r_k.reshape(rw_r_k.shape[0], -1), rw_ln_g=rw_ln_g, rw_ln_b=rw_ln_b, w_out=w_out,
             g_norm2=g_norm2, w_gate=w_gate, w_up=w_up, w_down=w_down)
    depth = w_in.shape[0]
    bsz, t_len, d = x_prompt.shape
    dec = x_sample.shape[0]
    hgw = hg_lb_logits.shape[-1]
    n_x = mb_norm_g.shape[-1]
    n_bc = mb_conv_w.shape[-1] - n_x
    nh_mb = mb_dt_bias.shape[-1]
    rww = rw_w0.shape[-1]
    lw, la, lg = rw_w2.shape[1], rw_a2.shape[1], rw_g2.shape[1]
    dims = (hgw, n_x, n_bc, nh_mb, rww, lw, la, lg)
    layers = [_prep_layer(l, P, dims) for l in range(depth)]
    gf = g_final.reshape(1, d)

    cs = jnp.cumsum(jax.nn.softmax(hg_lb_logits.astype(F32), axis=0), axis=0)
    lb_all = cs - cs[:1]

    n_c = bsz + dec
    pad_c = -n_c % (2 * SUBLANES)
    mods = _ada(jnp.pad(jnp.concatenate([c_prompt, c_sample], axis=0), ((0, pad_c), (0, 0))), w_ada, b_ada)
    mods_p = [mods[l, :bsz][:, None, :] for l in range(depth)]
    mods_s = [mods[l, bsz:n_c][None] for l in range(depth)]

    dt = x_prompt.dtype
    zeros = lambda a: jnp.zeros((depth, bsz) + a.shape[2:], dt)
    y_p, st_p = _trunk(x_prompt, mods_p, zeros(state_hgrn), zeros(state_ssm), zeros(state_conv), zeros(state_wkv),
                       zeros(state_shift), lb_all, layers, gf, dims, True, min(512, t_len), 256)
    y_s, st_s = _trunk(x_sample.reshape(1, dec, d), mods_s, state_hgrn, state_ssm, state_conv, state_wkv,
                       state_shift, lb_all, layers, gf, dims, False, dec, 256)
    hg_p, ssm_p, conv_p, wkv_p, shift_p = st_p
    hg_s, ssm_s, conv_s, wkv_s, shift_s = st_s
    return (y_p, y_s.reshape(x_sample.shape), hg_p, hg_s, ssm_p, ssm_s, conv_p, conv_s, wkv_p, wkv_s,
            shift_p, shift_s)
```

```python
import functools

import jax
import jax.numpy as jnp
from jax import lax
from jax.experimental import pallas as pl
from jax.experimental.pallas import tpu as pltpu

F32 = jnp.float32
BF16 = jnp.bfloat16

LANES = 128
SUBLANES = 8
HEAD = 64
RMS_EPS = 1e-6
RW_GN_EPS = 64e-5
VMEM_LIMIT = 48 * 1024 * 1024
SSD_CHUNK = 128
HG_BLOCK = 128
HG_SUB = 16
RW_CHUNK = 64
RW_INV_BLOCK = 16
SEQ_PER_STEP = 2
RW_SEQ_PER_STEP = 4

_NN = (((1,), (0,)), ((), ()))
_NT = (((1,), (1,)), ((), ()))
_TN = (((0,), (0,)), ((), ()))


def _cparams(sem):
    return pltpu.CompilerParams(dimension_semantics=sem, vmem_limit_bytes=VMEM_LIMIT)


def _sigmoid(x):
    return 1.0 / (1.0 + jnp.exp(-x))


def _silu(x):
    return x * _sigmoid(x)


def _softplus(x):
    return jnp.maximum(x, 0.0) + jnp.log1p(jnp.exp(-jnp.abs(x)))


def _split3(x):
    hi = x.astype(BF16)
    r1 = x - hi.astype(F32)
    mid = r1.astype(BF16)
    return hi, mid, (r1 - mid.astype(F32)).astype(BF16)


def _seg_sum(x, seg=HEAD):
    n = x.shape[-1]
    i = lax.broadcasted_iota(jnp.int32, (3 * n, n), 0) % n // seg
    j = lax.broadcasted_iota(jnp.int32, (3 * n, n), 1) // seg
    return _dot01(x, i == j)


def _dot01(x, mask3):
    return jnp.dot(jnp.concatenate(_split3(x), axis=-1), jnp.where(mask3, 1.0, 0.0).astype(BF16),
                   preferred_element_type=F32)


def _rmsnorm_mod(x, g, sc, sh):
    h = x * lax.rsqrt(jnp.mean(x * x, axis=-1, keepdims=True) + RMS_EPS) * g
    return h * (1.0 + sc) + sh


def _ada_kernel(c_ref, w_ref, b_ref, o_ref):
    a = _silu(c_ref[...]).astype(BF16)
    o_ref[...] = jnp.dot(a, w_ref[...].astype(BF16), preferred_element_type=F32) + b_ref[...]


def _ada(c_all, w_ada, b_ada):
    depth, d, n = w_ada.shape
    rows = c_all.shape[0]
    tn = n // 4
    return pl.pallas_call(
        _ada_kernel,
        grid=(depth, n // tn),
        in_specs=[pl.BlockSpec((rows, d), lambda l, j: (0, 0)),
                  pl.BlockSpec((None, d, tn), lambda l, j: (l, 0, j)),
                  pl.BlockSpec((None, 1, tn), lambda l, j: (l, 0, j))],
        out_specs=pl.BlockSpec((None, rows, tn), lambda l, j: (l, 0, j)),
        out_shape=jax.ShapeDtypeStruct((depth, rows, n), F32),
        compiler_params=_cparams(("parallel", "parallel")),
        name="ada_mod",
    )(c_all, w_ada, b_ada.reshape(depth, 1, n))


def _row_spec(tm, width, col=0):
    return pl.BlockSpec((None, tm, width), lambda g, i: (g, i, col))


def _mod_spec(mod, tm, d, idx):
    if mod.shape[1] == 1:
        return pl.BlockSpec((None, 1, d), lambda g, i: (g, 0, idx))
    return pl.BlockSpec((None, tm, d), lambda g, i: (g, i, idx))


def _const_spec(a):
    nd = a.ndim
    return pl.BlockSpec(a.shape, lambda g, i: (0,) * nd)


def _resident_spec(a):
    nd = a.ndim
    return pl.BlockSpec(a.shape, lambda g, i: (0,) * nd, pipeline_mode=pl.Buffered(1))


def _in_proj_kernel(x_ref, g_ref, sc_ref, sh_ref, *rest, n_w, epilogues, n_extra):
    ws, rest = rest[:n_w], rest[n_w:]
    extra, outs = rest[:n_extra], rest[n_extra:]
    h = _rmsnorm_mod(x_ref[...], g_ref[...], sc_ref[...], sh_ref[...]).astype(BF16)
    for w_ref, (epilogue, n_out) in zip(ws, epilogues):
        epilogue(jnp.dot(h, w_ref[...], preferred_element_type=F32), extra, outs[:n_out])
        outs = outs[n_out:]


def _split_epilogue(y, extra, outs):
    col = 0
    for o in outs:
        wd = o.shape[-1]
        o[...] = y[:, col:col + wd]
        col += wd


def _hg_gates(zf, lb, log_lb, log_1mlb, log_space):
    k = (1.0 - lb) * _sigmoid(-zf)
    if not log_space:
        return lb + (1.0 - lb) * _sigmoid(zf), k
    b = log_1mlb - _softplus(-zf)
    return jnp.maximum(log_lb, b) + jnp.log1p(jnp.exp(-jnp.abs(log_lb - b))), k


def _hg_epilogue(y, extra, outs):
    lb = extra[0][...]
    q_ref, w_ref, k_ref, i_ref, g_ref = outs
    c = lb.shape[-1]
    q_ref[...] = y[:, 0:c]
    w_ref[...], k_ref[...] = _hg_gates(y[:, c:2 * c], lb, None, None, False)
    i_ref[...] = y[:, 2 * c:3 * c]
    g_ref[...] = y[:, 3 * c:4 * c]


def _in_proj(x, g, mod, ws, epilogues, extra, out_widths, tm, name):
    grp, rows, d = x.shape
    kern = functools.partial(_in_proj_kernel, n_w=len(ws), epilogues=epilogues, n_extra=len(extra))
    return pl.pallas_call(
        kern,
        grid=(grp, rows // tm),
        in_specs=[_row_spec(tm, d), _const_spec(g), _mod_spec(mod, tm, d, 1), _mod_spec(mod, tm, d, 0)]
        + [_resident_spec(w) for w in ws] + [_const_spec(e) for e in extra],
        out_specs=[_row_spec(tm, wd) for wd in out_widths],
        out_shape=[jax.ShapeDtypeStruct((grp, rows, wd), F32) for wd in out_widths],
        compiler_params=_cparams(("parallel", "parallel")),
        name=name,
    )(x, g, mod, mod, *ws, *extra)


def _shift_rows(cur, prev8, j):
    rolled = pltpu.roll(cur, j, 0)
    row8 = lax.broadcasted_iota(jnp.int32, (SUBLANES, 1), 0)
    head = jnp.where(row8 < j, pltpu.roll(prev8, j, 0), rolled[0:SUBLANES])
    if cur.shape[0] == SUBLANES:
        return head
    return jnp.concatenate([head, rolled[SUBLANES:]], axis=0)


def _rw_math(p, prev, mu, w0, a0, kk_s, ka_s, rk, w2, a2, g2, log_space):
    c = w0.shape[-1]
    lo = w2.shape[0]
    mixd = p + (prev - p) * mu
    r = mixd[:, 0:c]
    k0 = mixd[:, c:2 * c]
    v = mixd[:, 2 * c:3 * c]
    wl = mixd[:, 3 * c:3 * c + lo]
    al = mixd[:, 3 * c + lo:3 * c + 2 * lo]
    gl = mixd[:, 3 * c + 2 * lo:3 * c + 3 * lo]
    z = -(w0 + jnp.dot(jnp.tanh(wl).astype(BF16), w2, preferred_element_type=F32))
    w = -jnp.exp(-_softplus(z) - 0.5)
    if not log_space:
        w = jnp.exp(w)
    a = _sigmoid(a0 + jnp.dot(al.astype(BF16), a2, preferred_element_type=F32))
    g = jnp.dot(_sigmoid(gl).astype(BF16), g2, preferred_element_type=F32)
    kk = k0 * kk_s
    kk = kk * lax.rsqrt(jnp.maximum(_seg_sum(kk * kk), 1e-24))
    k = k0 * (1.0 + (a - 1.0) * ka_s)
    bv = _seg_sum(r * k * rk) * v
    return r, w, k, v, -kk, kk * a, g, bv


def _rw_pre_step_kernel(p_ref, prev_ref, *rest):
    params, outs = rest[:9], rest[9:]
    res = _rw_math(p_ref[...], prev_ref[...], *[a[...] for a in params], False)
    for o, val in zip(outs, res):
        o[...] = val


def _rw_pre_step(p, prev, params, tm):
    grp, rows, cp = p.shape
    c = params[1].shape[-1]
    return pl.pallas_call(
        _rw_pre_step_kernel, grid=(grp, rows // tm),
        in_specs=[_row_spec(tm, cp), _row_spec(tm, cp)] + [_const_spec(a) for a in params],
        out_specs=[_row_spec(tm, c)] * 8, out_shape=[jax.ShapeDtypeStruct((grp, rows, c), F32)] * 8,
        compiler_params=_cparams(("parallel", "parallel")),
        name="rw_pre_step")(p, prev, *params)


def _mb_math(cur, s1, s2, s3, z, dtr, cw, cb, dtb, a_exp, a_pad, chunked):
    n_x = z.shape[-1]
    conv = s3 * cw[0:1] + s2 * cw[1:2] + s1 * cw[2:3] + cur * cw[3:4] + cb
    act = _silu(conv)
    mx = act[:, 0:n_x]
    half = (act.shape[-1] - n_x) // 2
    mb = act[:, n_x:n_x + half]
    mc = act[:, n_x + half:]
    dt = _softplus(dtr + dtb)
    nd = dt.shape[-1]
    i = lax.broadcasted_iota(jnp.int32, (3 * nd, n_x), 0) % nd
    j = lax.broadcasted_iota(jnp.int32, (3 * nd, n_x), 1)
    dte = _dot01(dt, j // HEAD == i)
    if chunked:
        return dt * a_pad, mx * dte, act[:, n_x:], mx, _silu(z)
    i = lax.broadcasted_iota(jnp.int32, (3 * half, 2 * half), 0) % half
    j = lax.broadcasted_iota(jnp.int32, (3 * half, 2 * half), 1)
    rep = i == (j // LANES) * HEAD + j % HEAD
    return jnp.exp(dte * a_exp), mx * dte, _dot01(mb, rep), _dot01(mc, rep), mx, _silu(z)


def _mb_pre_step_kernel(xbc_ref, s1_ref, s2_ref, s3_ref, z_ref, dt_ref, cw, cb, dtb, a_exp, a_pad, *outs):
    res = _mb_math(xbc_ref[...], s1_ref[...], s2_ref[...], s3_ref[...], z_ref[...], dt_ref[...],
                   cw[...], cb[...], dtb[...], a_exp[...], a_pad[...], False)
    for o, val in zip(outs, res):
        o[...] = val


def _mb_pre_step(xbc, z, dtr, conv_rows, params, tm):
    grp, rows, cx = xbc.shape
    n_x = z.shape[-1]
    n_k = cx - n_x
    widths = [n_x, n_x, n_k, n_k, n_x, n_x]
    return pl.pallas_call(
        _mb_pre_step_kernel, grid=(grp, rows // tm),
        in_specs=[_row_spec(tm, cx)] * 4 + [_row_spec(tm, n_x), _row_spec(tm, dtr.shape[-1])]
        + [_const_spec(a) for a in params],
        out_specs=[_row_spec(tm, wd) for wd in widths],
        out_shape=[jax.ShapeDtypeStruct((grp, rows, wd), F32) for wd in widths],
        compiler_params=_cparams(("parallel", "parallel")),
        name="mb_pre_step")(xbc, *conv_rows, z, dtr, *params)


def _hg_finish(o, g, hg_g):
    return o * lax.rsqrt(_seg_sum(o * o) * (1.0 / HEAD) + RMS_EPS) * hg_g * _silu(g)


def _mb_finish(o, mx, zs, d_exp, mb_g, n_grp):
    y = (o + d_exp * mx) * zs
    gw = y.shape[-1] // n_grp
    return jnp.concatenate(
        [y[:, g * gw:(g + 1) * gw]
         * lax.rsqrt(jnp.mean(y[:, g * gw:(g + 1) * gw] ** 2, axis=-1, keepdims=True) + RMS_EPS)
         for g in range(n_grp)], axis=-1) * mb_g


def _rw_finish(o, g, bv, ln_g, ln_b):
    mu = _seg_sum(o) * (1.0 / HEAD)
    dlt = o - mu
    var = _seg_sum(dlt * dlt) * (1.0 / HEAD)
    return (dlt * lax.rsqrt(var + RW_GN_EPS) * ln_g + ln_b + bv) * g


def _mix_finish_kernel(ohg_ref, ghg_ref, omb_ref, mx_ref, zs_ref, orw_ref, grw_ref, bv_ref,
                       hg_g, d_exp, mb_g, ln_g, ln_b, a_ref, y_ref, c_ref, *, n_grp):
    a_ref[...] = _hg_finish(ohg_ref[...], ghg_ref[...], hg_g[...]).astype(BF16)
    y_ref[...] = _mb_finish(omb_ref[...], mx_ref[...], zs_ref[...], d_exp[...], mb_g[...], n_grp).astype(BF16)
    c_ref[...] = _rw_finish(orw_ref[...], grw_ref[...], bv_ref[...], ln_g[...], ln_b[...]).astype(BF16)


def _mix_finish(acts, params, tm, n_grp):
    grp, rows, _ = acts[0].shape
    widths = [acts[0].shape[-1], acts[2].shape[-1], acts[5].shape[-1]]
    return pl.pallas_call(
        functools.partial(_mix_finish_kernel, n_grp=n_grp),
        grid=(grp, rows // tm),
        in_specs=[_row_spec(tm, a.shape[-1]) for a in acts] + [_const_spec(a) for a in params],
        out_specs=[_row_spec(tm, wd) for wd in widths],
        out_shape=[jax.ShapeDtypeStruct((grp, rows, wd), BF16) for wd in widths],
        compiler_params=_cparams(("parallel", "parallel")),
        name="mix_finish",
    )(*acts, *params)


def _block_diag_mask():
    i = lax.broadcasted_iota(jnp.int32, (LANES, LANES), 0) // HEAD
    j = lax.broadcasted_iota(jnp.int32, (LANES, LANES), 1) // HEAD
    return i == j


def _pair_state_load(s0_ref, s_scr, nhp):
    for p in range(nhp):
        s_scr[p] = jnp.zeros((LANES, LANES), F32)
        s_scr[p, 0:HEAD, 0:HEAD] = s0_ref[2 * p]
        s_scr[p, HEAD:LANES, HEAD:LANES] = s0_ref[2 * p + 1]


def _pair_state_store(s_scr, so_ref, nhp):
    for p in range(nhp):
        so_ref[2 * p] = s_scr[p, 0:HEAD, 0:HEAD]
        so_ref[2 * p + 1] = s_scr[p, HEAD:LANES, HEAD:LANES]


def _chunk_call(kern, proj, carry0, params, s0, out_width, tb, nb, scratch, name):
    grp, rows, c = proj.shape
    nb = min(nb, grp)
    sspec = pl.BlockSpec((nb,) + s0.shape[1:], lambda g, i: (g, 0, 0, 0))
    in_specs = [pl.BlockSpec((nb, tb, c), lambda g, i: (g, i, 0))]
    args = [proj]
    if carry0 is not None:
        in_specs.append(pl.BlockSpec((nb,) + carry0.shape[1:], lambda g, i: (g, 0, 0)))
        args.append(carry0)
    return pl.pallas_call(
        functools.partial(kern, nb=nb), grid=(grp // nb, rows // tb),
        in_specs=in_specs + [_const_spec(a) for a in params] + [sspec],
        out_specs=[pl.BlockSpec((nb, tb, out_width), lambda g, i: (g, i, 0)), sspec],
        out_shape=[jax.ShapeDtypeStruct((grp, rows, out_width), BF16), jax.ShapeDtypeStruct(s0.shape, F32)],
        scratch_shapes=[s(nb) for s in scratch],
        compiler_params=_cparams(("parallel", "arbitrary")),
        name=name,
    )(*args, *params, s0)


def _ssd_chunk_kernel(p_ref, c8_ref, cw, cb, dtb, a_exp, a_pad, d_exp, mb_g, s0_ref, y_ref, so_ref,
                      h_scr, carry_scr, y_scr, *, nb, nh, ng, n_x):
    c = pl.program_id(1)

    @pl.when(c == 0)
    def _():
        h_scr[...] = s0_ref[0]
        carry_scr[...] = c8_ref[0]

    n = p_ref.shape[1]
    n_bc = 2 * ng * HEAD
    cur = p_ref[0, :, 0:n_x + n_bc]
    prev8 = carry_scr[...]
    dta, xdt, bc, mx, zs = _mb_math(
        cur, _shift_rows(cur, prev8, 1), _shift_rows(cur, prev8, 2), _shift_rows(cur, prev8, 3),
        p_ref[0, :, n_x + n_bc:2 * n_x + n_bc], p_ref[0, :, 2 * n_x + n_bc:],
        cw[...], cb[...], dtb[...], a_exp[...], a_pad[...], True)
    carry_scr[...] = cur[n - SUBLANES:]

    row = lax.broadcasted_iota(jnp.int32, (n, n), 0)
    col = lax.broadcasted_iota(jnp.int32, (n, n), 1)
    tril = row >= col
    acum = _cumsum_rows(dta)
    acum_t = acum.T
    rg = nh // ng
    bgs = [bc[:, g * HEAD:(g + 1) * HEAD].astype(BF16) for g in range(ng)]
    cgs = [bc[:, (ng + g) * HEAD:(ng + g + 1) * HEAD].astype(BF16) for g in range(ng)]
    cbs = [lax.dot_general(cg, bg, _NT, preferred_element_type=F32) for cg, bg in zip(cgs, bgs)]
    work = []
    for h in range(nh):
        g = h // rg
        a_col = acum[:, h:h + 1]
        a_tot = acum[n - 1:n, h:h + 1]
        lmat = jnp.exp(jnp.where(tril, a_col - acum_t[h:h + 1, :], -jnp.inf))
        xh = xdt[:, h * HEAD:(h + 1) * HEAD]
        work.append(dict(h=h, g=g, a_col=a_col, a_tot=a_tot, sh=h_scr[h], xh=xh.astype(BF16),
                         gm=(cbs[g] * lmat).astype(BF16), xd=(xh * jnp.exp(a_tot - a_col)).astype(BF16)))
    for q in work:
        q['intra'] = jnp.dot(q['gm'], q['xh'], preferred_element_type=F32)
        q['inter'] = lax.dot_general(cgs[q['g']], q['sh'].astype(BF16), _NT, preferred_element_type=F32)
        q['upd'] = lax.dot_general(q['xd'], bgs[q['g']], _TN, preferred_element_type=F32)
    for q in work:
        h = q['h']
        y_scr[:, h * HEAD:(h + 1) * HEAD] = q['intra'] + jnp.exp(q['a_col']) * q['inter']
        h_scr[h] = jnp.exp(q['a_tot']) * q['sh'] + q['upd']
    y_ref[0] = _mb_finish(y_scr[...], mx, zs, d_exp[...], mb_g[...], ng).astype(BF16)

    @pl.when(c == pl.num_programs(1) - 1)
    def _():
        so_ref[0] = h_scr[...]


def _ssd_chunked(proj, conv8, params, s0, n_x, ng):
    nh = s0.shape[1]
    tb = min(SSD_CHUNK, proj.shape[1])
    scratch = [lambda nb: pltpu.VMEM(s0.shape[1:], F32),
               lambda nb: pltpu.VMEM((SUBLANES, conv8.shape[-1]), F32),
               lambda nb: pltpu.VMEM((tb, n_x), F32)]
    return _chunk_call(functools.partial(_ssd_chunk_kernel, nh=nh, ng=ng, n_x=n_x), proj, conv8, params, s0, n_x,
                       tb, 1, scratch, "ssd_chunk")


def _hg_chunk_kernel(p_ref, lb_ref, llb_ref, l1m_ref, hg_g, s0_ref, a_ref, so_ref, s_scr, o_scr,
                     *, nb, nhp, sub):
    c = pl.program_id(1)

    @pl.when(c == 0)
    def _():
        for s in range(nb):
            _pair_state_load(s0_ref.at[s], s_scr.at[s], nhp)

    tb = p_ref.shape[1]
    cw = lb_ref.shape[-1]
    pos = lax.broadcasted_iota(jnp.int32, (tb, 1), 0) % sub
    bd = _block_diag_mask()
    seg = jnp.where(bd, 1.0, 0.0).astype(BF16)
    causal = (lax.broadcasted_iota(jnp.int32, (sub, sub, 1), 1) <= lax.broadcasted_iota(jnp.int32, (sub, sub, 1), 0))
    bs, ks = [], []
    for s in range(nb):
        b, k = _hg_gates(p_ref[s, :, cw:2 * cw], lb_ref[...], llb_ref[...], l1m_ref[...], True)
        step = 1
        while step < sub:
            b = b + jnp.where(pos >= step, pltpu.roll(b, step, 0), 0.0)
            step *= 2
        bs.append(b)
        ks.append(k)
    probs = [(s, p) for s in range(nb) for p in range(nhp)]
    states = {sp: s_scr[sp[0], sp[1]] for sp in probs}
    for ci in range(tb // sub):
        rs = slice(ci * sub, (ci + 1) * sub)
        work = []
        for s, p in probs:
            ls = slice(p * LANES, (p + 1) * LANES)
            bc = bs[s][rs, ls]
            kc = ks[s][rs, ls]
            qc = p_ref[s, rs, p * LANES:(p + 1) * LANES]
            vc = p_ref[s, rs, 2 * cw + p * LANES:2 * cw + (p + 1) * LANES]
            btot = bc[sub - 1:sub]
            o = lax.dot_general((qc * jnp.exp(bc)).astype(BF16), states[s, p].astype(BF16), _NT,
                                preferred_element_type=F32)
            e = jnp.exp(jnp.where(causal, bc[:, None, :] - bc[None, :, :], -jnp.inf))
            x = (e * kc[None, :, :] * qc[:, None, :]).astype(BF16).reshape(sub * sub, LANES)
            att = jnp.dot(x, seg, preferred_element_type=F32).reshape(sub, sub, LANES)
            kd = (kc * jnp.exp(btot - bc)).astype(BF16)
            upd = lax.dot_general(vc.astype(BF16), kd, _TN, preferred_element_type=F32)
            work.append((s, p, ls, o, att, vc, jnp.exp(btot), upd))
        for s, p, ls, o, att, vc, dec, upd in work:
            o_scr[s, rs, ls] = o + jnp.sum(att * vc[None, :, :], axis=1)
            states[s, p] = states[s, p] * dec + jnp.where(bd, upd, 0.0)
    for s, p in probs:
        s_scr[s, p] = states[s, p]
    for s in range(nb):
        a_ref[s] = _hg_finish(o_scr[s], p_ref[s, :, 3 * cw:4 * cw], hg_g[...]).astype(BF16)

    @pl.when(c == pl.num_programs(1) - 1)
    def _():
        for s in range(nb):
            _pair_state_store(s_scr.at[s], so_ref.at[s], nhp)


def _hg_chunked(proj, params, s0):
    nhp = s0.shape[1] // 2
    tb = min(HG_BLOCK, proj.shape[1])
    cw = proj.shape[-1] // 4
    scratch = [lambda nb: pltpu.VMEM((nb, nhp, LANES, LANES), F32), lambda nb: pltpu.VMEM((nb, tb, cw), F32)]
    return _chunk_call(functools.partial(_hg_chunk_kernel, nhp=nhp, sub=HG_SUB), proj, None, params, s0, cw, tb,
                       SEQ_PER_STEP, scratch, "hg_chunk")


def _split2(x):
    hi = x.astype(BF16)
    return hi, (x - hi.astype(F32)).astype(BF16)


def _dot3(a, b, dims=_NN):
    ah, am = _split2(a)
    bh, bm = _split2(b)
    (ca,), (cb,) = dims[0]
    return lax.dot_general(jnp.concatenate([ah, ah, am], axis=ca), jnp.concatenate([bh, bm, bh], axis=cb), dims,
                           preferred_element_type=F32)


def _cumsum_rows(x):
    n = x.shape[0]
    tril3 = (lax.broadcasted_iota(jnp.int32, (n, 3 * n), 0) >= lax.broadcasted_iota(jnp.int32, (n, 3 * n), 1) % n)
    return jnp.dot(jnp.where(tril3, 1.0, 0.0).astype(BF16), jnp.concatenate(_split3(x), axis=0),
                   preferred_element_type=F32)


def _tri_inverse(n_mats, blk, top):
    n = n_mats[0].shape[0]
    ri = lax.broadcasted_iota(jnp.int32, (n, n), 0)
    ci = lax.broadcasted_iota(jnp.int32, (n, n), 1)
    eye = jnp.where(ri == ci, 1.0, 0.0)
    pws = [jnp.where(ri // blk == ci // blk, m, 0.0) for m in n_mats]
    ts = [eye + pw for pw in pws]
    s = 2
    while s < blk:
        pws = [_dot3(pw, pw) for pw in pws]
        ts = [t + _dot3(t, pw) for t, pw in zip(ts, pws)]
        s *= 2
    size = blk
    while size < top:
        in_2blk = ri // (2 * size) == ci // (2 * size)
        in_blk = ri // size == ci // size
        offs = [jnp.where(in_2blk, m, 0.0) - jnp.where(in_blk, m, 0.0) for m in n_mats]
        mids = [_dot3(t, off) for t, off in zip(ts, offs)]
        ts = [t + _dot3(mid, t) for t, mid in zip(ts, mids)]
        size *= 2
    return ts


def _rw_chunk_kernel(p_ref, c8_ref, mu, w0, a0, kk_s, ka_s, rk, w2, a2, g2, ln_g, ln_b, s0_ref, c_ref, so_ref,
                     s_scr, carry_scr, *, nb, nhp):
    c = pl.program_id(1)

    @pl.when(c == 0)
    def _():
        carry_scr[...] = c8_ref[...]
        for s in range(nb):
            _pair_state_load(s0_ref.at[s], s_scr.at[s], nhp)

    n = p_ref.shape[1]
    n2 = 2 * n
    ri = lax.broadcasted_iota(jnp.int32, (n2, n2), 0)
    ci = lax.broadcasted_iota(jnp.int32, (n2, n2), 1)
    tril = ri >= ci
    strict = ri > ci
    m0 = lax.broadcasted_iota(jnp.int32, (1, LANES), 1) < HEAD
    bd = _block_diag_mask()
    params = [a[...] for a in (mu, w0, a0, kk_s, ka_s, rk, w2, a2, g2)]

    def stack(x):
        return jnp.concatenate([jnp.where(m0, x, 0.0), jnp.where(m0, 0.0, x)], axis=0)

    probs = []
    fin = []
    for s in range(nb):
        cur = p_ref[s]
        r, lw, k, v, a, b, g, bv = _rw_math(cur, _shift_rows(cur, carry_scr[s], 1), *params, True)
        carry_scr[s] = cur[n - SUBLANES:]
        fin.append((g, bv))
        cum = _cumsum_rows(lw)
        e_n = jnp.exp(-cum)
        ctot = cum[n - 1:n]
        e_rel = jnp.exp(ctot - cum)
        at = a * jnp.exp(cum - lw)
        rt = r * jnp.exp(cum)
        bt = b * e_n
        kt = k * e_n
        bh = b * e_rel
        kh = k * e_rel
        for p in range(nhp):
            ls = slice(p * LANES, (p + 1) * LANES)
            probs.append(dict(
                s=s, p=p, ls=ls, vp=v[:, ls], dec=jnp.exp(ctot[:, ls]),
                ar=jnp.concatenate([stack(at[:, ls]), stack(rt[:, ls])], axis=0),
                bk=jnp.concatenate([stack(bt[:, ls]), stack(kt[:, ls])], axis=0),
                bkh=jnp.concatenate([bh[:, ls], kh[:, ls]], axis=0)))
    for q in probs:
        q['st'] = s_scr[q['s'], q['p']]
        q['a_s'] = _dot3(q['ar'], q['st'], _NT)
    for q in probs:
        q['g'] = _dot3(q['ar'], q['bk'], _NT)
    t_invs = _tri_inverse([jnp.where(strict, q['g'][0:n2, 0:n2], 0.0) for q in probs], RW_INV_BLOCK, n)
    for q in probs:
        q['v2'] = stack(q['vp'])
        q['rhs'] = q['a_s'][0:n2] + _dot3(jnp.where(strict, q['g'][0:n2, n2:], 0.0), q['v2'])
    for q, t_inv in zip(probs, t_invs):
        q['u2'] = _dot3(t_inv, q['rhs'])
    outs = {}
    for q in probs:
        g = q['g']
        pm = jnp.concatenate([jnp.where(tril, g[n2:, 0:n2], 0.0), jnp.where(tril, g[n2:, n2:], 0.0)], axis=1)
        o2 = q['a_s'][n2:] + _dot3(pm, jnp.concatenate([q['u2'], q['v2']], axis=0))
        outs[q['s'], q['p']] = o2[0:n] + o2[n:]
    for q in probs:
        u2 = q['u2']
        upd = _dot3(jnp.concatenate([u2[0:n] + u2[n:], q['vp']], axis=0), q['bkh'], _TN)
        s_scr[q['s'], q['p']] = q['st'] * q['dec'] + jnp.where(bd, upd, 0.0)
    for s in range(nb):
        o = jnp.concatenate([outs[s, p] for p in range(nhp)], axis=-1)
        c_ref[s] = _rw_finish(o, fin[s][0], fin[s][1], ln_g[...], ln_b[...]).astype(BF16)

    @pl.when(c == pl.num_programs(1) - 1)
    def _():
        for s in range(nb):
            _pair_state_store(s_scr.at[s], so_ref.at[s], nhp)


def _rw_chunked(proj, shift8, params, s0):
    nhp = s0.shape[1] // 2
    tb = min(RW_CHUNK, proj.shape[1])
    cw = params[1].shape[-1]
    scratch = [lambda nb: pltpu.VMEM((nb, nhp, LANES, LANES), F32),
               lambda nb: pltpu.VMEM((nb, SUBLANES, proj.shape[-1]), F32)]
    return _chunk_call(functools.partial(_rw_chunk_kernel, nhp=nhp), proj, shift8, params, s0, cw, tb,
                       RW_SEQ_PER_STEP, scratch, "rw_chunk")


def _pair_mask():
    lane = lax.broadcasted_iota(jnp.int32, (1, LANES), 1)
    return lane, lane < HEAD


def _seg_cols(p, m0):
    c0 = jnp.sum(jnp.where(m0, p, 0.0), axis=-1, keepdims=True)
    c1 = jnp.sum(jnp.where(m0, 0.0, p), axis=-1, keepdims=True)
    return c0, c1


def _value_cols(vt, mt, m0):
    col = jnp.sum(jnp.where(mt, vt, 0.0), axis=-1, keepdims=True)
    return jnp.where(m0, col[0:HEAD], col[HEAD:2 * HEAD])


def _store_cols(ot_ref, row0, o0, o1, mt):
    shape = (HEAD, LANES)
    mask = jnp.broadcast_to(mt, shape)
    pltpu.store(ot_ref.at[pl.ds(row0, HEAD), :], jnp.broadcast_to(o0, shape), mask=mask)
    pltpu.store(ot_ref.at[pl.ds(row0 + HEAD, HEAD), :], jnp.broadcast_to(o1, shape), mask=mask)


def _split_rec_refs(refs, delta):
    if delta:
        return refs
    return refs[:3] + (None, None) + refs[3:]


def _rec_step_kernel(*refs, nb, nhp, delta, divs):
    w_ref, k_ref, r_ref, nkk_ref, kka_ref, vt_ref, s0_ref, ot_ref, so_ref = _split_rec_refs(refs, delta)
    dw, dk, dr = divs
    i = pl.program_id(0)
    lane, m0 = _pair_mask()

    def row(ref, b, hp, d):
        return ref[b:b + 1, (hp // d) * LANES:(hp // d + 1) * LANES]

    probs = [(b, hp) for b in range(nb) for hp in range(nhp)]
    mts = [lane == i * nb + b for b in range(nb)]
    ss = [jnp.concatenate([s0_ref[b, 2 * hp], s0_ref[b, 2 * hp + 1]], axis=-1) for b, hp in probs]
    vbs = [_value_cols(vt_ref[hp * LANES:(hp + 1) * LANES, :], mts[b], m0) for b, hp in probs]
    if delta:
        cols = [_seg_cols(s * row(nkk_ref, b, hp, 1), m0) for s, (b, hp) in zip(ss, probs)]
        ss = [s * row(w_ref, b, hp, dw) + jnp.where(m0, c0, c1) * row(kka_ref, b, hp, 1) + vb * row(k_ref, b, hp, dk)
              for s, (c0, c1), vb, (b, hp) in zip(ss, cols, vbs, probs)]
    else:
        ss = [s * row(w_ref, b, hp, dw) + vb * row(k_ref, b, hp, dk) for s, vb, (b, hp) in zip(ss, vbs, probs)]
    outs = [_seg_cols(s * row(r_ref, b, hp, dr), m0) for s, (b, hp) in zip(ss, probs)]
    for s, (o0, o1), (b, hp) in zip(ss, outs, probs):
        so_ref[b, 2 * hp] = s[:, 0:HEAD]
        so_ref[b, 2 * hp + 1] = s[:, HEAD:2 * HEAD]
        _store_cols(ot_ref, hp * LANES, o0, o1, mts[b])


def _recurrence(rows_in, v, s0, divs):
    delta = len(rows_in) == 5
    grp, rows, c = v.shape
    ns, nh = s0.shape[:2]
    nhp = nh // 2
    state_shape = jax.ShapeDtypeStruct(s0.shape, F32)
    assert grp == 1 and rows == ns and rows <= LANES
    nb = SUBLANES
    pad = LANES - rows
    vt = jnp.pad(v[0].T, ((0, 0), (0, pad)))
    kern = functools.partial(_rec_step_kernel, nb=nb, nhp=nhp, delta=delta, divs=divs)
    ot, s_new = pl.pallas_call(
        kern, grid=(rows // nb,),
        in_specs=[pl.BlockSpec((nb, a.shape[-1]), lambda i: (i, 0)) for a in rows_in]
        + [pl.BlockSpec((c, LANES), lambda i: (0, 0)),
           pl.BlockSpec((nb,) + s0.shape[1:], lambda i: (i, 0, 0, 0))],
        out_specs=[pl.BlockSpec((c, LANES), lambda i: (0, 0)),
                   pl.BlockSpec((nb,) + s0.shape[1:], lambda i: (i, 0, 0, 0))],
        out_shape=[jax.ShapeDtypeStruct((c, LANES), F32), state_shape],
        compiler_params=_cparams(("arbitrary",)),
        name="rec_step",
    )(*[a[0] for a in rows_in], vt, s0)
    return ot[:, :rows].T[None], s_new


def _out_proj_kernel(x_ref, gate_ref, a_ref, y_ref, c_ref, wo_ref, o_ref):
    mix_in = jnp.concatenate([a_ref[...], y_ref[...], c_ref[...]], axis=-1)
    o_ref[...] = x_ref[...] + gate_ref[...] * jnp.dot(mix_in, wo_ref[...], preferred_element_type=F32)


def _out_proj(x, mod, acts, wo, tm):
    grp, rows, d = x.shape
    return pl.pallas_call(
        _out_proj_kernel,
        grid=(grp, rows // tm),
        in_specs=[_row_spec(tm, d), _mod_spec(mod, tm, d, 2)] + [_row_spec(tm, a.shape[-1]) for a in acts]
        + [_resident_spec(wo)],
        out_specs=_row_spec(tm, d),
        out_shape=jax.ShapeDtypeStruct(x.shape, F32),
        compiler_params=_cparams(("parallel", "parallel")),
        name="out_proj",
    )(x, mod, *acts, wo)


def _ffn_kernel(x_ref, g_ref, sc_ref, sh_ref, gate_ref, wg_ref, wu_ref, wd_ref, gf_ref, o_ref, u_scr,
                *, tf, final_norm):
    x = x_ref[...]
    h = _rmsnorm_mod(x, g_ref[...], sc_ref[...], sh_ref[...]).astype(BF16)
    for f in range(wg_ref.shape[-1] // tf):
        fs = slice(f * tf, (f + 1) * tf)
        u = _silu(jnp.dot(h, wg_ref[:, fs], preferred_element_type=F32)) * jnp.dot(h, wu_ref[:, fs],
                                                                                   preferred_element_type=F32)
        u_scr[:, fs] = u.astype(BF16)
    y = x + gate_ref[...] * jnp.dot(u_scr[...], wd_ref[...], preferred_element_type=F32)
    if final_norm:
        y = y * lax.rsqrt(jnp.mean(y * y, axis=-1, keepdims=True) + RMS_EPS) * gf_ref[...]
    o_ref[...] = y


def _ffn(x, g, mod, wg, wu, wd, g_final, tm, tf, final_norm):
    grp, rows, d = x.shape
    nf = wg.shape[-1]
    return pl.pallas_call(
        functools.partial(_ffn_kernel, tf=tf, final_norm=final_norm),
        grid=(grp, rows // tm),
        in_specs=[_row_spec(tm, d), _const_spec(g), _mod_spec(mod, tm, d, 4), _mod_spec(mod, tm, d, 3),
                  _mod_spec(mod, tm, d, 5), _resident_spec(wg), _resident_spec(wu), _resident_spec(wd),
                  _const_spec(g_final)],
        out_specs=_row_spec(tm, d),
        out_shape=jax.ShapeDtypeStruct(x.shape, F32),
        scratch_shapes=[pltpu.VMEM((tm, nf), BF16)],
        compiler_params=_cparams(("parallel", "parallel")),
        name="ffn",
    )(x, g, mod, mod, mod, wg, wu, wd, g_final)


def _pad_cols(a, width):
    return jnp.pad(a, [(0, 0)] * (a.ndim - 1) + [(0, width - a.shape[-1])])


def _pad_rows(a, height):
    return jnp.pad(a, [(0, 0)] * (a.ndim - 2) + [(0, height - a.shape[-2]), (0, 0)])


def _prep_layer(l, P, dims):
    hgw, n_x, n_bc, nh_mb, rww, lw, la, lg = dims
    w_in = P['w_in'][l]
    c0 = 4 * hgw
    c1 = c0 + n_x
    c2 = c1 + n_x + n_bc
    c3 = c2 + nh_mb
    w_hg = w_in[:, :c0].astype(BF16)
    w_mb = jnp.concatenate([w_in[:, c1:c2], w_in[:, c0:c1], _pad_cols(w_in[:, c2:c3], LANES)], axis=1).astype(BF16)
    w_rw_raw = w_in[:, c3:]

    def rw_cols(a):
        o = 3 * rww
        return jnp.concatenate([a[..., :o], _pad_cols(a[..., o:o + lw], LANES),
                                _pad_cols(a[..., o + lw:o + lw + la], LANES),
                                _pad_cols(a[..., o + lw + la:], LANES)], axis=-1)

    row = lambda a: a.reshape(1, -1).astype(F32)
    a_neg = -jnp.exp(P['mb_A_log'][l].astype(F32))
    return dict(
        w_hg=w_hg, w_mb=w_mb, w_rw=rw_cols(w_rw_raw).astype(BF16), rw_cols=rw_cols,
        g1=row(P['g_norm1'][l]), g2=row(P['g_norm2'][l]),
        rw_params=(rw_cols(row(P['rw_mu'][l])), row(P['rw_w0'][l]), row(P['rw_a0'][l]), row(P['rw_k_k'][l]),
                   row(P['rw_k_a'][l]), row(P['rw_r_k'][l]),
                   _pad_rows(P['rw_w2'][l], LANES).astype(BF16), _pad_rows(P['rw_a2'][l], LANES).astype(BF16),
                   _pad_rows(P['rw_g2'][l], LANES).astype(BF16)),
        mb_params=(P['mb_conv_w'][l].astype(F32), row(P['mb_conv_b'][l]), _pad_cols(row(P['mb_dt_bias'][l]), LANES),
                   row(jnp.repeat(a_neg, HEAD)), _pad_cols(row(a_neg), LANES)),
        out_params=(row(P['hg_norm_g'][l]), row(jnp.repeat(P['mb_D'][l], HEAD)), row(P['mb_norm_g'][l]),
                    row(P['rw_ln_g'][l]), row(P['rw_ln_b'][l])),
        wo=P['w_out'][l].astype(BF16), wg=P['w_gate'][l].astype(BF16), wu=P['w_up'][l].astype(BF16),
        wd=P['w_down'][l].astype(BF16),
    )


def _trunk(x, mods, st_hg, st_ssm, st_conv, st_wkv, st_shift, lb_all, layers, g_final, dims, seq, tm, tf):
    hgw, n_x, n_bc, nh_mb, rww, lw, la, lg = dims
    grp, rows, d = x.shape
    depth = len(layers)
    n_grp = n_bc // (2 * HEAD)
    outs = ([], [], [], [], [])
    for l in range(depth):
        L = layers[l]
        mod = mods[l]
        lb = lb_all[l].reshape(1, -1)
        hg_g, d_exp, mb_g, ln_g, ln_b = L['out_params']
        s_hg_in = jnp.swapaxes(st_hg[l], -1, -2)
        cw_rw = L['w_rw'].shape[-1]
        if seq:
            p_hg, p_mb, rw_p = _in_proj(
                x, L['g1'], mod, [L['w_hg'], L['w_mb'], L['w_rw']], [(_split_epilogue, 1)] * 3, [],
                [4 * hgw, 2 * n_x + n_bc + LANES, cw_rw], tm, "in_proj")
            shift8 = jnp.pad(L['rw_cols'](st_shift[l])[:, None, :], ((0, 0), (SUBLANES - 1, 0), (0, 0)))
            conv8 = jnp.pad(st_conv[l], ((0, 0), (SUBLANES - st_conv[l].shape[1], 0), (0, 0)))
            conv_new = p_mb[:, rows - 3:, :n_x + n_bc]
            shift_new = rw_p[:, rows - 1, :]
            a_hg, s_hg = _hg_chunked(p_hg, (lb, jnp.log(lb), jnp.log1p(-lb), hg_g), s_hg_in)
            y_mb, s_ssm = _ssd_chunked(p_mb, conv8, L['mb_params'] + (d_exp, mb_g), st_ssm[l], n_x, n_grp)
            c_rw, s_wkv = _rw_chunked(rw_p, shift8, L['rw_params'] + (ln_g, ln_b), st_wkv[l])
        else:
            q, w_hg, k_hg, i_hg, g_hg, xbc, z, dtr, rw_p = _in_proj(
                x, L['g1'], mod, [L['w_hg'], L['w_mb'], L['w_rw']],
                [(_hg_epilogue, 5), (_split_epilogue, 3), (_split_epilogue, 1)], [lb],
                [hgw] * 5 + [n_x + n_bc, n_x, LANES, cw_rw], tm, "in_proj_step")
            conv_new = jnp.concatenate([st_conv[l][:, 1:, :], xbc[0][:, None, :]], axis=1)
            shift_new = rw_p[0]
            r_rw, w_rw, k_rw, v_rw, nkk, kka, g_rw, bv = _rw_pre_step(
                rw_p, L['rw_cols'](st_shift[l])[None], L['rw_params'], tm)
            w_mb, v_mb, k_mb, r_mb, mx, zs = _mb_pre_step(
                xbc, z, dtr, tuple(st_conv[l][None, :, 2 - j, :] for j in range(3)), L['mb_params'], tm)
            o_hg, s_hg = _recurrence((w_hg, k_hg, q), i_hg, s_hg_in, (1, 1, 1))
            o_mb, s_ssm = _recurrence((w_mb, k_mb, r_mb), v_mb, st_ssm[l], (1, 2, 2))
            o_rw, s_wkv = _recurrence((w_rw, k_rw, r_rw, nkk, kka), v_rw, st_wkv[l], (1, 1, 1))
            a_hg, y_mb, c_rw = _mix_finish((o_hg, g_hg, o_mb, mx, zs, o_rw, g_rw, bv), L['out_params'], tm, n_grp)
        x = _out_proj(x, mod, (a_hg, y_mb, c_rw), L['wo'], tm)
        x = _ffn(x, L['g2'], mod, L['wg'], L['wu'], L['wd'], g_final, tm, tf, final_norm=(l == depth - 1))
        o3 = 3 * rww
        shift_new = jnp.concatenate([shift_new[:, :o3], shift_new[:, o3:o3 + lw],
                                     shift_new[:, o3 + LANES:o3 + LANES + la],
                                     shift_new[:, o3 + 2 * LANES:o3 + 2 * LANES + lg]], axis=-1)
        for lst, s in zip(outs, (jnp.swapaxes(s_hg, -1, -2), s_ssm, conv_new, s_wkv, shift_new)):
            lst.append(s)
    return x, [jnp.stack(lst) for lst in outs]


def kernel(x_prompt, x_sample, state_hgrn, state_ssm, state_conv, state_wkv, state_shift, c_prompt, c_sample,
           w_ada, b_ada, g_norm1, w_in, hg_lb_logits, hg_norm_g, mb_conv_w, mb_conv_b, mb_dt_bias, mb_A_log,
           mb_D, mb_norm_g, rw_mu, rw_w0, rw_w2, rw_a0, rw_a2, rw_g2, rw_k_k, rw_k_a, rw_r_k, rw_ln_g, rw_ln_b,
           w_out, g_norm2, w_gate, w_up, w_down, g_final):
    P = dict(g_norm1=g_norm1, w_in=w_in, hg_norm_g=hg_norm_g, mb_conv_w=mb_conv_w, mb_conv_b=mb_conv_b,
             mb_dt_bias=mb_dt_bias, mb_A_log=mb_A_log, mb_D=mb_D, mb_norm_g=mb_norm_g, rw_mu=rw_mu, rw_w0=rw_w0,
             rw_w2=rw_w2, rw_a0=rw_a0, rw_a2=rw_a2, rw_g2=rw_g2, rw_k_k=rw_k_k, rw_k_a=rw_k_a,
             rw_r_k=rw_r_k.reshape(rw_r_k.shape[0], -1), rw_ln_g=rw_ln_g, rw_ln_b=rw_ln_b, w_out=w_out,
             g_norm2=g_norm2, w_gate=w_gate, w_up=w_up, w_down=w_down)
    depth = w_in.shape[0]
    bsz, t_len, d = x_prompt.shape
    dec = x_sample.shape[0]
    hgw = hg_lb_logits.shape[-1]
    n_x = mb_norm_g.shape[-1]
    n_bc = mb_conv_w.shape[-1] - n_x
    nh_mb = mb_dt_bias.shape[-1]
    rww = rw_w0.shape[-1]
    lw, la, lg = rw_w2.shape[1], rw_a2.shape[1], rw_g2.shape[1]
    dims = (hgw, n_x, n_bc, nh_mb, rww, lw, la, lg)
    layers = [_prep_layer(l, P, dims) for l in range(depth)]
    gf = g_final.reshape(1, d)

    cs = jnp.cumsum(jax.nn.softmax(hg_lb_logits.astype(F32), axis=0), axis=0)
    lb_all = cs - cs[:1]

    n_c = bsz + dec
    pad_c = -n_c % (2 * SUBLANES)
    mods = _ada(jnp.pad(jnp.concatenate([c_prompt, c_sample], axis=0), ((0, pad_c), (0, 0))), w_ada, b_ada)
    mods_p = [mods[l, :bsz][:, None, :] for l in range(depth)]
    mods_s = [mods[l, bsz:n_c][None] for l in range(depth)]

    dt = x_prompt.dtype
    zeros = lambda a: jnp.zeros((depth, bsz) + a.shape[2:], dt)
    y_p, st_p = _trunk(x_prompt, mods_p, zeros(state_hgrn), zeros(state_ssm), zeros(state_conv), zeros(state_wkv),
                       zeros(state_shift), lb_all, layers, gf, dims, True, min(512, t_len), 256)
    y_s, st_s = _trunk(x_sample.reshape(1, dec, d), mods_s, state_hgrn, state_ssm, state_conv, state_wkv,
                       state_shift, lb_all, layers, gf, dims, False, dec, 256)
    hg_p, ssm_p, conv_p, wkv_p, shift_p = st_p
    hg_s, ssm_s, conv_s, wkv_s, shift_s = st_s
    return (y_p, y_s.reshape(x_sample.shape), hg_p, hg_s, ssm_p, ssm_s, conv_p, conv_s, wkv_p, wkv_s,
            shift_p, shift_s)
```

```python
import functools

import jax
import jax.numpy as jnp
from jax import lax
from jax.experimental import pallas as pl
from jax.experimental.pallas import tpu as pltpu

F32 = jnp.float32
BF16 = jnp.bfloat16

LANES = 128
SUBLANES = 8
HEAD = 64
RMS_EPS = 1e-6
RW_GN_EPS = 64e-5
VMEM_LIMIT = 48 * 1024 * 1024
SSD_CHUNK = 128
HG_BLOCK = 128
HG_SUB = 16
RW_CHUNK = 64
RW_INV_BLOCK = 16
SEQ_PER_STEP = 2
RW_SEQ_PER_STEP = 4

_NN = (((1,), (0,)), ((), ()))
_NT = (((1,), (1,)), ((), ()))
_TN = (((0,), (0,)), ((), ()))


def _cparams(sem):
    return pltpu.CompilerParams(dimension_semantics=sem, vmem_limit_bytes=VMEM_LIMIT)


def _sigmoid(x):
    return 1.0 / (1.0 + jnp.exp(-x))


def _silu(x):
    return x * _sigmoid(x)


def _softplus(x):
    return jnp.maximum(x, 0.0) + jnp.log1p(jnp.exp(-jnp.abs(x)))


def _split3(x):
    hi = x.astype(BF16)
    r1 = x - hi.astype(F32)
    mid = r1.astype(BF16)
    return hi, mid, (r1 - mid.astype(F32)).astype(BF16)


def _seg_sum(x, seg=HEAD):
    n = x.shape[-1]
    i = lax.broadcasted_iota(jnp.int32, (3 * n, n), 0) % n // seg
    j = lax.broadcasted_iota(jnp.int32, (3 * n, n), 1) // seg
    return _dot01(x, i == j)


def _dot01(x, mask3):
    return jnp.dot(jnp.concatenate(_split3(x), axis=-1), jnp.where(mask3, 1.0, 0.0).astype(BF16),
                   preferred_element_type=F32)


def _rmsnorm_mod(x, g, sc, sh):
    h = x * lax.rsqrt(jnp.mean(x * x, axis=-1, keepdims=True) + RMS_EPS) * g
    return h * (1.0 + sc) + sh


def _ada_kernel(c_ref, w_ref, b_ref, op_ref, os_ref):
    a = _silu(c_ref[...]).astype(BF16)
    y = jnp.dot(a, w_ref[...].astype(BF16), preferred_element_type=F32) + b_ref[...]
    n_p = op_ref.shape[0]
    op_ref[...] = y[0:n_p]
    os_ref[...] = y[n_p:n_p + os_ref.shape[0]]


def _ada(c_all, n_p, n_s, w_ada, b_ada):
    depth, d, n = w_ada.shape
    rows = c_all.shape[0]
    tn = n // 4
    return pl.pallas_call(
        _ada_kernel,
        grid=(depth, n // tn),
        in_specs=[pl.BlockSpec((rows, d), lambda l, j: (0, 0)),
                  pl.BlockSpec((None, d, tn), lambda l, j: (l, 0, j)),
                  pl.BlockSpec((None, 1, tn), lambda l, j: (l, 0, j))],
        out_specs=[pl.BlockSpec((None, n_p, tn), lambda l, j: (l, 0, j)),
                   pl.BlockSpec((None, n_s, tn), lambda l, j: (l, 0, j))],
        out_shape=[jax.ShapeDtypeStruct((depth, n_p, n), F32), jax.ShapeDtypeStruct((depth, n_s, n), F32)],
        compiler_params=_cparams(("parallel", "parallel")),
        name="ada_mod",
    )(c_all, w_ada, b_ada.reshape(depth, 1, n))


def _row_spec(tm, width, col=0):
    return pl.BlockSpec((None, tm, width), lambda g, i: (g, i, col))


def _mod_spec(mod, tm, d, idx):
    arr, l = mod
    if arr.shape[2] == 1:
        return pl.BlockSpec((None, None, 1, d), lambda g, i: (l, g, 0, idx))
    return pl.BlockSpec((None, None, tm, d), lambda g, i: (l, g, i, idx))


def _const_spec(a):
    nd = a.ndim
    return pl.BlockSpec(a.shape, lambda g, i: (0,) * nd)


def _resident_spec(a):
    nd = a.ndim
    return pl.BlockSpec(a.shape, lambda g, i: (0,) * nd, pipeline_mode=pl.Buffered(1))


def _in_proj_kernel(x_ref, g_ref, sc_ref, sh_ref, *rest, n_w, epilogues, n_extra):
    ws, rest = rest[:n_w], rest[n_w:]
    extra, outs = rest[:n_extra], rest[n_extra:]
    h = _rmsnorm_mod(x_ref[...], g_ref[...], sc_ref[...], sh_ref[...]).astype(BF16)
    for w_ref, (epilogue, n_out) in zip(ws, epilogues):
        epilogue(jnp.dot(h, w_ref[...], preferred_element_type=F32), extra, outs[:n_out])
        outs = outs[n_out:]


def _split_epilogue(y, extra, outs):
    col = 0
    for o in outs:
        wd = o.shape[-1]
        o[...] = y[:, col:col + wd]
        col += wd


def _hg_gates(zf, lb, log_lb, log_1mlb, log_space):
    k = (1.0 - lb) * _sigmoid(-zf)
    if not log_space:
        return lb + (1.0 - lb) * _sigmoid(zf), k
    b = log_1mlb - _softplus(-zf)
    return jnp.maximum(log_lb, b) + jnp.log1p(jnp.exp(-jnp.abs(log_lb - b))), k


def _hg_epilogue(y, extra, outs):
    lb = extra[0][...]
    q_ref, w_ref, k_ref, i_ref, g_ref = outs
    c = lb.shape[-1]
    q_ref[...] = y[:, 0:c]
    w_ref[...], k_ref[...] = _hg_gates(y[:, c:2 * c], lb, None, None, False)
    i_ref[...] = y[:, 2 * c:3 * c]
    g_ref[...] = y[:, 3 * c:4 * c]


def _in_proj(x, g, mod, ws, epilogues, extra, out_widths, tm, name):
    grp, rows, d = x.shape
    kern = functools.partial(_in_proj_kernel, n_w=len(ws), epilogues=epilogues, n_extra=len(extra))
    return pl.pallas_call(
        kern,
        grid=(grp, rows // tm),
        in_specs=[_row_spec(tm, d), _const_spec(g), _mod_spec(mod, tm, d, 1), _mod_spec(mod, tm, d, 0)]
        + [_resident_spec(w) for w in ws] + [_const_spec(e) for e in extra],
        out_specs=[_row_spec(tm, wd) for wd in out_widths],
        out_shape=[jax.ShapeDtypeStruct((grp, rows, wd), F32) for wd in out_widths],
        compiler_params=_cparams(("parallel", "parallel")),
        name=name,
    )(x, g, mod[0], mod[0], *ws, *extra)


def _shift_rows(cur, prev8, j):
    rolled = pltpu.roll(cur, j, 0)
    row8 = lax.broadcasted_iota(jnp.int32, (SUBLANES, 1), 0)
    head = jnp.where(row8 < j, pltpu.roll(prev8, j, 0), rolled[0:SUBLANES])
    if cur.shape[0] == SUBLANES:
        return head
    return jnp.concatenate([head, rolled[SUBLANES:]], axis=0)


def _rw_math(p, prev, mu, w0, a0, kk_s, ka_s, rk, w2, a2, g2, log_space):
    c = w0.shape[-1]
    lo = w2.shape[0]
    mixd = p + (prev - p) * mu
    r = mixd[:, 0:c]
    k0 = mixd[:, c:2 * c]
    v = mixd[:, 2 * c:3 * c]
    wl = mixd[:, 3 * c:3 * c + lo]
    al = mixd[:, 3 * c + lo:3 * c + 2 * lo]
    gl = mixd[:, 3 * c + 2 * lo:3 * c + 3 * lo]
    z = -(w0 + jnp.dot(jnp.tanh(wl).astype(BF16), w2, preferred_element_type=F32))
    w = -jnp.exp(-_softplus(z) - 0.5)
    if not log_space:
        w = jnp.exp(w)
    a = _sigmoid(a0 + jnp.dot(al.astype(BF16), a2, preferred_element_type=F32))
    g = jnp.dot(_sigmoid(gl).astype(BF16), g2, preferred_element_type=F32)
    kk = k0 * kk_s
    kk = kk * lax.rsqrt(jnp.maximum(_seg_sum(kk * kk), 1e-24))
    k = k0 * (1.0 + (a - 1.0) * ka_s)
    bv = _seg_sum(r * k * rk) * v
    return r, w, k, v, -kk, kk * a, g, bv


def _rw_pre_step_kernel(p_ref, prev_ref, *rest):
    params, outs = rest[:9], rest[9:]
    res = _rw_math(p_ref[...], prev_ref[...], *[a[...] for a in params], False)
    for o, val in zip(outs, res):
        o[...] = val


def _rw_pre_step(p, prev, params, tm):
    grp, rows, cp = p.shape
    c = params[1].shape[-1]
    return pl.pallas_call(
        _rw_pre_step_kernel, grid=(grp, rows // tm),
        in_specs=[_row_spec(tm, cp), _row_spec(tm, cp)] + [_const_spec(a) for a in params],
        out_specs=[_row_spec(tm, c)] * 8, out_shape=[jax.ShapeDtypeStruct((grp, rows, c), F32)] * 8,
        compiler_params=_cparams(("parallel", "parallel")),
        name="rw_pre_step")(p, prev, *params)


def _mb_math(cur, s1, s2, s3, z, dtr, cw, cb, dtb, a_exp, a_pad, chunked):
    n_x = z.shape[-1]
    conv = s3 * cw[0:1] + s2 * cw[1:2] + s1 * cw[2:3] + cur * cw[3:4] + cb
    act = _silu(conv)
    mx = act[:, 0:n_x]
    half = (act.shape[-1] - n_x) // 2
    mb = act[:, n_x:n_x + half]
    mc = act[:, n_x + half:]
    dt = _softplus(dtr + dtb)
    nd = dt.shape[-1]
    i = lax.broadcasted_iota(jnp.int32, (3 * nd, n_x), 0) % nd
    j = lax.broadcasted_iota(jnp.int32, (3 * nd, n_x), 1)
    dte = _dot01(dt, j // HEAD == i)
    if chunked:
        return dt * a_pad, mx * dte, act[:, n_x:], mx, _silu(z)
    i = lax.broadcasted_iota(jnp.int32, (3 * half, 2 * half), 0) % half
    j = lax.broadcasted_iota(jnp.int32, (3 * half, 2 * half), 1)
    rep = i == (j // LANES) * HEAD + j % HEAD
    return jnp.exp(dte * a_exp), mx * dte, _dot01(mb, rep), _dot01(mc, rep), mx, _silu(z)


def _mb_pre_step_kernel(xbc_ref, s1_ref, s2_ref, s3_ref, z_ref, dt_ref, cw, cb, dtb, a_exp, a_pad, *outs):
    res = _mb_math(xbc_ref[...], s1_ref[...], s2_ref[...], s3_ref[...], z_ref[...], dt_ref[...],
                   cw[...], cb[...], dtb[...], a_exp[...], a_pad[...], False)
    for o, val in zip(outs, res):
        o[...] = val


def _mb_pre_step(xbc, z, dtr, conv_rows, params, tm):
    grp, rows, cx = xbc.shape
    n_x = z.shape[-1]
    n_k = cx - n_x
    widths = [n_x, n_x, n_k, n_k, n_x, n_x]
    return pl.pallas_call(
        _mb_pre_step_kernel, grid=(grp, rows // tm),
        in_specs=[_row_spec(tm, cx)] * 4 + [_row_spec(tm, n_x), _row_spec(tm, dtr.shape[-1])]
        + [_const_spec(a) for a in params],
        out_specs=[_row_spec(tm, wd) for wd in widths],
        out_shape=[jax.ShapeDtypeStruct((grp, rows, wd), F32) for wd in widths],
        compiler_params=_cparams(("parallel", "parallel")),
        name="mb_pre_step")(xbc, *conv_rows, z, dtr, *params)


def _hg_finish(o, g, hg_g):
    return o * lax.rsqrt(_seg_sum(o * o) * (1.0 / HEAD) + RMS_EPS) * hg_g * _silu(g)


def _mb_finish(o, mx, zs, d_exp, mb_g, n_grp):
    y = (o + d_exp * mx) * zs
    gw = y.shape[-1] // n_grp
    return jnp.concatenate(
        [y[:, g * gw:(g + 1) * gw]
         * lax.rsqrt(jnp.mean(y[:, g * gw:(g + 1) * gw] ** 2, axis=-1, keepdims=True) + RMS_EPS)
         for g in range(n_grp)], axis=-1) * mb_g


def _rw_finish(o, g, bv, ln_g, ln_b):
    mu = _seg_sum(o) * (1.0 / HEAD)
    dlt = o - mu
    var = _seg_sum(dlt * dlt) * (1.0 / HEAD)
    return (dlt * lax.rsqrt(var + RW_GN_EPS) * ln_g + ln_b + bv) * g


def _mix_finish_kernel(ohg_ref, ghg_ref, omb_ref, mx_ref, zs_ref, orw_ref, grw_ref, bv_ref,
                       hg_g, d_exp, mb_g, ln_g, ln_b, a_ref, y_ref, c_ref, *, n_grp):
    a_ref[...] = _hg_finish(ohg_ref[...], ghg_ref[...], hg_g[...]).astype(BF16)
    y_ref[...] = _mb_finish(omb_ref[...], mx_ref[...], zs_ref[...], d_exp[...], mb_g[...], n_grp).astype(BF16)
    c_ref[...] = _rw_finish(orw_ref[...], grw_ref[...], bv_ref[...], ln_g[...], ln_b[...]).astype(BF16)


def _mix_finish(acts, params, tm, n_grp):
    grp, rows, _ = acts[0].shape
    widths = [acts[0].shape[-1], acts[2].shape[-1], acts[5].shape[-1]]
    return pl.pallas_call(
        functools.partial(_mix_finish_kernel, n_grp=n_grp),
        grid=(grp, rows // tm),
        in_specs=[_row_spec(tm, a.shape[-1]) for a in acts] + [_const_spec(a) for a in params],
        out_specs=[_row_spec(tm, wd) for wd in widths],
        out_shape=[jax.ShapeDtypeStruct((grp, rows, wd), BF16) for wd in widths],
        compiler_params=_cparams(("parallel", "parallel")),
        name="mix_finish",
    )(*acts, *params)


def _block_diag_mask():
    i = lax.broadcasted_iota(jnp.int32, (LANES, LANES), 0) // HEAD
    j = lax.broadcasted_iota(jnp.int32, (LANES, LANES), 1) // HEAD
    return i == j


def _pair_state_load(s0_ref, s_scr, nhp):
    for p in range(nhp):
        s_scr[p] = jnp.zeros((LANES, LANES), F32)
        s_scr[p, 0:HEAD, 0:HEAD] = s0_ref[2 * p]
        s_scr[p, HEAD:LANES, HEAD:LANES] = s0_ref[2 * p + 1]


def _pair_state_store(s_scr, so_ref, nhp):
    for p in range(nhp):
        so_ref[2 * p] = s_scr[p, 0:HEAD, 0:HEAD]
        so_ref[2 * p + 1] = s_scr[p, HEAD:LANES, HEAD:LANES]


def _chunk_call(kern, proj, carry0, params, s0, out_width, tb, nb, scratch, name):
    grp, rows, c = proj.shape
    nb = min(nb, grp)
    sspec = pl.BlockSpec((nb,) + s0.shape[1:], lambda g, i: (g, 0, 0, 0))
    in_specs = [pl.BlockSpec((nb, tb, c), lambda g, i: (g, i, 0))]
    args = [proj]
    out_specs = [pl.BlockSpec((nb, tb, out_width), lambda g, i: (g, i, 0)), sspec]
    out_shape = [jax.ShapeDtypeStruct((grp, rows, out_width), BF16), jax.ShapeDtypeStruct(s0.shape, F32)]
    if carry0 is not None:
        cspec = pl.BlockSpec((nb,) + carry0.shape[1:], lambda g, i: (g, 0, 0))
        in_specs.append(cspec)
        args.append(carry0)
        out_specs.append(cspec)
        out_shape.append(jax.ShapeDtypeStruct(carry0.shape, F32))
    return pl.pallas_call(
        functools.partial(kern, nb=nb), grid=(grp // nb, rows // tb),
        in_specs=in_specs + [_const_spec(a) for a in params] + [sspec],
        out_specs=out_specs,
        out_shape=out_shape,
        scratch_shapes=[s(nb) for s in scratch],
        compiler_params=_cparams(("parallel", "arbitrary")),
        name=name,
    )(*args, *params, s0)


def _ssd_chunk_kernel(p_ref, c8_ref, cw, cb, dtb, a_exp, a_pad, d_exp, mb_g, s0_ref, y_ref, so_ref, co_ref,
                      h_scr, carry_scr, y_scr, *, nb, nh, ng, n_x):
    c = pl.program_id(1)

    @pl.when(c == 0)
    def _():
        h_scr[...] = s0_ref[...]
        carry_scr[...] = c8_ref[...]

    n = p_ref.shape[1]
    n_bc = 2 * ng * HEAD
    rg = nh // ng
    row = lax.broadcasted_iota(jnp.int32, (n, n), 0)
    col = lax.broadcasted_iota(jnp.int32, (n, n), 1)
    tril = row >= col
    fin, work = [], []
    for s in range(nb):
        cur = p_ref[s, :, 0:n_x + n_bc]
        prev8 = carry_scr[s]
        dta, xdt, bc, mx, zs = _mb_math(
            cur, _shift_rows(cur, prev8, 1), _shift_rows(cur, prev8, 2), _shift_rows(cur, prev8, 3),
            p_ref[s, :, n_x + n_bc:2 * n_x + n_bc], p_ref[s, :, 2 * n_x + n_bc:],
            cw[...], cb[...], dtb[...], a_exp[...], a_pad[...], True)
        carry_scr[s] = cur[n - SUBLANES:]
        fin.append((mx, zs))
        acum = _cumsum_rows(dta)
        acum_t = acum.T
        bgs = [bc[:, g * HEAD:(g + 1) * HEAD].astype(BF16) for g in range(ng)]
        cgs = [bc[:, (ng + g) * HEAD:(ng + g + 1) * HEAD].astype(BF16) for g in range(ng)]
        cbs = [lax.dot_general(cg, bg, _NT, preferred_element_type=F32) for cg, bg in zip(cgs, bgs)]
        for h in range(nh):
            g = h // rg
            a_col = acum[:, h:h + 1]
            a_tot = acum[n - 1:n, h:h + 1]
            lmat = jnp.exp(jnp.where(tril, a_col - acum_t[h:h + 1, :], -jnp.inf))
            xh = xdt[:, h * HEAD:(h + 1) * HEAD]
            work.append(dict(s=s, h=h, a_col=a_col, a_tot=a_tot, sh=h_scr[s, h], xh=xh.astype(BF16), bg=bgs[g],
                             cg=cgs[g], gm=(cbs[g] * lmat).astype(BF16),
                             xd=(xh * jnp.exp(a_tot - a_col)).astype(BF16)))
    for q in work:
        q['intra'] = jnp.dot(q['gm'], q['xh'], preferred_element_type=F32)
        q['inter'] = lax.dot_general(q['cg'], q['sh'].astype(BF16), _NT, preferred_element_type=F32)
        q['upd'] = lax.dot_general(q['xd'], q['bg'], _TN, preferred_element_type=F32)
    for q in work:
        s, h = q['s'], q['h']
        y_scr[s, :, h * HEAD:(h + 1) * HEAD] = q['intra'] + jnp.exp(q['a_col']) * q['inter']
        h_scr[s, h] = jnp.exp(q['a_tot']) * q['sh'] + q['upd']
    for s in range(nb):
        y_ref[s] = _mb_finish(y_scr[s], fin[s][0], fin[s][1], d_exp[...], mb_g[...], ng).astype(BF16)

    @pl.when(c == pl.num_programs(1) - 1)
    def _():
        so_ref[...] = h_scr[...]
        co_ref[...] = carry_scr[...]


def _ssd_chunked(proj, conv8, params, s0, n_x, ng):
    nh = s0.shape[1]
    tb = min(SSD_CHUNK, proj.shape[1])
    scratch = [lambda nb: pltpu.VMEM((nb,) + s0.shape[1:], F32),
               lambda nb: pltpu.VMEM((nb, SUBLANES, conv8.shape[-1]), F32),
               lambda nb: pltpu.VMEM((nb, tb, n_x), F32)]
    return _chunk_call(functools.partial(_ssd_chunk_kernel, nh=nh, ng=ng, n_x=n_x), proj, conv8, params, s0, n_x,
                       tb, 1, scratch, "ssd_chunk")


def _hg_chunk_kernel(p_ref, lb_ref, llb_ref, l1m_ref, hg_g, s0_ref, a_ref, so_ref, s_scr, o_scr,
                     *, nb, nhp, sub):
    c = pl.program_id(1)

    @pl.when(c == 0)
    def _():
        for s in range(nb):
            _pair_state_load(s0_ref.at[s], s_scr.at[s], nhp)

    tb = p_ref.shape[1]
    cw = lb_ref.shape[-1]
    pos = lax.broadcasted_iota(jnp.int32, (tb, 1), 0) % sub
    bd = _block_diag_mask()
    seg = jnp.where(bd, 1.0, 0.0).astype(BF16)
    causal = (lax.broadcasted_iota(jnp.int32, (sub, sub, 1), 1) <= lax.broadcasted_iota(jnp.int32, (sub, sub, 1), 0))
    bs, ks = [], []
    for s in range(nb):
        b, k = _hg_gates(p_ref[s, :, cw:2 * cw], lb_ref[...], llb_ref[...], l1m_ref[...], True)
        step = 1
        while step < sub:
            b = b + jnp.where(pos >= step, pltpu.roll(b, step, 0), 0.0)
            step *= 2
        bs.append(b)
        ks.append(k)
    probs = [(s, p) for s in range(nb) for p in range(nhp)]
    states = {sp: s_scr[sp[0], sp[1]] for sp in probs}
    for ci in range(tb // sub):
        rs = slice(ci * sub, (ci + 1) * sub)
        work = []
        for s, p in probs:
            ls = slice(p * LANES, (p + 1) * LANES)
            bc = bs[s][rs, ls]
            kc = ks[s][rs, ls]
            qc = p_ref[s, rs, p * LANES:(p + 1) * LANES]
            vc = p_ref[s, rs, 2 * cw + p * LANES:2 * cw + (p + 1) * LANES]
            btot = bc[sub - 1:sub]
            o = lax.dot_general((qc * jnp.exp(bc)).astype(BF16), states[s, p].astype(BF16), _NT,
                                preferred_element_type=F32)
            e = jnp.exp(jnp.where(causal, bc[:, None, :] - bc[None, :, :], -jnp.inf))
            x = (e * kc[None, :, :] * qc[:, None, :]).astype(BF16).reshape(sub * sub, LANES)
            att = jnp.dot(x, seg, preferred_element_type=F32).reshape(sub, sub, LANES)
            kd = (kc * jnp.exp(btot - bc)).astype(BF16)
            upd = lax.dot_general(vc.astype(BF16), kd, _TN, preferred_element_type=F32)
            work.append((s, p, ls, o, att, vc, jnp.exp(btot), upd))
        for s, p, ls, o, att, vc, dec, upd in work:
            o_scr[s, rs, ls] = o + jnp.sum(att * vc[None, :, :], axis=1)
            states[s, p] = states[s, p] * dec + jnp.where(bd, upd, 0.0)
    for s, p in probs:
        s_scr[s, p] = states[s, p]
    for s in range(nb):
        a_ref[s] = _hg_finish(o_scr[s], p_ref[s, :, 3 * cw:4 * cw], hg_g[...]).astype(BF16)

    @pl.when(c == pl.num_programs(1) - 1)
    def _():
        for s in range(nb):
            _pair_state_store(s_scr.at[s], so_ref.at[s], nhp)


def _hg_chunked(proj, params, s0):
    nhp = s0.shape[1] // 2
    tb = min(HG_BLOCK, proj.shape[1])
    cw = proj.shape[-1] // 4
    scratch = [lambda nb: pltpu.VMEM((nb, nhp, LANES, LANES), F32), lambda nb: pltpu.VMEM((nb, tb, cw), F32)]
    return _chunk_call(functools.partial(_hg_chunk_kernel, nhp=nhp, sub=HG_SUB), proj, None, params, s0, cw, tb,
                       SEQ_PER_STEP, scratch, "hg_chunk")


def _split2(x):
    hi = x.astype(BF16)
    return hi, (x - hi.astype(F32)).astype(BF16)


def _dot3(a, b, dims=_NN):
    ah, am = _split2(a)
    bh, bm = _split2(b)
    (ca,), (cb,) = dims[0]
    return lax.dot_general(jnp.concatenate([ah, ah, am], axis=ca), jnp.concatenate([bh, bm, bh], axis=cb), dims,
                           preferred_element_type=F32)


def _cumsum_rows(x):
    n = x.shape[0]
    tril3 = (lax.broadcasted_iota(jnp.int32, (n, 3 * n), 0) >= lax.broadcasted_iota(jnp.int32, (n, 3 * n), 1) % n)
    return jnp.dot(jnp.where(tril3, 1.0, 0.0).astype(BF16), jnp.concatenate(_split3(x), axis=0),
                   preferred_element_type=F32)


def _tri_inverse(n_mats, blk, top):
    n = n_mats[0].shape[0]
    ri = lax.broadcasted_iota(jnp.int32, (n, n), 0)
    ci = lax.broadcasted_iota(jnp.int32, (n, n), 1)
    eye = jnp.where(ri == ci, 1.0, 0.0)
    pws = [jnp.where(ri // blk == ci // blk, m, 0.0) for m in n_mats]
    ts = [eye + pw for pw in pws]
    s = 2
    while s < blk:
        pws = [_dot3(pw, pw) for pw in pws]
        ts = [t + _dot3(t, pw) for t, pw in zip(ts, pws)]
        s *= 2
    size = blk
    while size < top:
        in_2blk = ri // (2 * size) == ci // (2 * size)
        in_blk = ri // size == ci // size
        offs = [jnp.where(in_2blk, m, 0.0) - jnp.where(in_blk, m, 0.0) for m in n_mats]
        mids = [_dot3(t, off) for t, off in zip(ts, offs)]
        ts = [t + _dot3(mid, t) for t, mid in zip(ts, mids)]
        size *= 2
    return ts


def _rw_chunk_kernel(p_ref, c8_ref, mu, w0, a0, kk_s, ka_s, rk, w2, a2, g2, ln_g, ln_b, s0_ref, c_ref, so_ref,
                     co_ref, s_scr, carry_scr, *, nb, nhp):
    c = pl.program_id(1)

    @pl.when(c == 0)
    def _():
        carry_scr[...] = c8_ref[...]
        for s in range(nb):
            _pair_state_load(s0_ref.at[s], s_scr.at[s], nhp)

    n = p_ref.shape[1]
    n2 = 2 * n
    ri = lax.broadcasted_iota(jnp.int32, (n2, n2), 0)
    ci = lax.broadcasted_iota(jnp.int32, (n2, n2), 1)
    tril = ri >= ci
    strict = ri > ci
    m0 = lax.broadcasted_iota(jnp.int32, (1, LANES), 1) < HEAD
    bd = _block_diag_mask()
    params = [a[...] for a in (mu, w0, a0, kk_s, ka_s, rk, w2, a2, g2)]

    def stack(x):
        return jnp.concatenate([jnp.where(m0, x, 0.0), jnp.where(m0, 0.0, x)], axis=0)

    probs = []
    curs = [p_ref[s] for s in range(nb)]
    prevs = [_shift_rows(cur, carry_scr[s], 1) for s, cur in enumerate(curs)]
    for s, cur in enumerate(curs):
        carry_scr[s] = cur[n - SUBLANES:]
    acts = _rw_math(jnp.concatenate(curs, axis=0), jnp.concatenate(prevs, axis=0), *params, True)
    g_all, bv_all = acts[6], acts[7]
    for s in range(nb):
        r, lw, k, v, a, b = [t[s * n:(s + 1) * n] for t in acts[:6]]
        cum = _cumsum_rows(lw)
        e_n = jnp.exp(-cum)
        ctot = cum[n - 1:n]
        e_rel = jnp.exp(ctot - cum)
        at = a * jnp.exp(cum - lw)
        rt = r * jnp.exp(cum)
        bt = b * e_n
        kt = k * e_n
        bh = b * e_rel
        kh = k * e_rel
        for p in range(nhp):
            ls = slice(p * LANES, (p + 1) * LANES)
            probs.append(dict(
                s=s, p=p, ls=ls, vp=v[:, ls], dec=jnp.exp(ctot[:, ls]),
                ar=jnp.concatenate([stack(at[:, ls]), stack(rt[:, ls])], axis=0),
                bk=jnp.concatenate([stack(bt[:, ls]), stack(kt[:, ls])], axis=0),
                bkh=jnp.concatenate([bh[:, ls], kh[:, ls]], axis=0)))
    for q in probs:
        q['st'] = s_scr[q['s'], q['p']]
        q['a_s'] = _dot3(q['ar'], q['st'], _NT)
    for q in probs:
        q['g'] = _dot3(q['ar'], q['bk'], _NT)
    t_invs = _tri_inverse([jnp.where(strict, q['g'][0:n2, 0:n2], 0.0) for q in probs], RW_INV_BLOCK, n)
    for q in probs:
        q['v2'] = stack(q['vp'])
        q['rhs'] = q['a_s'][0:n2] + _dot3(jnp.where(strict, q['g'][0:n2, n2:], 0.0), q['v2'])
    for q, t_inv in zip(probs, t_invs):
        q['u2'] = _dot3(t_inv, q['rhs'])
    outs = {}
    for q in probs:
        g = q['g']
        pm = jnp.concatenate([jnp.where(tril, g[n2:, 0:n2], 0.0), jnp.where(tril, g[n2:, n2:], 0.0)], axis=1)
        o2 = q['a_s'][n2:] + _dot3(pm, jnp.concatenate([q['u2'], q['v2']], axis=0))
        outs[q['s'], q['p']] = o2[0:n] + o2[n:]
    for q in probs:
        u2 = q['u2']
        upd = _dot3(jnp.concatenate([u2[0:n] + u2[n:], q['vp']], axis=0), q['bkh'], _TN)
        s_scr[q['s'], q['p']] = q['st'] * q['dec'] + jnp.where(bd, upd, 0.0)
    o_all = jnp.concatenate([jnp.concatenate([outs[s, p] for p in range(nhp)], axis=-1) for s in range(nb)], axis=0)
    c_all = _rw_finish(o_all, g_all, bv_all, ln_g[...], ln_b[...]).astype(BF16)
    for s in range(nb):
        c_ref[s] = c_all[s * n:(s + 1) * n]

    @pl.when(c == pl.num_programs(1) - 1)
    def _():
        co_ref[...] = carry_scr[...]
        for s in range(nb):
            _pair_state_store(s_scr.at[s], so_ref.at[s], nhp)


def _rw_chunked(proj, shift8, params, s0):
    nhp = s0.shape[1] // 2
    tb = min(RW_CHUNK, proj.shape[1])
    cw = params[1].shape[-1]
    scratch = [lambda nb: pltpu.VMEM((nb, nhp, LANES, LANES), F32),
               lambda nb: pltpu.VMEM((nb, SUBLANES, proj.shape[-1]), F32)]
    return _chunk_call(functools.partial(_rw_chunk_kernel, nhp=nhp), proj, shift8, params, s0, cw, tb,
                       RW_SEQ_PER_STEP, scratch, "rw_chunk")


def _pair_mask():
    lane = lax.broadcasted_iota(jnp.int32, (1, LANES), 1)
    return lane, lane < HEAD


def _seg_cols(p, m0):
    c0 = jnp.sum(jnp.where(m0, p, 0.0), axis=-1, keepdims=True)
    c1 = jnp.sum(jnp.where(m0, 0.0, p), axis=-1, keepdims=True)
    return c0, c1


def _value_cols(vt, mt, m0):
    col = jnp.sum(jnp.where(mt, vt, 0.0), axis=-1, keepdims=True)
    return jnp.where(m0, col[0:HEAD], col[HEAD:2 * HEAD])


def _store_cols(ot_ref, row0, o0, o1, mt):
    shape = (HEAD, LANES)
    mask = jnp.broadcast_to(mt, shape)
    pltpu.store(ot_ref.at[pl.ds(row0, HEAD), :], jnp.broadcast_to(o0, shape), mask=mask)
    pltpu.store(ot_ref.at[pl.ds(row0 + HEAD, HEAD), :], jnp.broadcast_to(o1, shape), mask=mask)


def _split_rec_refs(refs, delta, aliased):
    if aliased:
        refs = refs[:-3] + refs[-2:]
    if delta:
        return refs
    return refs[:3] + (None, None) + refs[3:]


def _rec_step_kernel(*refs, nb, nhp, delta, divs, aliased):
    w_ref, k_ref, r_ref, nkk_ref, kka_ref, vt_ref, s0_ref, ot_ref, so_ref = _split_rec_refs(refs, delta, aliased)
    dw, dk, dr = divs
    i = pl.program_id(0)
    lane, m0 = _pair_mask()

    def row(ref, b, hp, d):
        return ref[b:b + 1, (hp // d) * LANES:(hp // d + 1) * LANES]

    probs = [(b, hp) for b in range(nb) for hp in range(nhp)]
    mts = [lane == i * nb + b for b in range(nb)]
    ss = [jnp.concatenate([s0_ref[b, 2 * hp], s0_ref[b, 2 * hp + 1]], axis=-1) for b, hp in probs]
    vbs = [_value_cols(vt_ref[hp * LANES:(hp + 1) * LANES, :], mts[b], m0) for b, hp in probs]
    if delta:
        cols = [_seg_cols(s * row(nkk_ref, b, hp, 1), m0) for s, (b, hp) in zip(ss, probs)]
        ss = [s * row(w_ref, b, hp, dw) + jnp.where(m0, c0, c1) * row(kka_ref, b, hp, 1) + vb * row(k_ref, b, hp, dk)
              for s, (c0, c1), vb, (b, hp) in zip(ss, cols, vbs, probs)]
    else:
        ss = [s * row(w_ref, b, hp, dw) + vb * row(k_ref, b, hp, dk) for s, vb, (b, hp) in zip(ss, vbs, probs)]
    outs = [_seg_cols(s * row(r_ref, b, hp, dr), m0) for s, (b, hp) in zip(ss, probs)]
    for s, (o0, o1), (b, hp) in zip(ss, outs, probs):
        so_ref[b, 2 * hp] = s[:, 0:HEAD]
        so_ref[b, 2 * hp + 1] = s[:, HEAD:2 * HEAD]
        _store_cols(ot_ref, hp * LANES, o0, o1, mts[b])


def _recurrence(rows_in, v, s_all, layer, s_out, divs):
    delta = len(rows_in) == 5
    aliased = s_out is not None
    grp, rows, c = v.shape
    ns, nh = s_all.shape[1:3]
    nhp = nh // 2
    assert grp == 1 and rows == ns and rows <= LANES
    nb = SUBLANES
    pad = LANES - rows
    vt = jnp.pad(v[0].T, ((0, 0), (0, pad)))
    sspec = pl.BlockSpec((None, nb) + s_all.shape[2:], lambda i: (layer, i, 0, 0, 0))
    n_in = len(rows_in) + 2
    kern = functools.partial(_rec_step_kernel, nb=nb, nhp=nhp, delta=delta, divs=divs, aliased=aliased)
    ot, s_new = pl.pallas_call(
        kern, grid=(rows // nb,),
        in_specs=[pl.BlockSpec((nb, a.shape[-1]), lambda i: (i, 0)) for a in rows_in]
        + [pl.BlockSpec((c, LANES), lambda i: (0, 0)), sspec]
        + ([pl.BlockSpec(memory_space=pl.ANY)] if aliased else []),
        out_specs=[pl.BlockSpec((c, LANES), lambda i: (0, 0)), sspec],
        out_shape=[jax.ShapeDtypeStruct((c, LANES), F32), jax.ShapeDtypeStruct(s_all.shape, F32)],
        input_output_aliases={n_in: 1} if aliased else {},
        compiler_params=_cparams(("arbitrary",)),
        name="rec_step",
    )(*[a[0] for a in rows_in], vt, s_all, *([s_out] if aliased else []))
    return ot[:, :rows].T[None], s_new


def _out_ffn_kernel(x_ref, a_ref, y_ref, c_ref, gate1_ref, wo_ref, g_ref, sc_ref, sh_ref, gate_ref, wg_ref, wu_ref,
                    wd_ref, gf_ref, o_ref, u_scr, *, tf, final_norm):
    mix_in = jnp.concatenate([a_ref[...], y_ref[...], c_ref[...]], axis=-1)
    x = x_ref[...] + gate1_ref[...] * jnp.dot(mix_in, wo_ref[...], preferred_element_type=F32)
    h = _rmsnorm_mod(x, g_ref[...], sc_ref[...], sh_ref[...]).astype(BF16)
    for f in range(wg_ref.shape[-1] // tf):
        fs = slice(f * tf, (f + 1) * tf)
        u = _silu(jnp.dot(h, wg_ref[:, fs], preferred_element_type=F32)) * jnp.dot(h, wu_ref[:, fs],
                                                                                   preferred_element_type=F32)
        u_scr[:, fs] = u.astype(BF16)
    y = x + gate_ref[...] * jnp.dot(u_scr[...], wd_ref[...], preferred_element_type=F32)
    if final_norm:
        y = y * lax.rsqrt(jnp.mean(y * y, axis=-1, keepdims=True) + RMS_EPS) * gf_ref[...]
    o_ref[...] = y


def _out_ffn(x, acts, mod, wo, g, wg, wu, wd, g_final, tm, tf, final_norm):
    grp, rows, d = x.shape
    nf = wg.shape[-1]
    return pl.pallas_call(
        functools.partial(_out_ffn_kernel, tf=tf, final_norm=final_norm),
        grid=(grp, rows // tm),
        in_specs=[_row_spec(tm, d)] + [_row_spec(tm, a.shape[-1]) for a in acts]
        + [_mod_spec(mod, tm, d, 2), _resident_spec(wo), _const_spec(g), _mod_spec(mod, tm, d, 4),
           _mod_spec(mod, tm, d, 3), _mod_spec(mod, tm, d, 5), _resident_spec(wg), _resident_spec(wu),
           _resident_spec(wd), _const_spec(g_final)],
        out_specs=_row_spec(tm, d),
        out_shape=jax.ShapeDtypeStruct(x.shape, F32),
        scratch_shapes=[pltpu.VMEM((tm, nf), BF16)],
        compiler_params=_cparams(("parallel", "parallel")),
        name="out_ffn",
    )(x, *acts, mod[0], wo, g, mod[0], mod[0], mod[0], wg, wu, wd, g_final)


def _pad_cols(a, width):
    return jnp.pad(a, [(0, 0)] * (a.ndim - 1) + [(0, width - a.shape[-1])])


def _pad_rows(a, height):
    return jnp.pad(a, [(0, 0)] * (a.ndim - 2) + [(0, height - a.shape[-2]), (0, 0)])


def _prep_layer(l, P, dims):
    hgw, n_x, n_bc, nh_mb, rww, lw, la, lg = dims
    w_in = P['w_in'][l]
    c0 = 4 * hgw
    c1 = c0 + n_x
    c2 = c1 + n_x + n_bc
    c3 = c2 + nh_mb
    w_hg = w_in[:, :c0].astype(BF16)
    w_mb = jnp.concatenate([w_in[:, c1:c2], w_in[:, c0:c1], _pad_cols(w_in[:, c2:c3], LANES)], axis=1).astype(BF16)
    w_rw_raw = w_in[:, c3:]

    def rw_cols(a):
        o = 3 * rww
        return jnp.concatenate([a[..., :o], _pad_cols(a[..., o:o + lw], LANES),
                                _pad_cols(a[..., o + lw:o + lw + la], LANES),
                                _pad_cols(a[..., o + lw + la:], LANES)], axis=-1)

    row = lambda a: a.reshape(1, -1).astype(F32)
    a_neg = -jnp.exp(P['mb_A_log'][l].astype(F32))
    return dict(
        w_hg=w_hg, w_mb=w_mb, w_rw=rw_cols(w_rw_raw).astype(BF16), rw_cols=rw_cols,
        g1=row(P['g_norm1'][l]), g2=row(P['g_norm2'][l]),
        rw_params=(rw_cols(row(P['rw_mu'][l])), row(P['rw_w0'][l]), row(P['rw_a0'][l]), row(P['rw_k_k'][l]),
                   row(P['rw_k_a'][l]), row(P['rw_r_k'][l]),
                   _pad_rows(P['rw_w2'][l], LANES).astype(BF16), _pad_rows(P['rw_a2'][l], LANES).astype(BF16),
                   _pad_rows(P['rw_g2'][l], LANES).astype(BF16)),
        mb_params=(P['mb_conv_w'][l].astype(F32), row(P['mb_conv_b'][l]), _pad_cols(row(P['mb_dt_bias'][l]), LANES),
                   row(jnp.repeat(a_neg, HEAD)), _pad_cols(row(a_neg), LANES)),
        out_params=(row(P['hg_norm_g'][l]), row(jnp.repeat(P['mb_D'][l], HEAD)), row(P['mb_norm_g'][l]),
                    row(P['rw_ln_g'][l]), row(P['rw_ln_b'][l])),
        wo=P['w_out'][l].astype(BF16), wg=P['w_gate'][l].astype(BF16), wu=P['w_up'][l].astype(BF16),
        wd=P['w_down'][l].astype(BF16),
    )


def _trunk(x, mods, st_hg, st_ssm, st_conv, st_wkv, st_shift, lb_all, layers, g_final, dims, seq, tm, tf):
    hgw, n_x, n_bc, nh_mb, rww, lw, la, lg = dims
    grp, rows, d = x.shape
    depth = len(layers)
    n_grp = n_bc // (2 * HEAD)
    outs = ([], [], [], [], [])
    st_hg = jnp.swapaxes(st_hg, -1, -2)
    s_hg = s_ssm = s_wkv = None
    for l in range(depth):
        L = layers[l]
        mod = (mods, l)
        lb = lb_all[l].reshape(1, -1)
        hg_g, d_exp, mb_g, ln_g, ln_b = L['out_params']
        cw_rw = L['w_rw'].shape[-1]
        if seq:
            p_hg, p_mb, rw_p = _in_proj(
                x, L['g1'], mod, [L['w_hg'], L['w_mb'], L['w_rw']], [(_split_epilogue, 1)] * 3, [],
                [4 * hgw, 2 * n_x + n_bc + LANES, cw_rw], tm, "in_proj")
            shift8 = jnp.pad(L['rw_cols'](st_shift[l])[:, None, :], ((0, 0), (SUBLANES - 1, 0), (0, 0)))
            conv8 = jnp.pad(st_conv[l], ((0, 0), (SUBLANES - st_conv[l].shape[1], 0), (0, 0)))
            a_hg, s_hg = _hg_chunked(p_hg, (lb, jnp.log(lb), jnp.log1p(-lb), hg_g), st_hg[l])
            y_mb, s_ssm, conv_tail = _ssd_chunked(p_mb, conv8, L['mb_params'] + (d_exp, mb_g), st_ssm[l], n_x, n_grp)
            c_rw, s_wkv, shift_tail = _rw_chunked(rw_p, shift8, L['rw_params'] + (ln_g, ln_b), st_wkv[l])
            conv_new = conv_tail[:, SUBLANES - st_conv[l].shape[1]:, :]
            shift_new = shift_tail[:, SUBLANES - 1, :]
        else:
            q, w_hg, k_hg, i_hg, g_hg, xbc, z, dtr, rw_p = _in_proj(
                x, L['g1'], mod, [L['w_hg'], L['w_mb'], L['w_rw']],
                [(_hg_epilogue, 5), (_split_epilogue, 3), (_split_epilogue, 1)], [lb],
                [hgw] * 5 + [n_x + n_bc, n_x, LANES, cw_rw], tm, "in_proj_step")
            conv_new = jnp.concatenate([st_conv[l][:, 1:, :], xbc[0][:, None, :]], axis=1)
            shift_new = rw_p[0]
            r_rw, w_rw, k_rw, v_rw, nkk, kka, g_rw, bv = _rw_pre_step(
                rw_p, L['rw_cols'](st_shift[l])[None], L['rw_params'], tm)
            w_mb, v_mb, k_mb, r_mb, mx, zs = _mb_pre_step(
                xbc, z, dtr, tuple(st_conv[l][None, :, 2 - j, :] for j in range(3)), L['mb_params'], tm)
            o_hg, s_hg = _recurrence((w_hg, k_hg, q), i_hg, st_hg, l, s_hg, (1, 1, 1))
            o_mb, s_ssm = _recurrence((w_mb, k_mb, r_mb), v_mb, st_ssm, l, s_ssm, (1, 2, 2))
            o_rw, s_wkv = _recurrence((w_rw, k_rw, r_rw, nkk, kka), v_rw, st_wkv, l, s_wkv, (1, 1, 1))
            a_hg, y_mb, c_rw = _mix_finish((o_hg, g_hg, o_mb, mx, zs, o_rw, g_rw, bv), L['out_params'], tm, n_grp)
        x = _out_ffn(x, (a_hg, y_mb, c_rw), mod, L['wo'], L['g2'], L['wg'], L['wu'], L['wd'], g_final, tm, tf,
                     final_norm=(l == depth - 1))
        o3 = 3 * rww
        shift_new = jnp.concatenate([shift_new[:, :o3], shift_new[:, o3:o3 + lw],
                                     shift_new[:, o3 + LANES:o3 + LANES + la],
                                     shift_new[:, o3 + 2 * LANES:o3 + 2 * LANES + lg]], axis=-1)
        for lst, s in zip(outs, (s_hg, s_ssm, conv_new, s_wkv, shift_new)):
            lst.append(s)
    if seq:
        hg_new, ssm_new, conv_new, wkv_new, shift_new = [jnp.stack(lst) for lst in outs]
    else:
        hg_new, ssm_new, wkv_new = s_hg, s_ssm, s_wkv
        conv_new, shift_new = jnp.stack(outs[2]), jnp.stack(outs[4])
    return x, [jnp.swapaxes(hg_new, -1, -2), ssm_new, conv_new, wkv_new, shift_new]


def kernel(x_prompt, x_sample, state_hgrn, state_ssm, state_conv, state_wkv, state_shift, c_prompt, c_sample,
           w_ada, b_ada, g_norm1, w_in, hg_lb_logits, hg_norm_g, mb_conv_w, mb_conv_b, mb_dt_bias, mb_A_log,
           mb_D, mb_norm_g, rw_mu, rw_w0, rw_w2, rw_a0, rw_a2, rw_g2, rw_k_k, rw_k_a, rw_r_k, rw_ln_g, rw_ln_b,
           w_out, g_norm2, w_gate, w_up, w_down, g_final):
    P = dict(g_norm1=g_norm1, w_in=w_in, hg_norm_g=hg_norm_g, mb_conv_w=mb_conv_w, mb_conv_b=mb_conv_b,
             mb_dt_bias=mb_dt_bias, mb_A_log=mb_A_log, mb_D=mb_D, mb_norm_g=mb_norm_g, rw_mu=rw_mu, rw_w0=rw_w0,
             rw_w2=rw_w2, rw_a0=rw_a0, rw_a2=rw_a2, rw_g2=rw_g2, rw_k_k=rw_k_k, rw_k_a=rw_k_a,
             rw_r_k=rw_r_k.reshape(rw_r_k.shape[0], -1), rw_ln_g=rw_ln_g, rw_ln_b=rw_ln_b, w_out=w_out,
             g_norm2=g_norm2, w_gate=w_gate, w_up=w_up, w_down=w_down)
    depth = w_in.shape[0]
    bsz, t_len, d = x_prompt.shape
    dec = x_sample.shape[0]
    hgw = hg_lb_logits.shape[-1]
    n_x = mb_norm_g.shape[-1]
    n_bc = mb_conv_w.shape[-1] - n_x
    nh_mb = mb_dt_bias.shape[-1]
    rww = rw_w0.shape[-1]
    lw, la, lg = rw_w2.shape[1], rw_a2.shape[1], rw_g2.shape[1]
    dims = (hgw, n_x, n_bc, nh_mb, rww, lw, la, lg)
    layers = [_prep_layer(l, P, dims) for l in range(depth)]
    gf = g_final.reshape(1, d)

    cs = jnp.cumsum(jax.nn.softmax(hg_lb_logits.astype(F32), axis=0), axis=0)
    lb_all = cs - cs[:1]

    n_c = bsz + dec
    pad_c = -n_c % (2 * SUBLANES)
    mods_p, mods_s = _ada(jnp.pad(jnp.concatenate([c_prompt, c_sample], axis=0), ((0, pad_c), (0, 0))), bsz, dec,
                          w_ada, b_ada)
    mods_p = mods_p[:, :, None, :]
    mods_s = mods_s[:, None, :, :]

    dt = x_prompt.dtype
    zeros = lambda a: jnp.zeros((depth, bsz) + a.shape[2:], dt)
    y_p, st_p = _trunk(x_prompt, mods_p, zeros(state_hgrn), zeros(state_ssm), zeros(state_conv), zeros(state_wkv),
                       zeros(state_shift), lb_all, layers, gf, dims, True, min(512, t_len), 256)
    y_s, st_s = _trunk(x_sample.reshape(1, dec, d), mods_s, state_hgrn, state_ssm, state_conv, state_wkv,
                       state_shift, lb_all, layers, gf, dims, False, dec, 256)
    hg_p, ssm_p, conv_p, wkv_p, shift_p = st_p
    hg_s, ssm_s, conv_s, wkv_s, shift_s = st_s
    return (y_p, y_s.reshape(x_sample.shape), hg_p, hg_s, ssm_p, ssm_s, conv_p, conv_s, wkv_p, wkv_s,
            shift_p, shift_s)
```

```python
import functools

import jax
import jax.numpy as jnp
from jax import lax
from jax.experimental import pallas as pl
from jax.experimental.pallas import tpu as pltpu

F32 = jnp.float32
BF16 = jnp.bfloat16

LANES = 128
SUBLANES = 8
HEAD = 64
RMS_EPS = 1e-6
RW_GN_EPS = 64e-5
VMEM_LIMIT = 48 * 1024 * 1024
SSD_CHUNK = 128
HG_BLOCK = 128
HG_SUB = 16
RW_CHUNK = 64
RW_INV_BLOCK = 16
SEQ_PER_STEP = 2
RW_SEQ_PER_STEP = 4

_NN = (((1,), (0,)), ((), ()))
_NT = (((1,), (1,)), ((), ()))
_TN = (((0,), (0,)), ((), ()))


def _cparams(sem):
    return pltpu.CompilerParams(dimension_semantics=sem, vmem_limit_bytes=VMEM_LIMIT)


def _sigmoid(x):
    return 1.0 / (1.0 + jnp.exp(-x))


def _silu(x):
    return x * _sigmoid(x)


def _softplus(x):
    return jnp.maximum(x, 0.0) + jnp.log1p(jnp.exp(-jnp.abs(x)))


def _split3(x):
    hi = x.astype(BF16)
    r1 = x - hi.astype(F32)
    mid = r1.astype(BF16)
    return hi, mid, (r1 - mid.astype(F32)).astype(BF16)


def _seg_sum(x, seg=HEAD):
    n = x.shape[-1]
    i = lax.broadcasted_iota(jnp.int32, (3 * n, n), 0) % n // seg
    j = lax.broadcasted_iota(jnp.int32, (3 * n, n), 1) // seg
    return _dot01(x, i == j)


def _dot01(x, mask3):
    return jnp.dot(jnp.concatenate(_split3(x), axis=-1), jnp.where(mask3, 1.0, 0.0).astype(BF16),
                   preferred_element_type=F32)


def _rmsnorm_mod(x, g, sc, sh):
    h = x * lax.rsqrt(jnp.mean(x * x, axis=-1, keepdims=True) + RMS_EPS) * g
    return h * (1.0 + sc) + sh


def _ada_kernel(c_ref, w_ref, b_ref, op_ref, os_ref):
    a = _silu(c_ref[...]).astype(BF16)
    y = jnp.dot(a, w_ref[...].astype(BF16), preferred_element_type=F32) + b_ref[...]
    n_p = op_ref.shape[0]
    op_ref[...] = y[0:n_p]
    os_ref[...] = y[n_p:n_p + os_ref.shape[0]]


def _ada(c_all, n_p, n_s, w_ada, b_ada):
    depth, d, n = w_ada.shape
    rows = c_all.shape[0]
    tn = n // 4
    return pl.pallas_call(
        _ada_kernel,
        grid=(depth, n // tn),
        in_specs=[pl.BlockSpec((rows, d), lambda l, j: (0, 0)),
                  pl.BlockSpec((None, d, tn), lambda l, j: (l, 0, j)),
                  pl.BlockSpec((None, 1, tn), lambda l, j: (l, 0, j))],
        out_specs=[pl.BlockSpec((None, n_p, tn), lambda l, j: (l, 0, j)),
                   pl.BlockSpec((None, n_s, tn), lambda l, j: (l, 0, j))],
        out_shape=[jax.ShapeDtypeStruct((depth, n_p, n), F32), jax.ShapeDtypeStruct((depth, n_s, n), F32)],
        compiler_params=_cparams(("parallel", "parallel")),
        name="ada_mod",
    )(c_all, w_ada, b_ada.reshape(depth, 1, n))


def _row_spec(tm, width, col=0):
    return pl.BlockSpec((None, tm, width), lambda g, i: (g, i, col))


def _mod_spec(mod, tm, d, idx):
    arr, l = mod
    if arr.shape[2] == 1:
        return pl.BlockSpec((None, None, 1, d), lambda g, i: (l, g, 0, idx))
    return pl.BlockSpec((None, None, tm, d), lambda g, i: (l, g, i, idx))


def _const_spec(a):
    nd = a.ndim
    return pl.BlockSpec(a.shape, lambda g, i: (0,) * nd)


def _resident_spec(a):
    nd = a.ndim
    return pl.BlockSpec(a.shape, lambda g, i: (0,) * nd, pipeline_mode=pl.Buffered(1))


def _in_proj_kernel(x_ref, g_ref, sc_ref, sh_ref, *rest, n_w, epilogues, n_extra):
    ws, rest = rest[:n_w], rest[n_w:]
    extra, outs = rest[:n_extra], rest[n_extra:]
    h = _rmsnorm_mod(x_ref[...], g_ref[...], sc_ref[...], sh_ref[...]).astype(BF16)
    for w_ref, (epilogue, n_out) in zip(ws, epilogues):
        epilogue(jnp.dot(h, w_ref[...], preferred_element_type=F32), extra, outs[:n_out])
        outs = outs[n_out:]


def _split_epilogue(y, extra, outs):
    col = 0
    for o in outs:
        wd = o.shape[-1]
        o[...] = y[:, col:col + wd]
        col += wd


def _hg_gates(zf, lb, log_lb, log_1mlb, log_space):
    k = (1.0 - lb) * _sigmoid(-zf)
    if not log_space:
        return lb + (1.0 - lb) * _sigmoid(zf), k
    b = log_1mlb - _softplus(-zf)
    return jnp.maximum(log_lb, b) + jnp.log1p(jnp.exp(-jnp.abs(log_lb - b))), k


def _hg_epilogue(y, extra, outs):
    lb = extra[0][...]
    q_ref, w_ref, k_ref, i_ref, g_ref = outs
    c = lb.shape[-1]
    q_ref[...] = y[:, 0:c]
    w_ref[...], k_ref[...] = _hg_gates(y[:, c:2 * c], lb, None, None, False)
    i_ref[...] = y[:, 2 * c:3 * c]
    g_ref[...] = y[:, 3 * c:4 * c]


def _in_proj(x, g, mod, ws, epilogues, extra, out_widths, tm, name):
    grp, rows, d = x.shape
    kern = functools.partial(_in_proj_kernel, n_w=len(ws), epilogues=epilogues, n_extra=len(extra))
    return pl.pallas_call(
        kern,
        grid=(grp, rows // tm),
        in_specs=[_row_spec(tm, d), _const_spec(g), _mod_spec(mod, tm, d, 1), _mod_spec(mod, tm, d, 0)]
        + [_resident_spec(w) for w in ws] + [_const_spec(e) for e in extra],
        out_specs=[_row_spec(tm, wd) for wd in out_widths],
        out_shape=[jax.ShapeDtypeStruct((grp, rows, wd), F32) for wd in out_widths],
        compiler_params=_cparams(("parallel", "parallel")),
        name=name,
    )(x, g, mod[0], mod[0], *ws, *extra)


def _shift_rows(cur, prev8, j):
    rolled = pltpu.roll(cur, j, 0)
    row8 = lax.broadcasted_iota(jnp.int32, (SUBLANES, 1), 0)
    head = jnp.where(row8 < j, pltpu.roll(prev8, j, 0), rolled[0:SUBLANES])
    if cur.shape[0] == SUBLANES:
        return head
    return jnp.concatenate([head, rolled[SUBLANES:]], axis=0)


def _rw_math(p, prev, mu, w0, a0, kk_s, ka_s, rk, w2, a2, g2, log_space):
    c = w0.shape[-1]
    lo = w2.shape[0]
    mixd = p + (prev - p) * mu
    r = mixd[:, 0:c]
    k0 = mixd[:, c:2 * c]
    v = mixd[:, 2 * c:3 * c]
    wl = mixd[:, 3 * c:3 * c + lo]
    al = mixd[:, 3 * c + lo:3 * c + 2 * lo]
    gl = mixd[:, 3 * c + 2 * lo:3 * c + 3 * lo]
    z = -(w0 + jnp.dot(jnp.tanh(wl).astype(BF16), w2, preferred_element_type=F32))
    w = -jnp.exp(-_softplus(z) - 0.5)
    if not log_space:
        w = jnp.exp(w)
    a = _sigmoid(a0 + jnp.dot(al.astype(BF16), a2, preferred_element_type=F32))
    g = jnp.dot(_sigmoid(gl).astype(BF16), g2, preferred_element_type=F32)
    kk = k0 * kk_s
    kk = kk * lax.rsqrt(jnp.maximum(_seg_sum(kk * kk), 1e-24))
    k = k0 * (1.0 + (a - 1.0) * ka_s)
    bv = _seg_sum(r * k * rk) * v
    return r, w, k, v, -kk, kk * a, g, bv


def _rw_pre_step_kernel(p_ref, prev_ref, *rest):
    params, outs = rest[:9], rest[9:]
    res = _rw_math(p_ref[...], prev_ref[...], *[a[...] for a in params], False)
    for o, val in zip(outs, res):
        o[...] = val


def _rw_pre_step(p, prev, params, tm):
    grp, rows, cp = p.shape
    c = params[1].shape[-1]
    return pl.pallas_call(
        _rw_pre_step_kernel, grid=(grp, rows // tm),
        in_specs=[_row_spec(tm, cp), _row_spec(tm, cp)] + [_const_spec(a) for a in params],
        out_specs=[_row_spec(tm, c)] * 8, out_shape=[jax.ShapeDtypeStruct((grp, rows, c), F32)] * 8,
        compiler_params=_cparams(("parallel", "parallel")),
        name="rw_pre_step")(p, prev, *params)


def _mb_math(cur, s1, s2, s3, z, dtr, cw, cb, dtb, a_exp, a_pad, chunked):
    n_x = z.shape[-1]
    conv = s3 * cw[0:1] + s2 * cw[1:2] + s1 * cw[2:3] + cur * cw[3:4] + cb
    act = _silu(conv)
    mx = act[:, 0:n_x]
    half = (act.shape[-1] - n_x) // 2
    mb = act[:, n_x:n_x + half]
    mc = act[:, n_x + half:]
    dt = _softplus(dtr + dtb)
    nd = dt.shape[-1]
    i = lax.broadcasted_iota(jnp.int32, (3 * nd, n_x), 0) % nd
    j = lax.broadcasted_iota(jnp.int32, (3 * nd, n_x), 1)
    dte = _dot01(dt, j // HEAD == i)
    if chunked:
        return dt * a_pad, mx * dte, act[:, n_x:], mx, _silu(z)
    i = lax.broadcasted_iota(jnp.int32, (3 * half, 2 * half), 0) % half
    j = lax.broadcasted_iota(jnp.int32, (3 * half, 2 * half), 1)
    rep = i == (j // LANES) * HEAD + j % HEAD
    return jnp.exp(dte * a_exp), mx * dte, _dot01(mb, rep), _dot01(mc, rep), mx, _silu(z)


def _mb_pre_step_kernel(xbc_ref, s1_ref, s2_ref, s3_ref, z_ref, dt_ref, cw, cb, dtb, a_exp, a_pad, *outs):
    res = _mb_math(xbc_ref[...], s1_ref[...], s2_ref[...], s3_ref[...], z_ref[...], dt_ref[...],
                   cw[...], cb[...], dtb[...], a_exp[...], a_pad[...], False)
    for o, val in zip(outs, res):
        o[...] = val


def _mb_pre_step(xbc, z, dtr, conv_rows, params, tm):
    grp, rows, cx = xbc.shape
    n_x = z.shape[-1]
    n_k = cx - n_x
    widths = [n_x, n_x, n_k, n_k, n_x, n_x]
    return pl.pallas_call(
        _mb_pre_step_kernel, grid=(grp, rows // tm),
        in_specs=[_row_spec(tm, cx)] * 4 + [_row_spec(tm, n_x), _row_spec(tm, dtr.shape[-1])]
        + [_const_spec(a) for a in params],
        out_specs=[_row_spec(tm, wd) for wd in widths],
        out_shape=[jax.ShapeDtypeStruct((grp, rows, wd), F32) for wd in widths],
        compiler_params=_cparams(("parallel", "parallel")),
        name="mb_pre_step")(xbc, *conv_rows, z, dtr, *params)


def _hg_finish(o, g, hg_g):
    return o * lax.rsqrt(_seg_sum(o * o) * (1.0 / HEAD) + RMS_EPS) * hg_g * _silu(g)


def _mb_finish(o, mx, zs, d_exp, mb_g, n_grp):
    y = (o + d_exp * mx) * zs
    gw = y.shape[-1] // n_grp
    return jnp.concatenate(
        [y[:, g * gw:(g + 1) * gw]
         * lax.rsqrt(jnp.mean(y[:, g * gw:(g + 1) * gw] ** 2, axis=-1, keepdims=True) + RMS_EPS)
         for g in range(n_grp)], axis=-1) * mb_g


def _rw_finish(o, g, bv, ln_g, ln_b):
    mu = _seg_sum(o) * (1.0 / HEAD)
    dlt = o - mu
    var = _seg_sum(dlt * dlt) * (1.0 / HEAD)
    return (dlt * lax.rsqrt(var + RW_GN_EPS) * ln_g + ln_b + bv) * g


def _mix_finish_kernel(ohg_ref, ghg_ref, omb_ref, mx_ref, zs_ref, orw_ref, grw_ref, bv_ref,
                       hg_g, d_exp, mb_g, ln_g, ln_b, a_ref, y_ref, c_ref, *, n_grp):
    a_ref[...] = _hg_finish(ohg_ref[...], ghg_ref[...], hg_g[...]).astype(BF16)
    y_ref[...] = _mb_finish(omb_ref[...], mx_ref[...], zs_ref[...], d_exp[...], mb_g[...], n_grp).astype(BF16)
    c_ref[...] = _rw_finish(orw_ref[...], grw_ref[...], bv_ref[...], ln_g[...], ln_b[...]).astype(BF16)


def _mix_finish(acts, params, tm, n_grp):
    grp, rows, _ = acts[0].shape
    widths = [acts[0].shape[-1], acts[2].shape[-1], acts[5].shape[-1]]
    return pl.pallas_call(
        functools.partial(_mix_finish_kernel, n_grp=n_grp),
        grid=(grp, rows // tm),
        in_specs=[_row_spec(tm, a.shape[-1]) for a in acts] + [_const_spec(a) for a in params],
        out_specs=[_row_spec(tm, wd) for wd in widths],
        out_shape=[jax.ShapeDtypeStruct((grp, rows, wd), BF16) for wd in widths],
        compiler_params=_cparams(("parallel", "parallel")),
        name="mix_finish",
    )(*acts, *params)


def _block_diag_mask():
    i = lax.broadcasted_iota(jnp.int32, (LANES, LANES), 0) // HEAD
    j = lax.broadcasted_iota(jnp.int32, (LANES, LANES), 1) // HEAD
    return i == j


def _pair_state_load(s0_ref, s_scr, nhp):
    for p in range(nhp):
        s_scr[p] = jnp.zeros((LANES, LANES), F32)
        s_scr[p, 0:HEAD, 0:HEAD] = s0_ref[2 * p]
        s_scr[p, HEAD:LANES, HEAD:LANES] = s0_ref[2 * p + 1]


def _pair_state_store(s_scr, so_ref, nhp):
    for p in range(nhp):
        so_ref[2 * p] = s_scr[p, 0:HEAD, 0:HEAD]
        so_ref[2 * p + 1] = s_scr[p, HEAD:LANES, HEAD:LANES]


def _chunk_call(kern, proj, carry0, params, s0, out_width, tb, nb, scratch, name):
    grp, rows, c = proj.shape
    nb = min(nb, grp)
    sspec = pl.BlockSpec((nb,) + s0.shape[1:], lambda g, i: (g, 0, 0, 0))
    in_specs = [pl.BlockSpec((nb, tb, c), lambda g, i: (g, i, 0))]
    args = [proj]
    out_specs = [pl.BlockSpec((nb, tb, out_width), lambda g, i: (g, i, 0)), sspec]
    out_shape = [jax.ShapeDtypeStruct((grp, rows, out_width), BF16), jax.ShapeDtypeStruct(s0.shape, F32)]
    if carry0 is not None:
        cspec = pl.BlockSpec((nb,) + carry0.shape[1:], lambda g, i: (g, 0, 0))
        in_specs.append(cspec)
        args.append(carry0)
        out_specs.append(cspec)
        out_shape.append(jax.ShapeDtypeStruct(carry0.shape, F32))
    return pl.pallas_call(
        functools.partial(kern, nb=nb), grid=(grp // nb, rows // tb),
        in_specs=in_specs + [_const_spec(a) for a in params] + [sspec],
        out_specs=out_specs,
        out_shape=out_shape,
        scratch_shapes=[s(nb) for s in scratch],
        compiler_params=_cparams(("parallel", "arbitrary")),
        name=name,
    )(*args, *params, s0)


def _ssd_chunk_kernel(p_ref, c8_ref, cw, cb, dtb, a_exp, a_pad, d_exp, mb_g, s0_ref, y_ref, so_ref, co_ref,
                      h_scr, carry_scr, y_scr, *, nb, nh, ng, n_x):
    c = pl.program_id(1)

    @pl.when(c == 0)
    def _():
        h_scr[...] = s0_ref[...]
        carry_scr[...] = c8_ref[...]

    n = p_ref.shape[1]
    n_bc = 2 * ng * HEAD
    rg = nh // ng
    row = lax.broadcasted_iota(jnp.int32, (n, n), 0)
    col = lax.broadcasted_iota(jnp.int32, (n, n), 1)
    tril = row >= col
    fin, work = [], []
    for s in range(nb):
        cur = p_ref[s, :, 0:n_x + n_bc]
        prev8 = carry_scr[s]
        dta, xdt, bc, mx, zs = _mb_math(
            cur, _shift_rows(cur, prev8, 1), _shift_rows(cur, prev8, 2), _shift_rows(cur, prev8, 3),
            p_ref[s, :, n_x + n_bc:2 * n_x + n_bc], p_ref[s, :, 2 * n_x + n_bc:],
            cw[...], cb[...], dtb[...], a_exp[...], a_pad[...], True)
        carry_scr[s] = cur[n - SUBLANES:]
        fin.append((mx, zs))
        acum = _cumsum_rows(dta)
        acum_t = acum.T
        bgs = [bc[:, g * HEAD:(g + 1) * HEAD].astype(BF16) for g in range(ng)]
        cgs = [bc[:, (ng + g) * HEAD:(ng + g + 1) * HEAD].astype(BF16) for g in range(ng)]
        cbs = [lax.dot_general(cg, bg, _NT, preferred_element_type=F32) for cg, bg in zip(cgs, bgs)]
        for h in range(nh):
            g = h // rg
            a_col = acum[:, h:h + 1]
            a_tot = acum[n - 1:n, h:h + 1]
            lmat = jnp.exp(jnp.where(tril, a_col - acum_t[h:h + 1, :], -jnp.inf))
            xh = xdt[:, h * HEAD:(h + 1) * HEAD]
            work.append(dict(s=s, h=h, a_col=a_col, a_tot=a_tot, sh=h_scr[s, h], xh=xh.astype(BF16), bg=bgs[g],
                             cg=cgs[g], gm=(cbs[g] * lmat).astype(BF16),
                             xd=(xh * jnp.exp(a_tot - a_col)).astype(BF16)))
    for q in work:
        q['intra'] = jnp.dot(q['gm'], q['xh'], preferred_element_type=F32)
        q['inter'] = lax.dot_general(q['cg'], q['sh'].astype(BF16), _NT, preferred_element_type=F32)
        q['upd'] = lax.dot_general(q['xd'], q['bg'], _TN, preferred_element_type=F32)
    for q in work:
        s, h = q['s'], q['h']
        y_scr[s, :, h * HEAD:(h + 1) * HEAD] = q['intra'] + jnp.exp(q['a_col']) * q['inter']
        h_scr[s, h] = jnp.exp(q['a_tot']) * q['sh'] + q['upd']
    for s in range(nb):
        y_ref[s] = _mb_finish(y_scr[s], fin[s][0], fin[s][1], d_exp[...], mb_g[...], ng).astype(BF16)

    @pl.when(c == pl.num_programs(1) - 1)
    def _():
        so_ref[...] = h_scr[...]
        co_ref[...] = carry_scr[...]


def _ssd_chunked(proj, conv8, params, s0, n_x, ng):
    nh = s0.shape[1]
    tb = min(SSD_CHUNK, proj.shape[1])
    scratch = [lambda nb: pltpu.VMEM((nb,) + s0.shape[1:], F32),
               lambda nb: pltpu.VMEM((nb, SUBLANES, conv8.shape[-1]), F32),
               lambda nb: pltpu.VMEM((nb, tb, n_x), F32)]
    return _chunk_call(functools.partial(_ssd_chunk_kernel, nh=nh, ng=ng, n_x=n_x), proj, conv8, params, s0, n_x,
                       tb, 1, scratch, "ssd_chunk")


def _hg_chunk_kernel(p_ref, lb_ref, llb_ref, l1m_ref, hg_g, s0_ref, a_ref, so_ref, s_scr, o_scr,
                     *, nb, nhp, sub):
    c = pl.program_id(1)

    @pl.when(c == 0)
    def _():
        for s in range(nb):
            _pair_state_load(s0_ref.at[s], s_scr.at[s], nhp)

    tb = p_ref.shape[1]
    cw = lb_ref.shape[-1]
    pos = lax.broadcasted_iota(jnp.int32, (tb, 1), 0) % sub
    bd = _block_diag_mask()
    seg = jnp.where(bd, 1.0, 0.0).astype(BF16)
    causal = (lax.broadcasted_iota(jnp.int32, (sub, sub, 1), 1) <= lax.broadcasted_iota(jnp.int32, (sub, sub, 1), 0))
    bs, ks = [], []
    for s in range(nb):
        b, k = _hg_gates(p_ref[s, :, cw:2 * cw], lb_ref[...], llb_ref[...], l1m_ref[...], True)
        step = 1
        while step < sub:
            b = b + jnp.where(pos >= step, pltpu.roll(b, step, 0), 0.0)
            step *= 2
        bs.append(b)
        ks.append(k)
    probs = [(s, p) for s in range(nb) for p in range(nhp)]
    states = {sp: s_scr[sp[0], sp[1]] for sp in probs}
    for ci in range(tb // sub):
        rs = slice(ci * sub, (ci + 1) * sub)
        work = []
        for s, p in probs:
            ls = slice(p * LANES, (p + 1) * LANES)
            bc = bs[s][rs, ls]
            kc = ks[s][rs, ls]
            qc = p_ref[s, rs, p * LANES:(p + 1) * LANES]
            vc = p_ref[s, rs, 2 * cw + p * LANES:2 * cw + (p + 1) * LANES]
            btot = bc[sub - 1:sub]
            o = lax.dot_general((qc * jnp.exp(bc)).astype(BF16), states[s, p].astype(BF16), _NT,
                                preferred_element_type=F32)
            e = jnp.exp(jnp.where(causal, bc[:, None, :] - bc[None, :, :], -jnp.inf))
            x = (e * kc[None, :, :] * qc[:, None, :]).astype(BF16).reshape(sub * sub, LANES)
            att = jnp.dot(x, seg, preferred_element_type=F32).reshape(sub, sub, LANES)
            kd = (kc * jnp.exp(btot - bc)).astype(BF16)
            upd = lax.dot_general(vc.astype(BF16), kd, _TN, preferred_element_type=F32)
            work.append((s, p, ls, o, att, vc, jnp.exp(btot), upd))
        for s, p, ls, o, att, vc, dec, upd in work:
            o_scr[s, rs, ls] = o + jnp.sum(att * vc[None, :, :], axis=1)
            states[s, p] = states[s, p] * dec + jnp.where(bd, upd, 0.0)
    for s, p in probs:
        s_scr[s, p] = states[s, p]
    for s in range(nb):
        a_ref[s] = _hg_finish(o_scr[s], p_ref[s, :, 3 * cw:4 * cw], hg_g[...]).astype(BF16)

    @pl.when(c == pl.num_programs(1) - 1)
    def _():
        for s in range(nb):
            _pair_state_store(s_scr.at[s], so_ref.at[s], nhp)


def _hg_chunked(proj, params, s0):
    nhp = s0.shape[1] // 2
    tb = min(HG_BLOCK, proj.shape[1])
    cw = proj.shape[-1] // 4
    scratch = [lambda nb: pltpu.VMEM((nb, nhp, LANES, LANES), F32), lambda nb: pltpu.VMEM((nb, tb, cw), F32)]
    return _chunk_call(functools.partial(_hg_chunk_kernel, nhp=nhp, sub=HG_SUB), proj, None, params, s0, cw, tb,
                       SEQ_PER_STEP, scratch, "hg_chunk")


def _split2(x):
    hi = x.astype(BF16)
    return hi, (x - hi.astype(F32)).astype(BF16)


def _dot3(a, b, dims=_NN):
    ah, am = _split2(a)
    bh, bm = _split2(b)
    (ca,), (cb,) = dims[0]
    return lax.dot_general(jnp.concatenate([ah, ah, am], axis=ca), jnp.concatenate([bh, bm, bh], axis=cb), dims,
                           preferred_element_type=F32)


def _cumsum_rows(x):
    n = x.shape[0]
    tril3 = (lax.broadcasted_iota(jnp.int32, (n, 3 * n), 0) >= lax.broadcasted_iota(jnp.int32, (n, 3 * n), 1) % n)
    return jnp.dot(jnp.where(tril3, 1.0, 0.0).astype(BF16), jnp.concatenate(_split3(x), axis=0),
                   preferred_element_type=F32)


def _tri_inverse(n_mats, blk, top):
    n = n_mats[0].shape[0]
    ri = lax.broadcasted_iota(jnp.int32, (n, n), 0)
    ci = lax.broadcasted_iota(jnp.int32, (n, n), 1)
    eye = jnp.where(ri == ci, 1.0, 0.0)
    pws = [jnp.where(ri // blk == ci // blk, m, 0.0) for m in n_mats]
    ts = [eye + pw for pw in pws]
    s = 2
    while s < blk:
        pws = [_dot3(pw, pw) for pw in pws]
        ts = [t + _dot3(t, pw) for t, pw in zip(ts, pws)]
        s *= 2
    size = blk
    while size < top:
        in_2blk = ri // (2 * size) == ci // (2 * size)
        in_blk = ri // size == ci // size
        offs = [jnp.where(in_2blk, m, 0.0) - jnp.where(in_blk, m, 0.0) for m in n_mats]
        mids = [_dot3(t, off) for t, off in zip(ts, offs)]
        ts = [t + _dot3(mid, t) for t, mid in zip(ts, mids)]
        size *= 2
    return ts


def _rw_chunk_kernel(p_ref, c8_ref, mu, w0, a0, kk_s, ka_s, rk, w2, a2, g2, ln_g, ln_b, s0_ref, c_ref, so_ref,
                     co_ref, s_scr, carry_scr, *, nb, nhp):
    c = pl.program_id(1)

    @pl.when(c == 0)
    def _():
        carry_scr[...] = c8_ref[...]
        for s in range(nb):
            _pair_state_load(s0_ref.at[s], s_scr.at[s], nhp)

    n = p_ref.shape[1]
    n2 = 2 * n
    ri = lax.broadcasted_iota(jnp.int32, (n2, n2), 0)
    ci = lax.broadcasted_iota(jnp.int32, (n2, n2), 1)
    tril = ri >= ci
    strict = ri > ci
    m0 = lax.broadcasted_iota(jnp.int32, (1, LANES), 1) < HEAD
    bd = _block_diag_mask()
    params = [a[...] for a in (mu, w0, a0, kk_s, ka_s, rk, w2, a2, g2)]

    def stack(x):
        return jnp.concatenate([jnp.where(m0, x, 0.0), jnp.where(m0, 0.0, x)], axis=0)

    probs = []
    curs = [p_ref[s] for s in range(nb)]
    prevs = [_shift_rows(cur, carry_scr[s], 1) for s, cur in enumerate(curs)]
    for s, cur in enumerate(curs):
        carry_scr[s] = cur[n - SUBLANES:]
    acts = _rw_math(jnp.concatenate(curs, axis=0), jnp.concatenate(prevs, axis=0), *params, True)
    g_all, bv_all = acts[6], acts[7]
    for s in range(nb):
        r, lw, k, v, a, b = [t[s * n:(s + 1) * n] for t in acts[:6]]
        cum = _cumsum_rows(lw)
        e_n = jnp.exp(-cum)
        ctot = cum[n - 1:n]
        e_rel = jnp.exp(ctot - cum)
        at = a * jnp.exp(cum - lw)
        rt = r * jnp.exp(cum)
        bt = b * e_n
        kt = k * e_n
        bh = b * e_rel
        kh = k * e_rel
        for p in range(nhp):
            ls = slice(p * LANES, (p + 1) * LANES)
            probs.append(dict(
                s=s, p=p, ls=ls, vp=v[:, ls], dec=jnp.exp(ctot[:, ls]),
                ar=jnp.concatenate([stack(at[:, ls]), stack(rt[:, ls])], axis=0),
                bk=jnp.concatenate([stack(bt[:, ls]), stack(kt[:, ls])], axis=0),
                bkh=jnp.concatenate([bh[:, ls], kh[:, ls]], axis=0)))
    for q in probs:
        q['st'] = s_scr[q['s'], q['p']]
        q['a_s'] = _dot3(q['ar'], q['st'], _NT)
    for q in probs:
        q['g'] = _dot3(q['ar'], q['bk'], _NT)
    t_invs = _tri_inverse([jnp.where(strict, q['g'][0:n2, 0:n2], 0.0) for q in probs], RW_INV_BLOCK, n)
    for q in probs:
        q['v2'] = stack(q['vp'])
        q['rhs'] = q['a_s'][0:n2] + _dot3(jnp.where(strict, q['g'][0:n2, n2:], 0.0), q['v2'])
    for q, t_inv in zip(probs, t_invs):
        q['u2'] = _dot3(t_inv, q['rhs'])
    outs = {}
    for q in probs:
        g = q['g']
        pm = jnp.concatenate([jnp.where(tril, g[n2:, 0:n2], 0.0), jnp.where(tril, g[n2:, n2:], 0.0)], axis=1)
        o2 = q['a_s'][n2:] + _dot3(pm, jnp.concatenate([q['u2'], q['v2']], axis=0))
        outs[q['s'], q['p']] = o2[0:n] + o2[n:]
    for q in probs:
        u2 = q['u2']
        upd = _dot3(jnp.concatenate([u2[0:n] + u2[n:], q['vp']], axis=0), q['bkh'], _TN)
        s_scr[q['s'], q['p']] = q['st'] * q['dec'] + jnp.where(bd, upd, 0.0)
    o_all = jnp.concatenate([jnp.concatenate([outs[s, p] for p in range(nhp)], axis=-1) for s in range(nb)], axis=0)
    c_all = _rw_finish(o_all, g_all, bv_all, ln_g[...], ln_b[...]).astype(BF16)
    for s in range(nb):
        c_ref[s] = c_all[s * n:(s + 1) * n]

    @pl.when(c == pl.num_programs(1) - 1)
    def _():
        co_ref[...] = carry_scr[...]
        for s in range(nb):
            _pair_state_store(s_scr.at[s], so_ref.at[s], nhp)


def _rw_chunked(proj, shift8, params, s0):
    nhp = s0.shape[1] // 2
    tb = min(RW_CHUNK, proj.shape[1])
    cw = params[1].shape[-1]
    scratch = [lambda nb: pltpu.VMEM((nb, nhp, LANES, LANES), F32),
               lambda nb: pltpu.VMEM((nb, SUBLANES, proj.shape[-1]), F32)]
    return _chunk_call(functools.partial(_rw_chunk_kernel, nhp=nhp), proj, shift8, params, s0, cw, tb,
                       RW_SEQ_PER_STEP, scratch, "rw_chunk")


REC_VBLOCK = 8


def _rec_step_kernel(*refs, delta, aliased):
    if aliased:
        refs = refs[:-4] + refs[-3:]
    if delta:
        w_ref, k_ref, r_ref, v_ref, nkk_ref, kka_ref, s0_ref, o_ref, so_ref, ot_scr = refs
    else:
        w_ref, k_ref, r_ref, v_ref, s0_ref, o_ref, so_ref, ot_scr = refs
    wt, kt, rt, vt = w_ref[...].T, k_ref[...].T, r_ref[...].T, v_ref[...].T
    if delta:
        at, bt = nkk_ref[...].T, kka_ref[...].T
    for h in range(2):
        hs = slice(h * HEAD, (h + 1) * HEAD)
        w, k, r = wt[hs][None], kt[hs][None], rt[hs][None]
        for vb in range(0, HEAD, REC_VBLOCK):
            vs = slice(vb, vb + REC_VBLOCK)
            s = s0_ref[h, vs]
            val = vt[h * HEAD + vb:h * HEAD + vb + REC_VBLOCK][:, None, :]
            if delta:
                sa = jnp.sum(s * at[hs][None], axis=1)[:, None, :]
                s = s * w + sa * bt[hs][None] + val * k
            else:
                s = s * w + val * k
            so_ref[h, vs] = s
            ot_scr[h * HEAD + vb:h * HEAD + vb + REC_VBLOCK, :] = jnp.sum(s * r, axis=1)
    o_ref[...] = ot_scr[...].T


def _recurrence(rows_in, v, s_all, layer, s_out, divs):
    delta = len(rows_in) == 5
    aliased = s_out is not None
    grp, rows, c = v.shape
    nh, ns = s_all.shape[1], s_all.shape[-1]
    assert grp == 1 and rows == ns
    dw, dk, dr = divs

    def tile(div):
        return pl.BlockSpec((rows, LANES), lambda p: (0, p // div))

    sspec = pl.BlockSpec((None, 2) + s_all.shape[2:], lambda p: (layer, p, 0, 0, 0))
    operands = [rows_in[0][0], rows_in[1][0], rows_in[2][0], v[0]] + [a[0] for a in rows_in[3:]]
    in_specs = [tile(dw), tile(dk), tile(dr), tile(1)] + [tile(1)] * (len(rows_in) - 3)
    o, s_new = pl.pallas_call(
        functools.partial(_rec_step_kernel, delta=delta, aliased=aliased), grid=(nh // 2,),
        in_specs=in_specs + [sspec] + ([pl.BlockSpec(memory_space=pl.ANY)] if aliased else []),
        out_specs=[tile(1), sspec],
        out_shape=[jax.ShapeDtypeStruct((rows, c), F32), jax.ShapeDtypeStruct(s_all.shape, F32)],
        scratch_shapes=[pltpu.VMEM((LANES, rows), F32)],
        input_output_aliases={len(operands) + 1: 1} if aliased else {},
        compiler_params=_cparams(("arbitrary",)),
        name="rec_step",
    )(*operands, s_all, *([s_out] if aliased else []))
    return o[None], s_new


def _out_ffn_kernel(x_ref, a_ref, y_ref, c_ref, gate1_ref, wo_ref, g_ref, sc_ref, sh_ref, gate_ref, wg_ref, wu_ref,
                    wd_ref, gf_ref, o_ref, u_scr, *, tf, final_norm):
    mix_in = jnp.concatenate([a_ref[...], y_ref[...], c_ref[...]], axis=-1)
    x = x_ref[...] + gate1_ref[...] * jnp.dot(mix_in, wo_ref[...], preferred_element_type=F32)
    h = _rmsnorm_mod(x, g_ref[...], sc_ref[...], sh_ref[...]).astype(BF16)
    for f in range(wg_ref.shape[-1] // tf):
        fs = slice(f * tf, (f + 1) * tf)
        u = _silu(jnp.dot(h, wg_ref[:, fs], preferred_element_type=F32)) * jnp.dot(h, wu_ref[:, fs],
                                                                                   preferred_element_type=F32)
        u_scr[:, fs] = u.astype(BF16)
    y = x + gate_ref[...] * jnp.dot(u_scr[...], wd_ref[...], preferred_element_type=F32)
    if final_norm:
        y = y * lax.rsqrt(jnp.mean(y * y, axis=-1, keepdims=True) + RMS_EPS) * gf_ref[...]
    o_ref[...] = y


def _out_ffn(x, acts, mod, wo, g, wg, wu, wd, g_final, tm, tf, final_norm):
    grp, rows, d = x.shape
    nf = wg.shape[-1]
    return pl.pallas_call(
        functools.partial(_out_ffn_kernel, tf=tf, final_norm=final_norm),
        grid=(grp, rows // tm),
        in_specs=[_row_spec(tm, d)] + [_row_spec(tm, a.shape[-1]) for a in acts]
        + [_mod_spec(mod, tm, d, 2), _resident_spec(wo), _const_spec(g), _mod_spec(mod, tm, d, 4),
           _mod_spec(mod, tm, d, 3), _mod_spec(mod, tm, d, 5), _resident_spec(wg), _resident_spec(wu),
           _resident_spec(wd), _const_spec(g_final)],
        out_specs=_row_spec(tm, d),
        out_shape=jax.ShapeDtypeStruct(x.shape, F32),
        scratch_shapes=[pltpu.VMEM((tm, nf), BF16)],
        compiler_params=_cparams(("parallel", "parallel")),
        name="out_ffn",
    )(x, *acts, mod[0], wo, g, mod[0], mod[0], mod[0], wg, wu, wd, g_final)


def _pad_cols(a, width):
    return jnp.pad(a, [(0, 0)] * (a.ndim - 1) + [(0, width - a.shape[-1])])


def _pad_rows(a, height):
    return jnp.pad(a, [(0, 0)] * (a.ndim - 2) + [(0, height - a.shape[-2]), (0, 0)])


def _prep_layer(l, P, dims):
    hgw, n_x, n_bc, nh_mb, rww, lw, la, lg = dims
    w_in = P['w_in'][l]
    c0 = 4 * hgw
    c1 = c0 + n_x
    c2 = c1 + n_x + n_bc
    c3 = c2 + nh_mb
    w_hg = w_in[:, :c0].astype(BF16)
    w_mb = jnp.concatenate([w_in[:, c1:c2], w_in[:, c0:c1], _pad_cols(w_in[:, c2:c3], LANES)], axis=1).astype(BF16)
    w_rw_raw = w_in[:, c3:]

    def rw_cols(a):
        o = 3 * rww
        return jnp.concatenate([a[..., :o], _pad_cols(a[..., o:o + lw], LANES),
                                _pad_cols(a[..., o + lw:o + lw + la], LANES),
                                _pad_cols(a[..., o + lw + la:], LANES)], axis=-1)

    row = lambda a: a.reshape(1, -1).astype(F32)
    a_neg = -jnp.exp(P['mb_A_log'][l].astype(F32))
    return dict(
        w_hg=w_hg, w_mb=w_mb, w_rw=rw_cols(w_rw_raw).astype(BF16), rw_cols=rw_cols,
        g1=row(P['g_norm1'][l]), g2=row(P['g_norm2'][l]),
        rw_params=(rw_cols(row(P['rw_mu'][l])), row(P['rw_w0'][l]), row(P['rw_a0'][l]), row(P['rw_k_k'][l]),
                   row(P['rw_k_a'][l]), row(P['rw_r_k'][l]),
                   _pad_rows(P['rw_w2'][l], LANES).astype(BF16), _pad_rows(P['rw_a2'][l], LANES).astype(BF16),
                   _pad_rows(P['rw_g2'][l], LANES).astype(BF16)),
        mb_params=(P['mb_conv_w'][l].astype(F32), row(P['mb_conv_b'][l]), _pad_cols(row(P['mb_dt_bias'][l]), LANES),
                   row(jnp.repeat(a_neg, HEAD)), _pad_cols(row(a_neg), LANES)),
        out_params=(row(P['hg_norm_g'][l]), row(jnp.repeat(P['mb_D'][l], HEAD)), row(P['mb_norm_g'][l]),
                    row(P['rw_ln_g'][l]), row(P['rw_ln_b'][l])),
        wo=P['w_out'][l].astype(BF16), wg=P['w_gate'][l].astype(BF16), wu=P['w_up'][l].astype(BF16),
        wd=P['w_down'][l].astype(BF16),
    )


def _trunk(x, mods, st_hg, st_ssm, st_conv, st_wkv, st_shift, lb_all, layers, g_final, dims, seq, tm, tf):
    hgw, n_x, n_bc, nh_mb, rww, lw, la, lg = dims
    grp, rows, d = x.shape
    depth = len(layers)
    n_grp = n_bc // (2 * HEAD)
    outs = ([], [], [], [], [])
    if seq:
        st_hg = jnp.swapaxes(st_hg, -1, -2)
    else:
        st_hg = jnp.transpose(st_hg, (0, 2, 4, 3, 1))
        st_ssm = jnp.transpose(st_ssm, (0, 2, 3, 4, 1))
        st_wkv = jnp.transpose(st_wkv, (0, 2, 3, 4, 1))
    s_hg = s_ssm = s_wkv = None
    for l in range(depth):
        L = layers[l]
        mod = (mods, l)
        lb = lb_all[l].reshape(1, -1)
        hg_g, d_exp, mb_g, ln_g, ln_b = L['out_params']
        cw_rw = L['w_rw'].shape[-1]
        if seq:
            p_hg, p_mb, rw_p = _in_proj(
                x, L['g1'], mod, [L['w_hg'], L['w_mb'], L['w_rw']], [(_split_epilogue, 1)] * 3, [],
                [4 * hgw, 2 * n_x + n_bc + LANES, cw_rw], tm, "in_proj")
            shift8 = jnp.pad(L['rw_cols'](st_shift[l])[:, None, :], ((0, 0), (SUBLANES - 1, 0), (0, 0)))
            conv8 = jnp.pad(st_conv[l], ((0, 0), (SUBLANES - st_conv[l].shape[1], 0), (0, 0)))
            a_hg, s_hg = _hg_chunked(p_hg, (lb, jnp.log(lb), jnp.log1p(-lb), hg_g), st_hg[l])
            y_mb, s_ssm, conv_tail = _ssd_chunked(p_mb, conv8, L['mb_params'] + (d_exp, mb_g), st_ssm[l], n_x, n_grp)
            c_rw, s_wkv, shift_tail = _rw_chunked(rw_p, shift8, L['rw_params'] + (ln_g, ln_b), st_wkv[l])
            conv_new = conv_tail[:, SUBLANES - st_conv[l].shape[1]:, :]
            shift_new = shift_tail[:, SUBLANES - 1, :]
        else:
            q, w_hg, k_hg, i_hg, g_hg, xbc, z, dtr, rw_p = _in_proj(
                x, L['g1'], mod, [L['w_hg'], L['w_mb'], L['w_rw']],
                [(_hg_epilogue, 5), (_split_epilogue, 3), (_split_epilogue, 1)], [lb],
                [hgw] * 5 + [n_x + n_bc, n_x, LANES, cw_rw], tm, "in_proj_step")
            conv_new = jnp.concatenate([st_conv[l][:, 1:, :], xbc[0][:, None, :]], axis=1)
            shift_new = rw_p[0]
            r_rw, w_rw, k_rw, v_rw, nkk, kka, g_rw, bv = _rw_pre_step(
                rw_p, L['rw_cols'](st_shift[l])[None], L['rw_params'], tm)
            w_mb, v_mb, k_mb, r_mb, mx, zs = _mb_pre_step(
                xbc, z, dtr, tuple(st_conv[l][None, :, 2 - j, :] for j in range(3)), L['mb_params'], tm)
            o_hg, s_hg = _recurrence((w_hg, k_hg, q), i_hg, st_hg, l, s_hg, (1, 1, 1))
            o_mb, s_ssm = _recurrence((w_mb, k_mb, r_mb), v_mb, st_ssm, l, s_ssm, (1, 2, 2))
            o_rw, s_wkv = _recurrence((w_rw, k_rw, r_rw, nkk, kka), v_rw, st_wkv, l, s_wkv, (1, 1, 1))
            a_hg, y_mb, c_rw = _mix_finish((o_hg, g_hg, o_mb, mx, zs, o_rw, g_rw, bv), L['out_params'], tm, n_grp)
        x = _out_ffn(x, (a_hg, y_mb, c_rw), mod, L['wo'], L['g2'], L['wg'], L['wu'], L['wd'], g_final, tm, tf,
                     final_norm=(l == depth - 1))
        o3 = 3 * rww
        shift_new = jnp.concatenate([shift_new[:, :o3], shift_new[:, o3:o3 + lw],
                                     shift_new[:, o3 + LANES:o3 + LANES + la],
                                     shift_new[:, o3 + 2 * LANES:o3 + 2 * LANES + lg]], axis=-1)
        for lst, s in zip(outs, (s_hg, s_ssm, conv_new, s_wkv, shift_new)):
            lst.append(s)
    if seq:
        hg_new, ssm_new, conv_new, wkv_new, shift_new = [jnp.stack(lst) for lst in outs]
        return x, [jnp.swapaxes(hg_new, -1, -2), ssm_new, conv_new, wkv_new, shift_new]
    return x, [jnp.transpose(s_hg, (0, 4, 1, 3, 2)), jnp.transpose(s_ssm, (0, 4, 1, 2, 3)), jnp.stack(outs[2]),
               jnp.transpose(s_wkv, (0, 4, 1, 2, 3)), jnp.stack(outs[4])]


def kernel(x_prompt, x_sample, state_hgrn, state_ssm, state_conv, state_wkv, state_shift, c_prompt, c_sample,
           w_ada, b_ada, g_norm1, w_in, hg_lb_logits, hg_norm_g, mb_conv_w, mb_conv_b, mb_dt_bias, mb_A_log,
           mb_D, mb_norm_g, rw_mu, rw_w0, rw_w2, rw_a0, rw_a2, rw_g2, rw_k_k, rw_k_a, rw_r_k, rw_ln_g, rw_ln_b,
           w_out, g_norm2, w_gate, w_up, w_down, g_final):
    P = dict(g_norm1=g_norm1, w_in=w_in, hg_norm_g=hg_norm_g, mb_conv_w=mb_conv_w, mb_conv_b=mb_conv_b,
             mb_dt_bias=mb_dt_bias, mb_A_log=mb_A_log, mb_D=mb_D, mb_norm_g=mb_norm_g, rw_mu=rw_mu, rw_w0=rw_w0,
             rw_w2=rw_w2, rw_a0=rw_a0, rw_a2=rw_a2, rw_g2=rw_g2, rw_k_k=rw_k_k, rw_k_a=rw_k_a,
             rw_r_k=rw_r_k.reshape(rw_r_k.shape[0], -1), rw_ln_g=rw_ln_g, rw_ln_b=rw_ln_b, w_out=w_out,
             g_norm2=g_norm2, w_gate=w_gate, w_up=w_up, w_down=w_down)
    depth = w_in.shape[0]
    bsz, t_len, d = x_prompt.shape
    dec = x_sample.shape[0]
    hgw = hg_lb_logits.shape[-1]
    n_x = mb_norm_g.shape[-1]
    n_bc = mb_conv_w.shape[-1] - n_x
    nh_mb = mb_dt_bias.shape[-1]
    rww = rw_w0.shape[-1]
    lw, la, lg = rw_w2.shape[1], rw_a2.shape[1], rw_g2.shape[1]
    dims = (hgw, n_x, n_bc, nh_mb, rww, lw, la, lg)
    layers = [_prep_layer(l, P, dims) for l in range(depth)]
    gf = g_final.reshape(1, d)

    cs = jnp.cumsum(jax.nn.softmax(hg_lb_logits.astype(F32), axis=0), axis=0)
    lb_all = cs - cs[:1]

    n_c = bsz + dec
    pad_c = -n_c % (2 * SUBLANES)
    mods_p, mods_s = _ada(jnp.pad(jnp.concatenate([c_prompt, c_sample], axis=0), ((0, pad_c), (0, 0))), bsz, dec,
                          w_ada, b_ada)
    mods_p = mods_p[:, :, None, :]
    mods_s = mods_s[:, None, :, :]

    dt = x_prompt.dtype
    zeros = lambda a: jnp.zeros((depth, bsz) + a.shape[2:], dt)
    y_p, st_p = _trunk(x_prompt, mods_p, zeros(state_hgrn), zeros(state_ssm), zeros(state_conv), zeros(state_wkv),
                       zeros(state_shift), lb_all, layers, gf, dims, True, min(512, t_len), 256)
    y_s, st_s = _trunk(x_sample.reshape(1, dec, d), mods_s, state_hgrn, state_ssm, state_conv, state_wkv,
                       state_shift, lb_all, layers, gf, dims, False, dec, 256)
    hg_p, ssm_p, conv_p, wkv_p, shift_p = st_p
    hg_s, ssm_s, conv_s, wkv_s, shift_s = st_s
    return (y_p, y_s.reshape(x_sample.shape), hg_p, hg_s, ssm_p, ssm_s, conv_p, conv_s, wkv_p, wkv_s,
            shift_p, shift_s)
```

```python
import functools

import jax
import jax.numpy as jnp
from jax import lax
from jax.experimental import pallas as pl
from jax.experimental.pallas import tpu as pltpu

F32 = jnp.float32
BF16 = jnp.bfloat16

LANES = 128
SUBLANES = 8
HEAD = 64
RMS_EPS = 1e-6
RW_GN_EPS = 64e-5
VMEM_LIMIT = 48 * 1024 * 1024
SSD_CHUNK = 128
HG_BLOCK = 128
HG_SUB = 16
RW_CHUNK = 64
RW_INV_BLOCK = 16
SEQ_PER_STEP = 8
SSD_SEQ_PER_STEP = 2
RW_SEQ_PER_STEP = 8

_NN = (((1,), (0,)), ((), ()))
_NT = (((1,), (1,)), ((), ()))
_TN = (((0,), (0,)), ((), ()))


def _cparams(sem):
    return pltpu.CompilerParams(dimension_semantics=sem, vmem_limit_bytes=VMEM_LIMIT)


def _sigmoid(x):
    return 1.0 / (1.0 + jnp.exp(-x))


def _silu(x):
    return x * _sigmoid(x)


def _softplus(x):
    return jnp.maximum(x, 0.0) + jnp.log1p(jnp.exp(-jnp.abs(x)))


def _split3(x):
    hi = x.astype(BF16)
    r1 = x - hi.astype(F32)
    mid = r1.astype(BF16)
    return hi, mid, (r1 - mid.astype(F32)).astype(BF16)


def _seg_sum(x, seg=HEAD):
    n = x.shape[-1]
    i = lax.broadcasted_iota(jnp.int32, (3 * n, n), 0) % n // seg
    j = lax.broadcasted_iota(jnp.int32, (3 * n, n), 1) // seg
    return _dot01(x, i == j)


def _dot01(x, mask3):
    return jnp.dot(jnp.concatenate(_split3(x), axis=-1), jnp.where(mask3, 1.0, 0.0).astype(BF16),
                   preferred_element_type=F32)


def _rmsnorm_mod(x, g, sc, sh):
    h = x * lax.rsqrt(jnp.mean(x * x, axis=-1, keepdims=True) + RMS_EPS) * g
    return h * (1.0 + sc) + sh


def _ada_kernel(c_ref, w_ref, b_ref, op_ref, os_ref):
    a = _silu(c_ref[...]).astype(BF16)
    y = jnp.dot(a, w_ref[...].astype(BF16), preferred_element_type=F32) + b_ref[...]
    n_p = op_ref.shape[0]
    op_ref[...] = y[0:n_p]
    os_ref[...] = y[n_p:n_p + os_ref.shape[0]]


def _ada(c_all, n_p, n_s, w_ada, b_ada):
    depth, d, n = w_ada.shape
    rows = c_all.shape[0]
    tn = n // 4
    return pl.pallas_call(
        _ada_kernel,
        grid=(depth, n // tn),
        in_specs=[pl.BlockSpec((rows, d), lambda l, j: (0, 0)),
                  pl.BlockSpec((None, d, tn), lambda l, j: (l, 0, j)),
                  pl.BlockSpec((None, 1, tn), lambda l, j: (l, 0, j))],
        out_specs=[pl.BlockSpec((None, n_p, tn), lambda l, j: (l, 0, j)),
                   pl.BlockSpec((None, n_s, tn), lambda l, j: (l, 0, j))],
        out_shape=[jax.ShapeDtypeStruct((depth, n_p, n), F32), jax.ShapeDtypeStruct((depth, n_s, n), F32)],
        compiler_params=_cparams(("parallel", "parallel")),
        name="ada_mod",
    )(c_all, w_ada, b_ada.reshape(depth, 1, n))


def _row_spec(tm, width, col=0):
    return pl.BlockSpec((None, tm, width), lambda g, i: (g, i, col))


def _mod_spec(mod, tm, d, idx):
    arr, l = mod
    if arr.shape[2] == 1:
        return pl.BlockSpec((None, None, 1, d), lambda g, i: (l, g, 0, idx))
    return pl.BlockSpec((None, None, tm, d), lambda g, i: (l, g, i, idx))


def _const_spec(a):
    nd = a.ndim
    return pl.BlockSpec(a.shape, lambda g, i: (0,) * nd)


def _resident_spec(a):
    nd = a.ndim
    return pl.BlockSpec(a.shape, lambda g, i: (0,) * nd, pipeline_mode=pl.Buffered(1))


def _in_proj_kernel(x_ref, g_ref, sc_ref, sh_ref, *rest, n_w, epilogues, n_extra):
    ws, rest = rest[:n_w], rest[n_w:]
    extra, outs = rest[:n_extra], rest[n_extra:]
    h = _rmsnorm_mod(x_ref[...], g_ref[...], sc_ref[...], sh_ref[...]).astype(BF16)
    for w_ref, (epilogue, n_out) in zip(ws, epilogues):
        epilogue(jnp.dot(h, w_ref[...], preferred_element_type=F32), extra, outs[:n_out])
        outs = outs[n_out:]


def _split_epilogue(y, extra, outs):
    col = 0
    for o in outs:
        wd = o.shape[-1]
        o[...] = y[:, col:col + wd]
        col += wd


def _hg_gates(zf, lb, log_lb, log_1mlb, log_space):
    k = (1.0 - lb) * _sigmoid(-zf)
    if not log_space:
        return lb + (1.0 - lb) * _sigmoid(zf), k
    b = log_1mlb - _softplus(-zf)
    return jnp.maximum(log_lb, b) + jnp.log1p(jnp.exp(-jnp.abs(log_lb - b))), k


def _hg_epilogue(y, extra, outs):
    lb = extra[0][...]
    q_ref, w_ref, k_ref, i_ref, g_ref = outs
    c = lb.shape[-1]
    q_ref[...] = y[:, 0:c]
    w_ref[...], k_ref[...] = _hg_gates(y[:, c:2 * c], lb, None, None, False)
    i_ref[...] = y[:, 2 * c:3 * c]
    g_ref[...] = y[:, 3 * c:4 * c]


def _in_proj(x, g, mod, ws, epilogues, extra, out_widths, tm, name):
    grp, rows, d = x.shape
    kern = functools.partial(_in_proj_kernel, n_w=len(ws), epilogues=epilogues, n_extra=len(extra))
    return pl.pallas_call(
        kern,
        grid=(grp, rows // tm),
        in_specs=[_row_spec(tm, d), _const_spec(g), _mod_spec(mod, tm, d, 1), _mod_spec(mod, tm, d, 0)]
        + [_resident_spec(w) for w in ws] + [_const_spec(e) for e in extra],
        out_specs=[_row_spec(tm, wd) for wd in out_widths],
        out_shape=[jax.ShapeDtypeStruct((grp, rows, wd), F32) for wd in out_widths],
        compiler_params=_cparams(("parallel", "parallel")),
        name=name,
    )(x, g, mod[0], mod[0], *ws, *extra)


def _shift_rows(cur, prev8, j):
    rolled = pltpu.roll(cur, j, 0)
    row8 = lax.broadcasted_iota(jnp.int32, (SUBLANES, 1), 0)
    head = jnp.where(row8 < j, pltpu.roll(prev8, j, 0), rolled[0:SUBLANES])
    if cur.shape[0] == SUBLANES:
        return head
    return jnp.concatenate([head, rolled[SUBLANES:]], axis=0)


def _rw_math(p, prev, mu, w0, a0, kk_s, ka_s, rk, w2, a2, g2, log_space):
    c = w0.shape[-1]
    lo = w2.shape[0]
    mixd = p + (prev - p) * mu
    r = mixd[:, 0:c]
    k0 = mixd[:, c:2 * c]
    v = mixd[:, 2 * c:3 * c]
    wl = mixd[:, 3 * c:3 * c + lo]
    al = mixd[:, 3 * c + lo:3 * c + 2 * lo]
    gl = mixd[:, 3 * c + 2 * lo:3 * c + 3 * lo]
    z = -(w0 + jnp.dot(jnp.tanh(wl).astype(BF16), w2, preferred_element_type=F32))
    w = -jnp.exp(-_softplus(z) - 0.5)
    if not log_space:
        w = jnp.exp(w)
    a = _sigmoid(a0 + jnp.dot(al.astype(BF16), a2, preferred_element_type=F32))
    g = jnp.dot(_sigmoid(gl).astype(BF16), g2, preferred_element_type=F32)
    kk = k0 * kk_s
    kk = kk * lax.rsqrt(jnp.maximum(_seg_sum(kk * kk), 1e-24))
    k = k0 * (1.0 + (a - 1.0) * ka_s)
    bv = _seg_sum(r * k * rk) * v
    return r, w, k, v, -kk, kk * a, g, bv


def _rw_pre_step_kernel(p_ref, prev_ref, *rest):
    params, outs = rest[:9], rest[9:]
    res = _rw_math(p_ref[...], prev_ref[...], *[a[...] for a in params], False)
    for o, val in zip(outs, res):
        o[...] = val


def _rw_pre_step(p, prev, params, tm):
    grp, rows, cp = p.shape
    c = params[1].shape[-1]
    return pl.pallas_call(
        _rw_pre_step_kernel, grid=(grp, rows // tm),
        in_specs=[_row_spec(tm, cp), _row_spec(tm, cp)] + [_const_spec(a) for a in params],
        out_specs=[_row_spec(tm, c)] * 8, out_shape=[jax.ShapeDtypeStruct((grp, rows, c), F32)] * 8,
        compiler_params=_cparams(("parallel", "parallel")),
        name="rw_pre_step")(p, prev, *params)


def _mb_math(cur, s1, s2, s3, z, dtr, cw, cb, dtb, a_exp, a_pad, chunked):
    n_x = z.shape[-1]
    conv = s3 * cw[0:1] + s2 * cw[1:2] + s1 * cw[2:3] + cur * cw[3:4] + cb
    act = _silu(conv)
    mx = act[:, 0:n_x]
    half = (act.shape[-1] - n_x) // 2
    mb = act[:, n_x:n_x + half]
    mc = act[:, n_x + half:]
    dt = _softplus(dtr + dtb)
    nd = dt.shape[-1]
    i = lax.broadcasted_iota(jnp.int32, (3 * nd, n_x), 0) % nd
    j = lax.broadcasted_iota(jnp.int32, (3 * nd, n_x), 1)
    dte = _dot01(dt, j // HEAD == i)
    if chunked:
        return dt * a_pad, mx * dte, act[:, n_x:], mx, _silu(z)
    i = lax.broadcasted_iota(jnp.int32, (3 * half, 2 * half), 0) % half
    j = lax.broadcasted_iota(jnp.int32, (3 * half, 2 * half), 1)
    rep = i == (j // LANES) * HEAD + j % HEAD
    return jnp.exp(dte * a_exp), mx * dte, _dot01(mb, rep), _dot01(mc, rep), mx, _silu(z)


def _mb_pre_step_kernel(xbc_ref, s1_ref, s2_ref, s3_ref, z_ref, dt_ref, cw, cb, dtb, a_exp, a_pad, *outs):
    res = _mb_math(xbc_ref[...], s1_ref[...], s2_ref[...], s3_ref[...], z_ref[...], dt_ref[...],
                   cw[...], cb[...], dtb[...], a_exp[...], a_pad[...], False)
    for o, val in zip(outs, res):
        o[...] = val


def _mb_pre_step(xbc, z, dtr, conv_rows, params, tm):
    grp, rows, cx = xbc.shape
    n_x = z.shape[-1]
    n_k = cx - n_x
    widths = [n_x, n_x, n_k, n_k, n_x, n_x]
    return pl.pallas_call(
        _mb_pre_step_kernel, grid=(grp, rows // tm),
        in_specs=[_row_spec(tm, cx)] * 4 + [_row_spec(tm, n_x), _row_spec(tm, dtr.shape[-1])]
        + [_const_spec(a) for a in params],
        out_specs=[_row_spec(tm, wd) for wd in widths],
        out_shape=[jax.ShapeDtypeStruct((grp, rows, wd), F32) for wd in widths],
        compiler_params=_cparams(("parallel", "parallel")),
        name="mb_pre_step")(xbc, *conv_rows, z, dtr, *params)


def _hg_finish(o, g, hg_g):
    return o * lax.rsqrt(_seg_sum(o * o) * (1.0 / HEAD) + RMS_EPS) * hg_g * _silu(g)


def _mb_finish(o, mx, zs, d_exp, mb_g, n_grp):
    y = (o + d_exp * mx) * zs
    gw = y.shape[-1] // n_grp
    return jnp.concatenate(
        [y[:, g * gw:(g + 1) * gw]
         * lax.rsqrt(jnp.mean(y[:, g * gw:(g + 1) * gw] ** 2, axis=-1, keepdims=True) + RMS_EPS)
         for g in range(n_grp)], axis=-1) * mb_g


def _rw_finish(o, g, bv, ln_g, ln_b):
    mu = _seg_sum(o) * (1.0 / HEAD)
    dlt = o - mu
    var = _seg_sum(dlt * dlt) * (1.0 / HEAD)
    return (dlt * lax.rsqrt(var + RW_GN_EPS) * ln_g + ln_b + bv) * g


def _mix_finish_kernel(ohg_ref, ghg_ref, omb_ref, mx_ref, zs_ref, orw_ref, grw_ref, bv_ref,
                       hg_g, d_exp, mb_g, ln_g, ln_b, a_ref, y_ref, c_ref, *, n_grp):
    a_ref[...] = _hg_finish(ohg_ref[...], ghg_ref[...], hg_g[...]).astype(BF16)
    y_ref[...] = _mb_finish(omb_ref[...], mx_ref[...], zs_ref[...], d_exp[...], mb_g[...], n_grp).astype(BF16)
    c_ref[...] = _rw_finish(orw_ref[...], grw_ref[...], bv_ref[...], ln_g[...], ln_b[...]).astype(BF16)


def _mix_finish(acts, params, tm, n_grp):
    grp, rows, _ = acts[0].shape
    widths = [acts[0].shape[-1], acts[2].shape[-1], acts[5].shape[-1]]
    return pl.pallas_call(
        functools.partial(_mix_finish_kernel, n_grp=n_grp),
        grid=(grp, rows // tm),
        in_specs=[_row_spec(tm, a.shape[-1]) for a in acts] + [_const_spec(a) for a in params],
        out_specs=[_row_spec(tm, wd) for wd in widths],
        out_shape=[jax.ShapeDtypeStruct((grp, rows, wd), BF16) for wd in widths],
        compiler_params=_cparams(("parallel", "parallel")),
        name="mix_finish",
    )(*acts, *params)


def _block_diag_mask():
    i = lax.broadcasted_iota(jnp.int32, (LANES, LANES), 0) // HEAD
    j = lax.broadcasted_iota(jnp.int32, (LANES, LANES), 1) // HEAD
    return i == j


def _pair_state_load(s0_ref, s_scr, nhp):
    for p in range(nhp):
        s_scr[p] = jnp.zeros((LANES, LANES), F32)
        s_scr[p, 0:HEAD, 0:HEAD] = s0_ref[2 * p]
        s_scr[p, HEAD:LANES, HEAD:LANES] = s0_ref[2 * p + 1]


def _pair_state_store(s_scr, so_ref, nhp):
    for p in range(nhp):
        so_ref[2 * p] = s_scr[p, 0:HEAD, 0:HEAD]
        so_ref[2 * p + 1] = s_scr[p, HEAD:LANES, HEAD:LANES]


def _chunk_call(kern, proj, carry0, params, s0, out_width, tb, nb, scratch, name):
    grp, rows, c = proj.shape
    nb = min(nb, grp)
    sspec = pl.BlockSpec((nb,) + s0.shape[1:], lambda g, i: (g, 0, 0, 0))
    in_specs = [pl.BlockSpec((nb, tb, c), lambda g, i: (g, i, 0))]
    args = [proj]
    out_specs = [pl.BlockSpec((nb, tb, out_width), lambda g, i: (g, i, 0)), sspec]
    out_shape = [jax.ShapeDtypeStruct((grp, rows, out_width), BF16), jax.ShapeDtypeStruct(s0.shape, F32)]
    if carry0 is not None:
        cspec = pl.BlockSpec((nb,) + carry0.shape[1:], lambda g, i: (g, 0, 0))
        in_specs.append(cspec)
        args.append(carry0)
        out_specs.append(cspec)
        out_shape.append(jax.ShapeDtypeStruct(carry0.shape, F32))
    return pl.pallas_call(
        functools.partial(kern, nb=nb), grid=(grp // nb, rows // tb),
        in_specs=in_specs + [_const_spec(a) for a in params] + [sspec],
        out_specs=out_specs,
        out_shape=out_shape,
        scratch_shapes=[s(nb) for s in scratch],
        compiler_params=_cparams(("parallel", "arbitrary")),
        name=name,
    )(*args, *params, s0)


def _ssd_chunk_kernel(p_ref, c8_ref, cw, cb, dtb, a_exp, a_pad, d_exp, mb_g, s0_ref, y_ref, so_ref, co_ref,
                      h_scr, carry_scr, y_scr, *, nb, nh, ng, n_x):
    c = pl.program_id(1)

    @pl.when(c == 0)
    def _():
        h_scr[...] = s0_ref[...]
        carry_scr[...] = c8_ref[...]

    n = p_ref.shape[1]
    n_bc = 2 * ng * HEAD
    rg = nh // ng
    row = lax.broadcasted_iota(jnp.int32, (n, n), 0)
    col = lax.broadcasted_iota(jnp.int32, (n, n), 1)
    tril = row >= col
    for s in range(nb):
        work = []
        cur = p_ref[s, :, 0:n_x + n_bc]
        prev8 = carry_scr[s]
        dta, xdt, bc, mx, zs = _mb_math(
            cur, _shift_rows(cur, prev8, 1), _shift_rows(cur, prev8, 2), _shift_rows(cur, prev8, 3),
            p_ref[s, :, n_x + n_bc:2 * n_x + n_bc], p_ref[s, :, 2 * n_x + n_bc:],
            cw[...], cb[...], dtb[...], a_exp[...], a_pad[...], True)
        carry_scr[s] = cur[n - SUBLANES:]
        acum = _cumsum_rows(dta)
        acum_t = acum.T
        bgs = [bc[:, g * HEAD:(g + 1) * HEAD].astype(BF16) for g in range(ng)]
        cgs = [bc[:, (ng + g) * HEAD:(ng + g + 1) * HEAD].astype(BF16) for g in range(ng)]
        cbs = [lax.dot_general(cg, bg, _NT, preferred_element_type=F32) for cg, bg in zip(cgs, bgs)]
        for h in range(nh):
            g = h // rg
            a_col = acum[:, h:h + 1]
            a_tot = acum[n - 1:n, h:h + 1]
            lmat = jnp.exp(jnp.where(tril, a_col - acum_t[h:h + 1, :], -jnp.inf))
            xh = xdt[:, h * HEAD:(h + 1) * HEAD]
            work.append(dict(h=h, a_col=a_col, a_tot=a_tot, sh=h_scr[s, h], xh=xh.astype(BF16), bg=bgs[g],
                             cg=cgs[g], gm=(cbs[g] * lmat).astype(BF16),
                             xd=(xh * jnp.exp(a_tot - a_col)).astype(BF16)))
        for q in work:
            q['intra'] = jnp.dot(q['gm'], q['xh'], preferred_element_type=F32)
            q['inter'] = lax.dot_general(q['cg'], q['sh'].astype(BF16), _NT, preferred_element_type=F32)
            q['upd'] = lax.dot_general(q['xd'], q['bg'], _TN, preferred_element_type=F32)
        for q in work:
            h = q['h']
            y_scr[s, :, h * HEAD:(h + 1) * HEAD] = q['intra'] + jnp.exp(q['a_col']) * q['inter']
            h_scr[s, h] = jnp.exp(q['a_tot']) * q['sh'] + q['upd']
        y_ref[s] = _mb_finish(y_scr[s], mx, zs, d_exp[...], mb_g[...], ng).astype(BF16)

    @pl.when(c == pl.num_programs(1) - 1)
    def _():
        so_ref[...] = h_scr[...]
        co_ref[...] = carry_scr[...]


def _ssd_chunked(proj, conv8, params, s0, n_x, ng):
    nh = s0.shape[1]
    tb = min(SSD_CHUNK, proj.shape[1])
    scratch = [lambda nb: pltpu.VMEM((nb,) + s0.shape[1:], F32),
               lambda nb: pltpu.VMEM((nb, SUBLANES, conv8.shape[-1]), F32),
               lambda nb: pltpu.VMEM((nb, tb, n_x), F32)]
    return _chunk_call(functools.partial(_ssd_chunk_kernel, nh=nh, ng=ng, n_x=n_x), proj, conv8, params, s0, n_x,
                       tb, SSD_SEQ_PER_STEP, scratch, "ssd_chunk")


def _hg_chunk_kernel(p_ref, lb_ref, llb_ref, l1m_ref, hg_g, s0_ref, a_ref, so_ref, s_scr, o_scr,
                     *, nb, nhp, sub):
    c = pl.program_id(1)

    @pl.when(c == 0)
    def _():
        for s in range(nb):
            _pair_state_load(s0_ref.at[s], s_scr.at[s], nhp)

    tb = p_ref.shape[1]
    cw = lb_ref.shape[-1]
    pos = lax.broadcasted_iota(jnp.int32, (tb, 1), 0) % sub
    bd = _block_diag_mask()
    seg = jnp.where(bd, 1.0, 0.0).astype(BF16)
    causal = (lax.broadcasted_iota(jnp.int32, (sub, sub, 1), 1) <= lax.broadcasted_iota(jnp.int32, (sub, sub, 1), 0))
    bs, ks = [], []
    for s in range(nb):
        b, k = _hg_gates(p_ref[s, :, cw:2 * cw], lb_ref[...], llb_ref[...], l1m_ref[...], True)
        step = 1
        while step < sub:
            b = b + jnp.where(pos >= step, pltpu.roll(b, step, 0), 0.0)
            step *= 2
        bs.append(b)
        ks.append(k)
    probs = [(s, p) for s in range(nb) for p in range(nhp)]
    states = {sp: s_scr[sp[0], sp[1]] for sp in probs}
    for ci in range(tb // sub):
        rs = slice(ci * sub, (ci + 1) * sub)
        work = []
        for s, p in probs:
            ls = slice(p * LANES, (p + 1) * LANES)
            bc = bs[s][rs, ls]
            kc = ks[s][rs, ls]
            qc = p_ref[s, rs, p * LANES:(p + 1) * LANES]
            vc = p_ref[s, rs, 2 * cw + p * LANES:2 * cw + (p + 1) * LANES]
            btot = bc[sub - 1:sub]
            o = lax.dot_general((qc * jnp.exp(bc)).astype(BF16), states[s, p].astype(BF16), _NT,
                                preferred_element_type=F32)
            e = jnp.exp(jnp.where(causal, bc[:, None, :] - bc[None, :, :], -jnp.inf))
            x = (e * kc[None, :, :] * qc[:, None, :]).astype(BF16).reshape(sub * sub, LANES)
            att = jnp.dot(x, seg, preferred_element_type=F32).reshape(sub, sub, LANES)
            kd = (kc * jnp.exp(btot - bc)).astype(BF16)
            upd = lax.dot_general(vc.astype(BF16), kd, _TN, preferred_element_type=F32)
            work.append((s, p, ls, o, att, vc, jnp.exp(btot), upd))
        for s, p, ls, o, att, vc, dec, upd in work:
            o_scr[s, rs, ls] = o + jnp.sum(att * vc[None, :, :], axis=1)
            states[s, p] = states[s, p] * dec + jnp.where(bd, upd, 0.0)
    for s, p in probs:
        s_scr[s, p] = states[s, p]
    for s in range(nb):
        a_ref[s] = _hg_finish(o_scr[s], p_ref[s, :, 3 * cw:4 * cw], hg_g[...]).astype(BF16)

    @pl.when(c == pl.num_programs(1) - 1)
    def _():
        for s in range(nb):
            _pair_state_store(s_scr.at[s], so_ref.at[s], nhp)


def _hg_chunked(proj, params, s0):
    nhp = s0.shape[1] // 2
    tb = min(HG_BLOCK, proj.shape[1])
    cw = proj.shape[-1] // 4
    scratch = [lambda nb: pltpu.VMEM((nb, nhp, LANES, LANES), F32), lambda nb: pltpu.VMEM((nb, tb, cw), F32)]
    return _chunk_call(functools.partial(_hg_chunk_kernel, nhp=nhp, sub=HG_SUB), proj, None, params, s0, cw, tb,
                       SEQ_PER_STEP, scratch, "hg_chunk")


def _split2(x):
    hi = x.astype(BF16)
    return hi, (x - hi.astype(F32)).astype(BF16)


def _dot3(a, b, dims=_NN):
    ah, am = _split2(a)
    bh, bm = _split2(b)
    (ca,), (cb,) = dims[0]
    return lax.dot_general(jnp.concatenate([ah, ah, am], axis=ca), jnp.concatenate([bh, bm, bh], axis=cb), dims,
                           preferred_element_type=F32)


def _cumsum_rows(x):
    n = x.shape[0]
    tril3 = (lax.broadcasted_iota(jnp.int32, (n, 3 * n), 0) >= lax.broadcasted_iota(jnp.int32, (n, 3 * n), 1) % n)
    return jnp.dot(jnp.where(tril3, 1.0, 0.0).astype(BF16), jnp.concatenate(_split3(x), axis=0),
                   preferred_element_type=F32)


def _tri_inverse(n_mats, blk, top):
    n = n_mats[0].shape[0]
    ri = lax.broadcasted_iota(jnp.int32, (n, n), 0)
    ci = lax.broadcasted_iota(jnp.int32, (n, n), 1)
    eye = jnp.where(ri == ci, 1.0, 0.0)
    pws = [jnp.where(ri // blk == ci // blk, m, 0.0) for m in n_mats]
    ts = [eye + pw for pw in pws]
    s = 2
    while s < blk:
        pws = [_dot3(pw, pw) for pw in pws]
        ts = [t + _dot3(t, pw) for t, pw in zip(ts, pws)]
        s *= 2
    size = blk
    while size < top:
        in_2blk = ri // (2 * size) == ci // (2 * size)
        in_blk = ri // size == ci // size
        offs = [jnp.where(in_2blk, m, 0.0) - jnp.where(in_blk, m, 0.0) for m in n_mats]
        mids = [_dot3(t, off) for t, off in zip(ts, offs)]
        ts = [t + _dot3(mid, t) for t, mid in zip(ts, mids)]
        size *= 2
    return ts


def _rw_chunk_kernel(p_ref, c8_ref, mu, w0, a0, kk_s, ka_s, rk, w2, a2, g2, ln_g, ln_b, s0_ref, c_ref, so_ref,
                     co_ref, s_scr, carry_scr, *, nb, nhp):
    c = pl.program_id(1)

    @pl.when(c == 0)
    def _():
        carry_scr[...] = c8_ref[...]
        for s in range(nb):
            _pair_state_load(s0_ref.at[s], s_scr.at[s], nhp)

    n = p_ref.shape[1]
    n2 = 2 * n
    ri = lax.broadcasted_iota(jnp.int32, (n2, n2), 0)
    ci = lax.broadcasted_iota(jnp.int32, (n2, n2), 1)
    tril = ri >= ci
    strict = ri > ci
    m0 = lax.broadcasted_iota(jnp.int32, (1, LANES), 1) < HEAD
    bd = _block_diag_mask()
    params = [a[...] for a in (mu, w0, a0, kk_s, ka_s, rk, w2, a2, g2)]

    def stack(x):
        return jnp.concatenate([jnp.where(m0, x, 0.0), jnp.where(m0, 0.0, x)], axis=0)

    probs = []
    curs = [p_ref[s] for s in range(nb)]
    prevs = [_shift_rows(cur, carry_scr[s], 1) for s, cur in enumerate(curs)]
    for s, cur in enumerate(curs):
        carry_scr[s] = cur[n - SUBLANES:]
    acts = _rw_math(jnp.concatenate(curs, axis=0), jnp.concatenate(prevs, axis=0), *params, True)
    g_all, bv_all = acts[6], acts[7]
    for s in range(nb):
        r, lw, k, v, a, b = [t[s * n:(s + 1) * n] for t in acts[:6]]
        cum = _cumsum_rows(lw)
        e_n = jnp.exp(-cum)
        ctot = cum[n - 1:n]
        e_rel = jnp.exp(ctot - cum)
        at = a * jnp.exp(cum - lw)
        rt = r * jnp.exp(cum)
        bt = b * e_n
        kt = k * e_n
        bh = b * e_rel
        kh = k * e_rel
        for p in range(nhp):
            ls = slice(p * LANES, (p + 1) * LANES)
            probs.append(dict(
                s=s, p=p, ls=ls, vp=v[:, ls], dec=jnp.exp(ctot[:, ls]),
                ar=jnp.concatenate([stack(at[:, ls]), stack(rt[:, ls])], axis=0),
                bk=jnp.concatenate([stack(bt[:, ls]), stack(kt[:, ls])], axis=0),
                bkh=jnp.concatenate([bh[:, ls], kh[:, ls]], axis=0)))
    for q in probs:
        q['st'] = s_scr[q['s'], q['p']]
        q['a_s'] = _dot3(q['ar'], q['st'], _NT)
    for q in probs:
        q['g'] = _dot3(q['ar'], q['bk'], _NT)
    t_invs = _tri_inverse([jnp.where(strict, q['g'][0:n2, 0:n2], 0.0) for q in probs], RW_INV_BLOCK, n)
    for q in probs:
        q['v2'] = stack(q['vp'])
        q['rhs'] = q['a_s'][0:n2] + _dot3(jnp.where(strict, q['g'][0:n2, n2:], 0.0), q['v2'])
    for q, t_inv in zip(probs, t_invs):
        q['u2'] = _dot3(t_inv, q['rhs'])
    outs = {}
    for q in probs:
        g = q['g']
        pm = jnp.concatenate([jnp.where(tril, g[n2:, 0:n2], 0.0), jnp.where(tril, g[n2:, n2:], 0.0)], axis=1)
        o2 = q['a_s'][n2:] + _dot3(pm, jnp.concatenate([q['u2'], q['v2']], axis=0))
        outs[q['s'], q['p']] = o2[0:n] + o2[n:]
    for q in probs:
        u2 = q['u2']
        upd = _dot3(jnp.concatenate([u2[0:n] + u2[n:], q['vp']], axis=0), q['bkh'], _TN)
        s_scr[q['s'], q['p']] = q['st'] * q['dec'] + jnp.where(bd, upd, 0.0)
    o_all = jnp.concatenate([jnp.concatenate([outs[s, p] for p in range(nhp)], axis=-1) for s in range(nb)], axis=0)
    c_all = _rw_finish(o_all, g_all, bv_all, ln_g[...], ln_b[...]).astype(BF16)
    for s in range(nb):
        c_ref[s] = c_all[s * n:(s + 1) * n]

    @pl.when(c == pl.num_programs(1) - 1)
    def _():
        co_ref[...] = carry_scr[...]
        for s in range(nb):
            _pair_state_store(s_scr.at[s], so_ref.at[s], nhp)


def _rw_chunked(proj, shift8, params, s0):
    nhp = s0.shape[1] // 2
    tb = min(RW_CHUNK, proj.shape[1])
    cw = params[1].shape[-1]
    scratch = [lambda nb: pltpu.VMEM((nb, nhp, LANES, LANES), F32),
               lambda nb: pltpu.VMEM((nb, SUBLANES, proj.shape[-1]), F32)]
    return _chunk_call(functools.partial(_rw_chunk_kernel, nhp=nhp), proj, shift8, params, s0, cw, tb,
                       RW_SEQ_PER_STEP, scratch, "rw_chunk")


REC_VBLOCK = 8


def _rec_step_kernel(*refs, delta, aliased):
    if aliased:
        refs = refs[:-4] + refs[-3:]
    if delta:
        w_ref, k_ref, r_ref, v_ref, nkk_ref, kka_ref, s0_ref, o_ref, so_ref, ot_scr = refs
    else:
        w_ref, k_ref, r_ref, v_ref, s0_ref, o_ref, so_ref, ot_scr = refs
    wt, kt, rt, vt = w_ref[...].T, k_ref[...].T, r_ref[...].T, v_ref[...].T
    if delta:
        at, bt = nkk_ref[...].T, kka_ref[...].T
    for h in range(2):
        hs = slice(h * HEAD, (h + 1) * HEAD)
        w, k, r = wt[hs][None], kt[hs][None], rt[hs][None]
        for vb in range(0, HEAD, REC_VBLOCK):
            vs = slice(vb, vb + REC_VBLOCK)
            s = s0_ref[h, vs]
            val = vt[h * HEAD + vb:h * HEAD + vb + REC_VBLOCK][:, None, :]
            if delta:
                sa = jnp.sum(s * at[hs][None], axis=1)[:, None, :]
                s = s * w + sa * bt[hs][None] + val * k
            else:
                s = s * w + val * k
            so_ref[h, vs] = s
            ot_scr[h * HEAD + vb:h * HEAD + vb + REC_VBLOCK, :] = jnp.sum(s * r, axis=1)
    o_ref[...] = ot_scr[...].T


def _recurrence(rows_in, v, s_all, layer, s_out, divs):
    delta = len(rows_in) == 5
    aliased = s_out is not None
    grp, rows, c = v.shape
    nh, ns = s_all.shape[1], s_all.shape[-1]
    assert grp == 1 and rows == ns
    dw, dk, dr = divs

    def tile(div):
        return pl.BlockSpec((rows, LANES), lambda p: (0, p // div))

    sspec = pl.BlockSpec((None, 2) + s_all.shape[2:], lambda p: (layer, p, 0, 0, 0))
    operands = [rows_in[0][0], rows_in[1][0], rows_in[2][0], v[0]] + [a[0] for a in rows_in[3:]]
    in_specs = [tile(dw), tile(dk), tile(dr), tile(1)] + [tile(1)] * (len(rows_in) - 3)
    o, s_new = pl.pallas_call(
        functools.partial(_rec_step_kernel, delta=delta, aliased=aliased), grid=(nh // 2,),
        in_specs=in_specs + [sspec] + ([pl.BlockSpec(memory_space=pl.ANY)] if aliased else []),
        out_specs=[tile(1), sspec],
        out_shape=[jax.ShapeDtypeStruct((rows, c), F32), jax.ShapeDtypeStruct(s_all.shape, F32)],
        scratch_shapes=[pltpu.VMEM((LANES, rows), F32)],
        input_output_aliases={len(operands) + 1: 1} if aliased else {},
        compiler_params=_cparams(("arbitrary",)),
        name="rec_step",
    )(*operands, s_all, *([s_out] if aliased else []))
    return o[None], s_new


def _out_ffn_kernel(x_ref, a_ref, y_ref, c_ref, gate1_ref, wo_ref, g_ref, sc_ref, sh_ref, gate_ref, wg_ref, wu_ref,
                    wd_ref, gf_ref, o_ref, u_scr, *, tf, final_norm):
    mix_in = jnp.concatenate([a_ref[...], y_ref[...], c_ref[...]], axis=-1)
    x = x_ref[...] + gate1_ref[...] * jnp.dot(mix_in, wo_ref[...], preferred_element_type=F32)
    h = _rmsnorm_mod(x, g_ref[...], sc_ref[...], sh_ref[...]).astype(BF16)
    for f in range(wg_ref.shape[-1] // tf):
        fs = slice(f * tf, (f + 1) * tf)
        u = _silu(jnp.dot(h, wg_ref[:, fs], preferred_element_type=F32)) * jnp.dot(h, wu_ref[:, fs],
                                                                                   preferred_element_type=F32)
        u_scr[:, fs] = u.astype(BF16)
    y = x + gate_ref[...] * jnp.dot(u_scr[...], wd_ref[...], preferred_element_type=F32)
    if final_norm:
        y = y * lax.rsqrt(jnp.mean(y * y, axis=-1, keepdims=True) + RMS_EPS) * gf_ref[...]
    o_ref[...] = y


def _out_ffn(x, acts, mod, wo, g, wg, wu, wd, g_final, tm, tf, final_norm):
    grp, rows, d = x.shape
    nf = wg.shape[-1]
    return pl.pallas_call(
        functools.partial(_out_ffn_kernel, tf=tf, final_norm=final_norm),
        grid=(grp, rows // tm),
        in_specs=[_row_spec(tm, d)] + [_row_spec(tm, a.shape[-1]) for a in acts]
        + [_mod_spec(mod, tm, d, 2), _resident_spec(wo), _const_spec(g), _mod_spec(mod, tm, d, 4),
           _mod_spec(mod, tm, d, 3), _mod_spec(mod, tm, d, 5), _resident_spec(wg), _resident_spec(wu),
           _resident_spec(wd), _const_spec(g_final)],
        out_specs=_row_spec(tm, d),
        out_shape=jax.ShapeDtypeStruct(x.shape, F32),
        scratch_shapes=[pltpu.VMEM((tm, nf), BF16)],
        compiler_params=_cparams(("parallel", "parallel")),
        name="out_ffn",
    )(x, *acts, mod[0], wo, g, mod[0], mod[0], mod[0], wg, wu, wd, g_final)


def _pad_cols(a, width):
    return jnp.pad(a, [(0, 0)] * (a.ndim - 1) + [(0, width - a.shape[-1])])


def _pad_rows(a, height):
    return jnp.pad(a, [(0, 0)] * (a.ndim - 2) + [(0, height - a.shape[-2]), (0, 0)])


def _prep_layer(l, P, dims):
    hgw, n_x, n_bc, nh_mb, rww, lw, la, lg = dims
    w_in = P['w_in'][l]
    c0 = 4 * hgw
    c1 = c0 + n_x
    c2 = c1 + n_x + n_bc
    c3 = c2 + nh_mb
    w_hg = w_in[:, :c0].astype(BF16)
    w_mb = jnp.concatenate([w_in[:, c1:c2], w_in[:, c0:c1], _pad_cols(w_in[:, c2:c3], LANES)], axis=1).astype(BF16)
    w_rw_raw = w_in[:, c3:]

    def rw_cols(a):
        o = 3 * rww
        return jnp.concatenate([a[..., :o], _pad_cols(a[..., o:o + lw], LANES),
                                _pad_cols(a[..., o + lw:o + lw + la], LANES),
                                _pad_cols(a[..., o + lw + la:], LANES)], axis=-1)

    row = lambda a: a.reshape(1, -1).astype(F32)
    a_neg = -jnp.exp(P['mb_A_log'][l].astype(F32))
    return dict(
        w_hg=w_hg, w_mb=w_mb, w_rw=rw_cols(w_rw_raw).astype(BF16), rw_cols=rw_cols,
        g1=row(P['g_norm1'][l]), g2=row(P['g_norm2'][l]),
        rw_params=(rw_cols(row(P['rw_mu'][l])), row(P['rw_w0'][l]), row(P['rw_a0'][l]), row(P['rw_k_k'][l]),
                   row(P['rw_k_a'][l]), row(P['rw_r_k'][l]),
                   _pad_rows(P['rw_w2'][l], LANES).astype(BF16), _pad_rows(P['rw_a2'][l], LANES).astype(BF16),
                   _pad_rows(P['rw_g2'][l], LANES).astype(BF16)),
        mb_params=(P['mb_conv_w'][l].astype(F32), row(P['mb_conv_b'][l]), _pad_cols(row(P['mb_dt_bias'][l]), LANES),
                   row(jnp.repeat(a_neg, HEAD)), _pad_cols(row(a_neg), LANES)),
        out_params=(row(P['hg_norm_g'][l]), row(jnp.repeat(P['mb_D'][l], HEAD)), row(P['mb_norm_g'][l]),
                    row(P['rw_ln_g'][l]), row(P['rw_ln_b'][l])),
        wo=P['w_out'][l].astype(BF16), wg=P['w_gate'][l].astype(BF16), wu=P['w_up'][l].astype(BF16),
        wd=P['w_down'][l].astype(BF16),
    )


def _trunk(x, mods, st_hg, st_ssm, st_conv, st_wkv, st_shift, lb_all, layers, g_final, dims, seq, tm, tf):
    hgw, n_x, n_bc, nh_mb, rww, lw, la, lg = dims
    grp, rows, d = x.shape
    depth = len(layers)
    n_grp = n_bc // (2 * HEAD)
    outs = ([], [], [], [], [])
    if seq:
        st_hg = jnp.swapaxes(st_hg, -1, -2)
    else:
        st_hg = jnp.transpose(st_hg, (0, 2, 4, 3, 1))
        st_ssm = jnp.transpose(st_ssm, (0, 2, 3, 4, 1))
        st_wkv = jnp.transpose(st_wkv, (0, 2, 3, 4, 1))
    s_hg = s_ssm = s_wkv = None
    for l in range(depth):
        L = layers[l]
        mod = (mods, l)
        lb = lb_all[l].reshape(1, -1)
        hg_g, d_exp, mb_g, ln_g, ln_b = L['out_params']
        cw_rw = L['w_rw'].shape[-1]
        if seq:
            p_hg, p_mb, rw_p = _in_proj(
                x, L['g1'], mod, [L['w_hg'], L['w_mb'], L['w_rw']], [(_split_epilogue, 1)] * 3, [],
                [4 * hgw, 2 * n_x + n_bc + LANES, cw_rw], tm, "in_proj")
            shift8 = jnp.pad(L['rw_cols'](st_shift[l])[:, None, :], ((0, 0), (SUBLANES - 1, 0), (0, 0)))
            conv8 = jnp.pad(st_conv[l], ((0, 0), (SUBLANES - st_conv[l].shape[1], 0), (0, 0)))
            a_hg, s_hg = _hg_chunked(p_hg, (lb, jnp.log(lb), jnp.log1p(-lb), hg_g), st_hg[l])
            y_mb, s_ssm, conv_tail = _ssd_chunked(p_mb, conv8, L['mb_params'] + (d_exp, mb_g), st_ssm[l], n_x, n_grp)
            c_rw, s_wkv, shift_tail = _rw_chunked(rw_p, shift8, L['rw_params'] + (ln_g, ln_b), st_wkv[l])
            conv_new = conv_tail[:, SUBLANES - st_conv[l].shape[1]:, :]
            shift_new = shift_tail[:, SUBLANES - 1, :]
        else:
            q, w_hg, k_hg, i_hg, g_hg, xbc, z, dtr, rw_p = _in_proj(
                x, L['g1'], mod, [L['w_hg'], L['w_mb'], L['w_rw']],
                [(_hg_epilogue, 5), (_split_epilogue, 3), (_split_epilogue, 1)], [lb],
                [hgw] * 5 + [n_x + n_bc, n_x, LANES, cw_rw], tm, "in_proj_step")
            conv_new = jnp.concatenate([st_conv[l][:, 1:, :], xbc[0][:, None, :]], axis=1)
            shift_new = rw_p[0]
            r_rw, w_rw, k_rw, v_rw, nkk, kka, g_rw, bv = _rw_pre_step(
                rw_p, L['rw_cols'](st_shift[l])[None], L['rw_params'], tm)
            w_mb, v_mb, k_mb, r_mb, mx, zs = _mb_pre_step(
                xbc, z, dtr, tuple(st_conv[l][None, :, 2 - j, :] for j in range(3)), L['mb_params'], tm)
            o_hg, s_hg = _recurrence((w_hg, k_hg, q), i_hg, st_hg, l, s_hg, (1, 1, 1))
            o_mb, s_ssm = _recurrence((w_mb, k_mb, r_mb), v_mb, st_ssm, l, s_ssm, (1, 2, 2))
            o_rw, s_wkv = _recurrence((w_rw, k_rw, r_rw, nkk, kka), v_rw, st_wkv, l, s_wkv, (1, 1, 1))
            a_hg, y_mb, c_rw = _mix_finish((o_hg, g_hg, o_mb, mx, zs, o_rw, g_rw, bv), L['out_params'], tm, n_grp)
        x = _out_ffn(x, (a_hg, y_mb, c_rw), mod, L['wo'], L['g2'], L['wg'], L['wu'], L['wd'], g_final, tm, tf,
                     final_norm=(l == depth - 1))
        o3 = 3 * rww
        shift_new = jnp.concatenate([shift_new[:, :o3], shift_new[:, o3:o3 + lw],
                                     shift_new[:, o3 + LANES:o3 + LANES + la],
                                     shift_new[:, o3 + 2 * LANES:o3 + 2 * LANES + lg]], axis=-1)
        for lst, s in zip(outs, (s_hg, s_ssm, conv_new, s_wkv, shift_new)):
            lst.append(s)
    if seq:
        hg_new, ssm_new, conv_new, wkv_new, shift_new = [jnp.stack(lst) for lst in outs]
        return x, [jnp.swapaxes(hg_new, -1, -2), ssm_new, conv_new, wkv_new, shift_new]
    return x, [jnp.transpose(s_hg, (0, 4, 1, 3, 2)), jnp.transpose(s_ssm, (0, 4, 1, 2, 3)), jnp.stack(outs[2]),
               jnp.transpose(s_wkv, (0, 4, 1, 2, 3)), jnp.stack(outs[4])]


def kernel(x_prompt, x_sample, state_hgrn, state_ssm, state_conv, state_wkv, state_shift, c_prompt, c_sample,
           w_ada, b_ada, g_norm1, w_in, hg_lb_logits, hg_norm_g, mb_conv_w, mb_conv_b, mb_dt_bias, mb_A_log,
           mb_D, mb_norm_g, rw_mu, rw_w0, rw_w2, rw_a0, rw_a2, rw_g2, rw_k_k, rw_k_a, rw_r_k, rw_ln_g, rw_ln_b,
           w_out, g_norm2, w_gate, w_up, w_down, g_final):
    P = dict(g_norm1=g_norm1, w_in=w_in, hg_norm_g=hg_norm_g, mb_conv_w=mb_conv_w, mb_conv_b=mb_conv_b,
             mb_dt_bias=mb_dt_bias, mb_A_log=mb_A_log, mb_D=mb_D, mb_norm_g=mb_norm_g, rw_mu=rw_mu, rw_w0=rw_w0,
             rw_w2=rw_w2, rw_a0=rw_a0, rw_a2=rw_a2, rw_g2=rw_g2, rw_k_k=rw_k_k, rw_k_a=rw_k_a,
             rw_r_k=rw_r_k.reshape(rw_r_k.shape[0], -1), rw_ln_g=rw_ln_g, rw_ln_b=rw_ln_b, w_out=w_out,
             g_norm2=g_norm2, w_gate=w_gate, w_up=w_up, w_down=w_down)
    depth = w_in.shape[0]
    bsz, t_len, d = x_prompt.shape
    dec = x_sample.shape[0]
    hgw = hg_lb_logits.shape[-1]
    n_x = mb_norm_g.shape[-1]
    n_bc = mb_conv_w.shape[-1] - n_x
    nh_mb = mb_dt_bias.shape[-1]
    rww = rw_w0.shape[-1]
    lw, la, lg = rw_w2.shape[1], rw_a2.shape[1], rw_g2.shape[1]
    dims = (hgw, n_x, n_bc, nh_mb, rww, lw, la, lg)
    layers = [_prep_layer(l, P, dims) for l in range(depth)]
    gf = g_final.reshape(1, d)

    cs = jnp.cumsum(jax.nn.softmax(hg_lb_logits.astype(F32), axis=0), axis=0)
    lb_all = cs - cs[:1]

    n_c = bsz + dec
    pad_c = -n_c % (2 * SUBLANES)
    mods_p, mods_s = _ada(jnp.pad(jnp.concatenate([c_prompt, c_sample], axis=0), ((0, pad_c), (0, 0))), bsz, dec,
                          w_ada, b_ada)
    mods_p = mods_p[:, :, None, :]
    mods_s = mods_s[:, None, :, :]

    dt = x_prompt.dtype
    zeros = lambda a: jnp.zeros((depth, bsz) + a.shape[2:], dt)
    y_p, st_p = _trunk(x_prompt, mods_p, zeros(state_hgrn), zeros(state_ssm), zeros(state_conv), zeros(state_wkv),
                       zeros(state_shift), lb_all, layers, gf, dims, True, min(512, t_len), 256)
    y_s, st_s = _trunk(x_sample.reshape(1, dec, d), mods_s, state_hgrn, state_ssm, state_conv, state_wkv,
                       state_shift, lb_all, layers, gf, dims, False, dec, 256)
    hg_p, ssm_p, conv_p, wkv_p, shift_p = st_p
    hg_s, ssm_s, conv_s, wkv_s, shift_s = st_s
    return (y_p, y_s.reshape(x_sample.shape), hg_p, hg_s, ssm_p, ssm_s, conv_p, conv_s, wkv_p, wkv_s,
            shift_p, shift_s)
```

```python
import functools

import jax
import jax.numpy as jnp
from jax import lax
from jax.experimental import pallas as pl
from jax.experimental.pallas import tpu as pltpu

F32 = jnp.float32
BF16 = jnp.bfloat16

LANES = 128
SUBLANES = 8
HEAD = 64
RMS_EPS = 1e-6
RW_GN_EPS = 64e-5
VMEM_LIMIT = 48 * 1024 * 1024
SSD_CHUNK = 128
HG_BLOCK = 128
HG_SUB = 16
RW_CHUNK = 64
RW_INV_BLOCK = 16
SEQ_PER_STEP = 8
SSD_SEQ_PER_STEP = 1
RW_SEQ_PER_STEP = 8

_NN = (((1,), (0,)), ((), ()))
_NT = (((1,), (1,)), ((), ()))
_TN = (((0,), (0,)), ((), ()))


def _cparams(sem):
    return pltpu.CompilerParams(dimension_semantics=sem, vmem_limit_bytes=VMEM_LIMIT)


def _sigmoid(x):
    return 1.0 / (1.0 + jnp.exp(-x))


def _silu(x):
    return x * _sigmoid(x)


def _softplus(x):
    return jnp.maximum(x, 0.0) + jnp.log1p(jnp.exp(-jnp.abs(x)))


def _split3(x):
    hi = x.astype(BF16)
    r1 = x - hi.astype(F32)
    mid = r1.astype(BF16)
    return hi, mid, (r1 - mid.astype(F32)).astype(BF16)


def _seg_sum(x, seg=HEAD):
    n = x.shape[-1]
    i = lax.broadcasted_iota(jnp.int32, (3 * n, n), 0) % n // seg
    j = lax.broadcasted_iota(jnp.int32, (3 * n, n), 1) // seg
    return _dot01(x, i == j)


def _dot01(x, mask3):
    return jnp.dot(jnp.concatenate(_split3(x), axis=-1), jnp.where(mask3, 1.0, 0.0).astype(BF16),
                   preferred_element_type=F32)


def _rmsnorm_mod(x, g, sc, sh):
    h = x * lax.rsqrt(jnp.mean(x * x, axis=-1, keepdims=True) + RMS_EPS) * g
    return h * (1.0 + sc) + sh


def _ada_kernel(c_ref, w_ref, b_ref, op_ref, os_ref):
    a = _silu(c_ref[...]).astype(BF16)
    y = jnp.dot(a, w_ref[...].astype(BF16), preferred_element_type=F32) + b_ref[...]
    n_p = op_ref.shape[0]
    op_ref[...] = y[0:n_p]
    os_ref[...] = y[n_p:n_p + os_ref.shape[0]]


def _ada(c_all, n_p, n_s, w_ada, b_ada):
    depth, d, n = w_ada.shape
    rows = c_all.shape[0]
    tn = n // 4
    return pl.pallas_call(
        _ada_kernel,
        grid=(depth, n // tn),
        in_specs=[pl.BlockSpec((rows, d), lambda l, j: (0, 0)),
                  pl.BlockSpec((None, d, tn), lambda l, j: (l, 0, j)),
                  pl.BlockSpec((None, 1, tn), lambda l, j: (l, 0, j))],
        out_specs=[pl.BlockSpec((None, n_p, tn), lambda l, j: (l, 0, j)),
                   pl.BlockSpec((None, n_s, tn), lambda l, j: (l, 0, j))],
        out_shape=[jax.ShapeDtypeStruct((depth, n_p, n), F32), jax.ShapeDtypeStruct((depth, n_s, n), F32)],
        compiler_params=_cparams(("parallel", "parallel")),
        name="ada_mod",
    )(c_all, w_ada, b_ada.reshape(depth, 1, n))


def _row_spec(tm, width, col=0):
    return pl.BlockSpec((None, tm, width), lambda g, i: (g, i, col))


def _mod_spec(mod, tm, d, idx):
    arr, l = mod
    if arr.shape[2] == 1:
        return pl.BlockSpec((None, None, 1, d), lambda g, i: (l, g, 0, idx))
    return pl.BlockSpec((None, None, tm, d), lambda g, i: (l, g, i, idx))


def _const_spec(a):
    nd = a.ndim
    return pl.BlockSpec(a.shape, lambda g, i: (0,) * nd)


def _resident_spec(a):
    nd = a.ndim
    return pl.BlockSpec(a.shape, lambda g, i: (0,) * nd, pipeline_mode=pl.Buffered(1))


def _in_proj_kernel(x_ref, g_ref, sc_ref, sh_ref, *rest, n_w, epilogues, n_extra):
    ws, rest = rest[:n_w], rest[n_w:]
    extra, outs = rest[:n_extra], rest[n_extra:]
    h = _rmsnorm_mod(x_ref[...], g_ref[...], sc_ref[...], sh_ref[...]).astype(BF16)
    for w_ref, (epilogue, n_out) in zip(ws, epilogues):
        epilogue(jnp.dot(h, w_ref[...], preferred_element_type=F32), extra, outs[:n_out])
        outs = outs[n_out:]


def _split_epilogue(y, extra, outs):
    col = 0
    for o in outs:
        wd = o.shape[-1]
        o[...] = y[:, col:col + wd]
        col += wd


def _hg_gates(zf, lb, log_lb, log_1mlb, log_space):
    k = (1.0 - lb) * _sigmoid(-zf)
    if not log_space:
        return lb + (1.0 - lb) * _sigmoid(zf), k
    b = log_1mlb - _softplus(-zf)
    return jnp.maximum(log_lb, b) + jnp.log1p(jnp.exp(-jnp.abs(log_lb - b))), k


def _hg_epilogue(y, extra, outs):
    lb = extra[0][...]
    q_ref, w_ref, k_ref, i_ref, g_ref = outs
    c = lb.shape[-1]
    q_ref[...] = y[:, 0:c]
    w_ref[...], k_ref[...] = _hg_gates(y[:, c:2 * c], lb, None, None, False)
    i_ref[...] = y[:, 2 * c:3 * c]
    g_ref[...] = y[:, 3 * c:4 * c]


def _in_proj(x, g, mod, ws, epilogues, extra, out_widths, tm, name):
    grp, rows, d = x.shape
    kern = functools.partial(_in_proj_kernel, n_w=len(ws), epilogues=epilogues, n_extra=len(extra))
    return pl.pallas_call(
        kern,
        grid=(grp, rows // tm),
        in_specs=[_row_spec(tm, d), _const_spec(g), _mod_spec(mod, tm, d, 1), _mod_spec(mod, tm, d, 0)]
        + [_resident_spec(w) for w in ws] + [_const_spec(e) for e in extra],
        out_specs=[_row_spec(tm, wd) for wd in out_widths],
        out_shape=[jax.ShapeDtypeStruct((grp, rows, wd), F32) for wd in out_widths],
        compiler_params=_cparams(("parallel", "parallel")),
        name=name,
    )(x, g, mod[0], mod[0], *ws, *extra)


def _shift_rows(cur, prev8, j):
    rolled = pltpu.roll(cur, j, 0)
    row8 = lax.broadcasted_iota(jnp.int32, (SUBLANES, 1), 0)
    head = jnp.where(row8 < j, pltpu.roll(prev8, j, 0), rolled[0:SUBLANES])
    if cur.shape[0] == SUBLANES:
        return head
    return jnp.concatenate([head, rolled[SUBLANES:]], axis=0)


def _rw_math(p, prev, mu, w0, a0, kk_s, ka_s, rk, w2, a2, g2, log_space):
    c = w0.shape[-1]
    lo = w2.shape[0]
    mixd = p + (prev - p) * mu
    r = mixd[:, 0:c]
    k0 = mixd[:, c:2 * c]
    v = mixd[:, 2 * c:3 * c]
    wl = mixd[:, 3 * c:3 * c + lo]
    al = mixd[:, 3 * c + lo:3 * c + 2 * lo]
    gl = mixd[:, 3 * c + 2 * lo:3 * c + 3 * lo]
    z = -(w0 + jnp.dot(jnp.tanh(wl).astype(BF16), w2, preferred_element_type=F32))
    w = -jnp.exp(-_softplus(z) - 0.5)
    if not log_space:
        w = jnp.exp(w)
    a = _sigmoid(a0 + jnp.dot(al.astype(BF16), a2, preferred_element_type=F32))
    g = jnp.dot(_sigmoid(gl).astype(BF16), g2, preferred_element_type=F32)
    kk = k0 * kk_s
    kk = kk * lax.rsqrt(jnp.maximum(_seg_sum(kk * kk), 1e-24))
    k = k0 * (1.0 + (a - 1.0) * ka_s)
    bv = _seg_sum(r * k * rk) * v
    return r, w, k, v, -kk, kk * a, g, bv


def _rw_pre_step_kernel(p_ref, prev_ref, *rest):
    params, outs = rest[:9], rest[9:]
    res = _rw_math(p_ref[...], prev_ref[...], *[a[...] for a in params], False)
    for o, val in zip(outs, res):
        o[...] = val


def _rw_pre_step(p, prev, params, tm):
    grp, rows, cp = p.shape
    c = params[1].shape[-1]
    return pl.pallas_call(
        _rw_pre_step_kernel, grid=(grp, rows // tm),
        in_specs=[_row_spec(tm, cp), _row_spec(tm, cp)] + [_const_spec(a) for a in params],
        out_specs=[_row_spec(tm, c)] * 8, out_shape=[jax.ShapeDtypeStruct((grp, rows, c), F32)] * 8,
        compiler_params=_cparams(("parallel", "parallel")),
        name="rw_pre_step")(p, prev, *params)


def _mb_math(cur, s1, s2, s3, z, dtr, cw, cb, dtb, a_exp, a_pad, chunked):
    n_x = z.shape[-1]
    conv = s3 * cw[0:1] + s2 * cw[1:2] + s1 * cw[2:3] + cur * cw[3:4] + cb
    act = _silu(conv)
    mx = act[:, 0:n_x]
    half = (act.shape[-1] - n_x) // 2
    mb = act[:, n_x:n_x + half]
    mc = act[:, n_x + half:]
    dt = _softplus(dtr + dtb)
    nd = dt.shape[-1]
    i = lax.broadcasted_iota(jnp.int32, (3 * nd, n_x), 0) % nd
    j = lax.broadcasted_iota(jnp.int32, (3 * nd, n_x), 1)
    dte = _dot01(dt, j // HEAD == i)
    if chunked:
        return dt * a_pad, mx * dte, act[:, n_x:], mx, _silu(z)
    i = lax.broadcasted_iota(jnp.int32, (3 * half, 2 * half), 0) % half
    j = lax.broadcasted_iota(jnp.int32, (3 * half, 2 * half), 1)
    rep = i == (j // LANES) * HEAD + j % HEAD
    return jnp.exp(dte * a_exp), mx * dte, _dot01(mb, rep), _dot01(mc, rep), mx, _silu(z)


def _mb_pre_step_kernel(xbc_ref, s1_ref, s2_ref, s3_ref, z_ref, dt_ref, cw, cb, dtb, a_exp, a_pad, *outs):
    res = _mb_math(xbc_ref[...], s1_ref[...], s2_ref[...], s3_ref[...], z_ref[...], dt_ref[...],
                   cw[...], cb[...], dtb[...], a_exp[...], a_pad[...], False)
    for o, val in zip(outs, res):
        o[...] = val


def _mb_pre_step(xbc, z, dtr, conv_rows, params, tm):
    grp, rows, cx = xbc.shape
    n_x = z.shape[-1]
    n_k = cx - n_x
    widths = [n_x, n_x, n_k, n_k, n_x, n_x]
    return pl.pallas_call(
        _mb_pre_step_kernel, grid=(grp, rows // tm),
        in_specs=[_row_spec(tm, cx)] * 4 + [_row_spec(tm, n_x), _row_spec(tm, dtr.shape[-1])]
        + [_const_spec(a) for a in params],
        out_specs=[_row_spec(tm, wd) for wd in widths],
        out_shape=[jax.ShapeDtypeStruct((grp, rows, wd), F32) for wd in widths],
        compiler_params=_cparams(("parallel", "parallel")),
        name="mb_pre_step")(xbc, *conv_rows, z, dtr, *params)


def _hg_finish(o, g, hg_g):
    return o * lax.rsqrt(_seg_sum(o * o) * (1.0 / HEAD) + RMS_EPS) * hg_g * _silu(g)


def _mb_finish(o, mx, zs, d_exp, mb_g, n_grp):
    y = (o + d_exp * mx) * zs
    gw = y.shape[-1] // n_grp
    return jnp.concatenate(
        [y[:, g * gw:(g + 1) * gw]
         * lax.rsqrt(jnp.mean(y[:, g * gw:(g + 1) * gw] ** 2, axis=-1, keepdims=True) + RMS_EPS)
         for g in range(n_grp)], axis=-1) * mb_g


def _rw_finish(o, g, bv, ln_g, ln_b):
    mu = _seg_sum(o) * (1.0 / HEAD)
    dlt = o - mu
    var = _seg_sum(dlt * dlt) * (1.0 / HEAD)
    return (dlt * lax.rsqrt(var + RW_GN_EPS) * ln_g + ln_b + bv) * g


def _mix_finish_kernel(ohg_ref, ghg_ref, omb_ref, mx_ref, zs_ref, orw_ref, grw_ref, bv_ref,
                       hg_g, d_exp, mb_g, ln_g, ln_b, a_ref, y_ref, c_ref, *, n_grp):
    a_ref[...] = _hg_finish(ohg_ref[...], ghg_ref[...], hg_g[...]).astype(BF16)
    y_ref[...] = _mb_finish(omb_ref[...], mx_ref[...], zs_ref[...], d_exp[...], mb_g[...], n_grp).astype(BF16)
    c_ref[...] = _rw_finish(orw_ref[...], grw_ref[...], bv_ref[...], ln_g[...], ln_b[...]).astype(BF16)


def _mix_finish(acts, params, tm, n_grp):
    grp, rows, _ = acts[0].shape
    widths = [acts[0].shape[-1], acts[2].shape[-1], acts[5].shape[-1]]
    return pl.pallas_call(
        functools.partial(_mix_finish_kernel, n_grp=n_grp),
        grid=(grp, rows // tm),
        in_specs=[_row_spec(tm, a.shape[-1]) for a in acts] + [_const_spec(a) for a in params],
        out_specs=[_row_spec(tm, wd) for wd in widths],
        out_shape=[jax.ShapeDtypeStruct((grp, rows, wd), BF16) for wd in widths],
        compiler_params=_cparams(("parallel", "parallel")),
        name="mix_finish",
    )(*acts, *params)


def _block_diag_mask():
    i = lax.broadcasted_iota(jnp.int32, (LANES, LANES), 0) // HEAD
    j = lax.broadcasted_iota(jnp.int32, (LANES, LANES), 1) // HEAD
    return i == j


def _pair_state_load(s0_ref, s_scr, nhp):
    for p in range(nhp):
        s_scr[p] = jnp.zeros((LANES, LANES), F32)
        s_scr[p, 0:HEAD, 0:HEAD] = s0_ref[2 * p]
        s_scr[p, HEAD:LANES, HEAD:LANES] = s0_ref[2 * p + 1]


def _pair_state_store(s_scr, so_ref, nhp):
    for p in range(nhp):
        so_ref[2 * p] = s_scr[p, 0:HEAD, 0:HEAD]
        so_ref[2 * p + 1] = s_scr[p, HEAD:LANES, HEAD:LANES]


def _chunk_call(kern, proj, carry0, params, s0, out_width, tb, nb, scratch, name):
    grp, rows, c = proj.shape
    nb = min(nb, grp)
    sspec = pl.BlockSpec((nb,) + s0.shape[1:], lambda g, i: (g, 0, 0, 0))
    in_specs = [pl.BlockSpec((nb, tb, c), lambda g, i: (g, i, 0))]
    args = [proj]
    out_specs = [pl.BlockSpec((nb, tb, out_width), lambda g, i: (g, i, 0)), sspec]
    out_shape = [jax.ShapeDtypeStruct((grp, rows, out_width), BF16), jax.ShapeDtypeStruct(s0.shape, F32)]
    if carry0 is not None:
        cspec = pl.BlockSpec((nb,) + carry0.shape[1:], lambda g, i: (g, 0, 0))
        in_specs.append(cspec)
        args.append(carry0)
        out_specs.append(cspec)
        out_shape.append(jax.ShapeDtypeStruct(carry0.shape, F32))
    return pl.pallas_call(
        functools.partial(kern, nb=nb), grid=(grp // nb, rows // tb),
        in_specs=in_specs + [_const_spec(a) for a in params] + [sspec],
        out_specs=out_specs,
        out_shape=out_shape,
        scratch_shapes=[s(nb) for s in scratch],
        compiler_params=_cparams(("parallel", "arbitrary")),
        name=name,
    )(*args, *params, s0)


def _ssd_chunk_kernel(p_ref, c8_ref, cw, cb, dtb, a_exp, a_pad, d_exp, mb_g, s0_ref, y_ref, so_ref, co_ref,
                      h_scr, carry_scr, y_scr, *, nb, nh, ng, n_x):
    c = pl.program_id(1)

    @pl.when(c == 0)
    def _():
        h_scr[...] = s0_ref[...]
        carry_scr[...] = c8_ref[...]

    n = p_ref.shape[1]
    n_bc = 2 * ng * HEAD
    rg = nh // ng
    row = lax.broadcasted_iota(jnp.int32, (n, n), 0)
    col = lax.broadcasted_iota(jnp.int32, (n, n), 1)
    tril = row >= col
    for s in range(nb):
        work = []
        cur = p_ref[s, :, 0:n_x + n_bc]
        prev8 = carry_scr[s]
        dta, xdt, bc, mx, zs = _mb_math(
            cur, _shift_rows(cur, prev8, 1), _shift_rows(cur, prev8, 2), _shift_rows(cur, prev8, 3),
            p_ref[s, :, n_x + n_bc:2 * n_x + n_bc], p_ref[s, :, 2 * n_x + n_bc:],
            cw[...], cb[...], dtb[...], a_exp[...], a_pad[...], True)
        carry_scr[s] = cur[n - SUBLANES:]
        acum = _cumsum_rows(dta)
        acum_t = acum.T
        bgs = [bc[:, g * HEAD:(g + 1) * HEAD].astype(BF16) for g in range(ng)]
        cgs = [bc[:, (ng + g) * HEAD:(ng + g + 1) * HEAD].astype(BF16) for g in range(ng)]
        cbs = [lax.dot_general(cg, bg, _NT, preferred_element_type=F32) for cg, bg in zip(cgs, bgs)]
        for h in range(nh):
            g = h // rg
            a_col = acum[:, h:h + 1]
            a_tot = acum[n - 1:n, h:h + 1]
            lmat = jnp.exp(jnp.where(tril, a_col - acum_t[h:h + 1, :], -jnp.inf))
            xh = xdt[:, h * HEAD:(h + 1) * HEAD]
            work.append(dict(h=h, a_col=a_col, a_tot=a_tot, sh=h_scr[s, h], xh=xh.astype(BF16), bg=bgs[g],
                             cg=cgs[g], gm=(cbs[g] * lmat).astype(BF16),
                             xd=(xh * jnp.exp(a_tot - a_col)).astype(BF16)))
        for q in work:
            q['intra'] = jnp.dot(q['gm'], q['xh'], preferred_element_type=F32)
            q['inter'] = lax.dot_general(q['cg'], q['sh'].astype(BF16), _NT, preferred_element_type=F32)
            q['upd'] = lax.dot_general(q['xd'], q['bg'], _TN, preferred_element_type=F32)
        for q in work:
            h = q['h']
            y_scr[s, :, h * HEAD:(h + 1) * HEAD] = q['intra'] + jnp.exp(q['a_col']) * q['inter']
            h_scr[s, h] = jnp.exp(q['a_tot']) * q['sh'] + q['upd']
        y_ref[s] = _mb_finish(y_scr[s], mx, zs, d_exp[...], mb_g[...], ng).astype(BF16)

    @pl.when(c == pl.num_programs(1) - 1)
    def _():
        so_ref[...] = h_scr[...]
        co_ref[...] = carry_scr[...]


def _ssd_chunked(proj, conv8, params, s0, n_x, ng):
    nh = s0.shape[1]
    tb = min(SSD_CHUNK, proj.shape[1])
    scratch = [lambda nb: pltpu.VMEM((nb,) + s0.shape[1:], F32),
               lambda nb: pltpu.VMEM((nb, SUBLANES, conv8.shape[-1]), F32),
               lambda nb: pltpu.VMEM((nb, tb, n_x), F32)]
    return _chunk_call(functools.partial(_ssd_chunk_kernel, nh=nh, ng=ng, n_x=n_x), proj, conv8, params, s0, n_x,
                       tb, SSD_SEQ_PER_STEP, scratch, "ssd_chunk")


def _hg_chunk_kernel(p_ref, lb_ref, llb_ref, l1m_ref, hg_g, s0_ref, a_ref, so_ref, s_scr, o_scr,
                     *, nb, nhp, sub):
    c = pl.program_id(1)

    @pl.when(c == 0)
    def _():
        for s in range(nb):
            _pair_state_load(s0_ref.at[s], s_scr.at[s], nhp)

    tb = p_ref.shape[1]
    cw = lb_ref.shape[-1]
    pos = lax.broadcasted_iota(jnp.int32, (tb, 1), 0) % sub
    bd = _block_diag_mask()
    seg = jnp.where(bd, 1.0, 0.0).astype(BF16)
    causal = (lax.broadcasted_iota(jnp.int32, (sub, sub, 1), 1) <= lax.broadcasted_iota(jnp.int32, (sub, sub, 1), 0))
    bs, ks = [], []
    for s in range(nb):
        b, k = _hg_gates(p_ref[s, :, cw:2 * cw], lb_ref[...], llb_ref[...], l1m_ref[...], True)
        step = 1
        while step < sub:
            b = b + jnp.where(pos >= step, pltpu.roll(b, step, 0), 0.0)
            step *= 2
        bs.append(b)
        ks.append(k)
    probs = [(s, p) for s in range(nb) for p in range(nhp)]
    states = {sp: s_scr[sp[0], sp[1]] for sp in probs}
    for ci in range(tb // sub):
        rs = slice(ci * sub, (ci + 1) * sub)
        work = []
        for s, p in probs:
            ls = slice(p * LANES, (p + 1) * LANES)
            bc = bs[s][rs, ls]
            kc = ks[s][rs, ls]
            qc = p_ref[s, rs, p * LANES:(p + 1) * LANES]
            vc = p_ref[s, rs, 2 * cw + p * LANES:2 * cw + (p + 1) * LANES]
            btot = bc[sub - 1:sub]
            o = lax.dot_general((qc * jnp.exp(bc)).astype(BF16), states[s, p].astype(BF16), _NT,
                                preferred_element_type=F32)
            e = jnp.exp(jnp.where(causal, bc[:, None, :] - bc[None, :, :], -jnp.inf))
            x = (e * kc[None, :, :] * qc[:, None, :]).astype(BF16).reshape(sub * sub, LANES)
            att = jnp.dot(x, seg, preferred_element_type=F32).reshape(sub, sub, LANES)
            kd = (kc * jnp.exp(btot - bc)).astype(BF16)
            upd = lax.dot_general(vc.astype(BF16), kd, _TN, preferred_element_type=F32)
            work.append((s, p, ls, o, att, vc, jnp.exp(btot), upd))
        for s, p, ls, o, att, vc, dec, upd in work:
            o_scr[s, rs, ls] = o + jnp.sum(att * vc[None, :, :], axis=1)
            states[s, p] = states[s, p] * dec + jnp.where(bd, upd, 0.0)
    for s, p in probs:
        s_scr[s, p] = states[s, p]
    for s in range(nb):
        a_ref[s] = _hg_finish(o_scr[s], p_ref[s, :, 3 * cw:4 * cw], hg_g[...]).astype(BF16)

    @pl.when(c == pl.num_programs(1) - 1)
    def _():
        for s in range(nb):
            _pair_state_store(s_scr.at[s], so_ref.at[s], nhp)


def _hg_chunked(proj, params, s0):
    nhp = s0.shape[1] // 2
    tb = min(HG_BLOCK, proj.shape[1])
    cw = proj.shape[-1] // 4
    scratch = [lambda nb: pltpu.VMEM((nb, nhp, LANES, LANES), F32), lambda nb: pltpu.VMEM((nb, tb, cw), F32)]
    return _chunk_call(functools.partial(_hg_chunk_kernel, nhp=nhp, sub=HG_SUB), proj, None, params, s0, cw, tb,
                       SEQ_PER_STEP, scratch, "hg_chunk")


def _split2(x):
    hi = x.astype(BF16)
    return hi, (x - hi.astype(F32)).astype(BF16)


def _dot3(a, b, dims=_NN):
    ah, am = _split2(a)
    bh, bm = _split2(b)
    (ca,), (cb,) = dims[0]
    return lax.dot_general(jnp.concatenate([ah, ah, am], axis=ca), jnp.concatenate([bh, bm, bh], axis=cb), dims,
                           preferred_element_type=F32)


def _cumsum_rows(x):
    n = x.shape[0]
    tril3 = (lax.broadcasted_iota(jnp.int32, (n, 3 * n), 0) >= lax.broadcasted_iota(jnp.int32, (n, 3 * n), 1) % n)
    return jnp.dot(jnp.where(tril3, 1.0, 0.0).astype(BF16), jnp.concatenate(_split3(x), axis=0),
                   preferred_element_type=F32)


def _tri_inverse(n_mats, blk, top):
    n = n_mats[0].shape[0]
    ri = lax.broadcasted_iota(jnp.int32, (n, n), 0)
    ci = lax.broadcasted_iota(jnp.int32, (n, n), 1)
    eye = jnp.where(ri == ci, 1.0, 0.0)
    pws = [jnp.where(ri // blk == ci // blk, m, 0.0) for m in n_mats]
    ts = [eye + pw for pw in pws]
    s = 2
    while s < blk:
        pws = [_dot3(pw, pw) for pw in pws]
        ts = [t + _dot3(t, pw) for t, pw in zip(ts, pws)]
        s *= 2
    size = blk
    while size < top:
        in_2blk = ri // (2 * size) == ci // (2 * size)
        in_blk = ri // size == ci // size
        offs = [jnp.where(in_2blk, m, 0.0) - jnp.where(in_blk, m, 0.0) for m in n_mats]
        mids = [_dot3(t, off) for t, off in zip(ts, offs)]
        ts = [t + _dot3(mid, t) for t, mid in zip(ts, mids)]
        size *= 2
    return ts


def _rw_chunk_kernel(p_ref, c8_ref, mu, w0, a0, kk_s, ka_s, rk, w2, a2, g2, ln_g, ln_b, s0_ref, c_ref, so_ref,
                     co_ref, s_scr, carry_scr, *, nb, nhp):
    c = pl.program_id(1)

    @pl.when(c == 0)
    def _():
        carry_scr[...] = c8_ref[...]
        for s in range(nb):
            _pair_state_load(s0_ref.at[s], s_scr.at[s], nhp)

    n = p_ref.shape[1]
    n2 = 2 * n
    ri = lax.broadcasted_iota(jnp.int32, (n2, n2), 0)
    ci = lax.broadcasted_iota(jnp.int32, (n2, n2), 1)
    tril = ri >= ci
    strict = ri > ci
    m0 = lax.broadcasted_iota(jnp.int32, (1, LANES), 1) < HEAD
    bd = _block_diag_mask()
    params = [a[...] for a in (mu, w0, a0, kk_s, ka_s, rk, w2, a2, g2)]

    def stack(x):
        return jnp.concatenate([jnp.where(m0, x, 0.0), jnp.where(m0, 0.0, x)], axis=0)

    probs = []
    curs = [p_ref[s] for s in range(nb)]
    prevs = [_shift_rows(cur, carry_scr[s], 1) for s, cur in enumerate(curs)]
    for s, cur in enumerate(curs):
        carry_scr[s] = cur[n - SUBLANES:]
    acts = _rw_math(jnp.concatenate(curs, axis=0), jnp.concatenate(prevs, axis=0), *params, True)
    g_all, bv_all = acts[6], acts[7]
    for s in range(nb):
        r, lw, k, v, a, b = [t[s * n:(s + 1) * n] for t in acts[:6]]
        cum = _cumsum_rows(lw)
        e_n = jnp.exp(-cum)
        ctot = cum[n - 1:n]
        e_rel = jnp.exp(ctot - cum)
        at = a * jnp.exp(cum - lw)
        rt = r * jnp.exp(cum)
        bt = b * e_n
        kt = k * e_n
        bh = b * e_rel
        kh = k * e_rel
        for p in range(nhp):
            ls = slice(p * LANES, (p + 1) * LANES)
            probs.append(dict(
                s=s, p=p, ls=ls, vp=v[:, ls], dec=jnp.exp(ctot[:, ls]),
                ar=jnp.concatenate([stack(at[:, ls]), stack(rt[:, ls])], axis=0),
                bk=jnp.concatenate([stack(bt[:, ls]), stack(kt[:, ls])], axis=0),
                bkh=jnp.concatenate([bh[:, ls], kh[:, ls]], axis=0)))
    for q in probs:
        q['st'] = s_scr[q['s'], q['p']]
        q['a_s'] = _dot3(q['ar'], q['st'], _NT)
    for q in probs:
        q['g'] = _dot3(q['ar'], q['bk'], _NT)
    t_invs = _tri_inverse([jnp.where(strict, q['g'][0:n2, 0:n2], 0.0) for q in probs], RW_INV_BLOCK, n)
    for q in probs:
        q['v2'] = stack(q['vp'])
        q['rhs'] = q['a_s'][0:n2] + _dot3(jnp.where(strict, q['g'][0:n2, n2:], 0.0), q['v2'])
    for q, t_inv in zip(probs, t_invs):
        q['u2'] = _dot3(t_inv, q['rhs'])
    outs = {}
    for q in probs:
        g = q['g']
        pm = jnp.concatenate([jnp.where(tril, g[n2:, 0:n2], 0.0), jnp.where(tril, g[n2:, n2:], 0.0)], axis=1)
        o2 = q['a_s'][n2:] + _dot3(pm, jnp.concatenate([q['u2'], q['v2']], axis=0))
        outs[q['s'], q['p']] = o2[0:n] + o2[n:]
    for q in probs:
        u2 = q['u2']
        upd = _dot3(jnp.concatenate([u2[0:n] + u2[n:], q['vp']], axis=0), q['bkh'], _TN)
        s_scr[q['s'], q['p']] = q['st'] * q['dec'] + jnp.where(bd, upd, 0.0)
    o_all = jnp.concatenate([jnp.concatenate([outs[s, p] for p in range(nhp)], axis=-1) for s in range(nb)], axis=0)
    c_all = _rw_finish(o_all, g_all, bv_all, ln_g[...], ln_b[...]).astype(BF16)
    for s in range(nb):
        c_ref[s] = c_all[s * n:(s + 1) * n]

    @pl.when(c == pl.num_programs(1) - 1)
    def _():
        co_ref[...] = carry_scr[...]
        for s in range(nb):
            _pair_state_store(s_scr.at[s], so_ref.at[s], nhp)


def _rw_chunked(proj, shift8, params, s0):
    nhp = s0.shape[1] // 2
    tb = min(RW_CHUNK, proj.shape[1])
    cw = params[1].shape[-1]
    scratch = [lambda nb: pltpu.VMEM((nb, nhp, LANES, LANES), F32),
               lambda nb: pltpu.VMEM((nb, SUBLANES, proj.shape[-1]), F32)]
    return _chunk_call(functools.partial(_rw_chunk_kernel, nhp=nhp), proj, shift8, params, s0, cw, tb,
                       RW_SEQ_PER_STEP, scratch, "rw_chunk")


REC_VBLOCK = 8


def _rec_step_kernel(*refs, delta):
    if delta:
        w_ref, k_ref, r_ref, v_ref, nkk_ref, kka_ref, s0_ref, o_ref, so_ref, ot_scr = refs
    else:
        w_ref, k_ref, r_ref, v_ref, s0_ref, o_ref, so_ref, ot_scr = refs
    wt, kt, rt, vt = w_ref[...].T, k_ref[...].T, r_ref[...].T, v_ref[...].T
    if delta:
        at, bt = nkk_ref[...].T, kka_ref[...].T
    for h in range(2):
        hs = slice(h * HEAD, (h + 1) * HEAD)
        w, k, r = wt[hs][None], kt[hs][None], rt[hs][None]
        for vb in range(0, HEAD, REC_VBLOCK):
            vs = slice(vb, vb + REC_VBLOCK)
            s = s0_ref[h, vs]
            val = vt[h * HEAD + vb:h * HEAD + vb + REC_VBLOCK][:, None, :]
            if delta:
                sa = jnp.sum(s * at[hs][None], axis=1)[:, None, :]
                s = s * w + sa * bt[hs][None] + val * k
            else:
                s = s * w + val * k
            so_ref[h, vs] = s
            ot_scr[h * HEAD + vb:h * HEAD + vb + REC_VBLOCK, :] = jnp.sum(s * r, axis=1)
    o_ref[...] = ot_scr[...].T


def _recurrence(rows_in, v, s_all, layer, divs):
    delta = len(rows_in) == 5
    grp, rows, c = v.shape
    nh, ns = s_all.shape[1], s_all.shape[-1]
    assert grp == 1 and rows == ns
    dw, dk, dr = divs

    def tile(div):
        return pl.BlockSpec((rows, LANES), lambda p: (0, p // div))

    sspec = pl.BlockSpec((None, 2) + s_all.shape[2:], lambda p: (layer, p, 0, 0, 0))
    operands = [rows_in[0][0], rows_in[1][0], rows_in[2][0], v[0]] + [a[0] for a in rows_in[3:]]
    in_specs = [tile(dw), tile(dk), tile(dr), tile(1)] + [tile(1)] * (len(rows_in) - 3)
    o, s_new = pl.pallas_call(
        functools.partial(_rec_step_kernel, delta=delta), grid=(nh // 2,),
        in_specs=in_specs + [sspec],
        out_specs=[tile(1), sspec],
        out_shape=[jax.ShapeDtypeStruct((rows, c), F32), jax.ShapeDtypeStruct(s_all.shape, F32)],
        scratch_shapes=[pltpu.VMEM((LANES, rows), F32)],
        input_output_aliases={len(operands): 1},
        compiler_params=_cparams(("arbitrary",)),
        name="rec_step",
    )(*operands, s_all)
    return o[None], s_new


def _out_ffn_kernel(x_ref, a_ref, y_ref, c_ref, gate1_ref, wo_ref, g_ref, sc_ref, sh_ref, gate_ref, wg_ref, wu_ref,
                    wd_ref, gf_ref, o_ref, u_scr, *, tf, final_norm):
    mix_in = jnp.concatenate([a_ref[...], y_ref[...], c_ref[...]], axis=-1)
    x = x_ref[...] + gate1_ref[...] * jnp.dot(mix_in, wo_ref[...], preferred_element_type=F32)
    h = _rmsnorm_mod(x, g_ref[...], sc_ref[...], sh_ref[...]).astype(BF16)
    for f in range(wg_ref.shape[-1] // tf):
        fs = slice(f * tf, (f + 1) * tf)
        u = _silu(jnp.dot(h, wg_ref[:, fs], preferred_element_type=F32)) * jnp.dot(h, wu_ref[:, fs],
                                                                                   preferred_element_type=F32)
        u_scr[:, fs] = u.astype(BF16)
    y = x + gate_ref[...] * jnp.dot(u_scr[...], wd_ref[...], preferred_element_type=F32)
    if final_norm:
        y = y * lax.rsqrt(jnp.mean(y * y, axis=-1, keepdims=True) + RMS_EPS) * gf_ref[...]
    o_ref[...] = y


def _out_ffn(x, acts, mod, wo, g, wg, wu, wd, g_final, tm, tf, final_norm):
    grp, rows, d = x.shape
    nf = wg.shape[-1]
    return pl.pallas_call(
        functools.partial(_out_ffn_kernel, tf=tf, final_norm=final_norm),
        grid=(grp, rows // tm),
        in_specs=[_row_spec(tm, d)] + [_row_spec(tm, a.shape[-1]) for a in acts]
        + [_mod_spec(mod, tm, d, 2), _resident_spec(wo), _const_spec(g), _mod_spec(mod, tm, d, 4),
           _mod_spec(mod, tm, d, 3), _mod_spec(mod, tm, d, 5), _resident_spec(wg), _resident_spec(wu),
           _resident_spec(wd), _const_spec(g_final)],
        out_specs=_row_spec(tm, d),
        out_shape=jax.ShapeDtypeStruct(x.shape, F32),
        scratch_shapes=[pltpu.VMEM((tm, nf), BF16)],
        compiler_params=_cparams(("parallel", "parallel")),
        name="out_ffn",
    )(x, *acts, mod[0], wo, g, mod[0], mod[0], mod[0], wg, wu, wd, g_final)


def _pad_cols(a, width):
    return jnp.pad(a, [(0, 0)] * (a.ndim - 1) + [(0, width - a.shape[-1])])


def _pad_rows(a, height):
    return jnp.pad(a, [(0, 0)] * (a.ndim - 2) + [(0, height - a.shape[-2]), (0, 0)])


def _prep_layer(l, P, dims):
    hgw, n_x, n_bc, nh_mb, rww, lw, la, lg = dims
    w_in = P['w_in'][l]
    c0 = 4 * hgw
    c1 = c0 + n_x
    c2 = c1 + n_x + n_bc
    c3 = c2 + nh_mb
    w_hg = w_in[:, :c0].astype(BF16)
    w_mb = jnp.concatenate([w_in[:, c1:c2], w_in[:, c0:c1], _pad_cols(w_in[:, c2:c3], LANES)], axis=1).astype(BF16)
    w_rw_raw = w_in[:, c3:]

    def rw_cols(a):
        o = 3 * rww
        return jnp.concatenate([a[..., :o], _pad_cols(a[..., o:o + lw], LANES),
                                _pad_cols(a[..., o + lw:o + lw + la], LANES),
                                _pad_cols(a[..., o + lw + la:], LANES)], axis=-1)

    row = lambda a: a.reshape(1, -1).astype(F32)
    a_neg = -jnp.exp(P['mb_A_log'][l].astype(F32))
    return dict(
        w_hg=w_hg, w_mb=w_mb, w_rw=rw_cols(w_rw_raw).astype(BF16), rw_cols=rw_cols,
        g1=row(P['g_norm1'][l]), g2=row(P['g_norm2'][l]),
        rw_params=(rw_cols(row(P['rw_mu'][l])), row(P['rw_w0'][l]), row(P['rw_a0'][l]), row(P['rw_k_k'][l]),
                   row(P['rw_k_a'][l]), row(P['rw_r_k'][l]),
                   _pad_rows(P['rw_w2'][l], LANES).astype(BF16), _pad_rows(P['rw_a2'][l], LANES).astype(BF16),
                   _pad_rows(P['rw_g2'][l], LANES).astype(BF16)),
        mb_params=(P['mb_conv_w'][l].astype(F32), row(P['mb_conv_b'][l]), _pad_cols(row(P['mb_dt_bias'][l]), LANES),
                   row(jnp.repeat(a_neg, HEAD)), _pad_cols(row(a_neg), LANES)),
        out_params=(row(P['hg_norm_g'][l]), row(jnp.repeat(P['mb_D'][l], HEAD)), row(P['mb_norm_g'][l]),
                    row(P['rw_ln_g'][l]), row(P['rw_ln_b'][l])),
        wo=P['w_out'][l].astype(BF16), wg=P['w_gate'][l].astype(BF16), wu=P['w_up'][l].astype(BF16),
        wd=P['w_down'][l].astype(BF16),
    )


def _trunk(x, mods, st_hg, st_ssm, st_conv, st_wkv, st_shift, lb_all, layers, g_final, dims, seq, tm, tf):
    hgw, n_x, n_bc, nh_mb, rww, lw, la, lg = dims
    grp, rows, d = x.shape
    depth = len(layers)
    n_grp = n_bc // (2 * HEAD)
    outs = ([], [], [], [], [])
    if seq:
        st_hg = jnp.swapaxes(st_hg, -1, -2)
    else:
        st_hg = jnp.transpose(st_hg, (0, 2, 4, 3, 1))
        st_ssm = jnp.transpose(st_ssm, (0, 2, 3, 4, 1))
        st_wkv = jnp.transpose(st_wkv, (0, 2, 3, 4, 1))
    s_hg = s_ssm = s_wkv = None
    for l in range(depth):
        L = layers[l]
        mod = (mods, l)
        lb = lb_all[l].reshape(1, -1)
        hg_g, d_exp, mb_g, ln_g, ln_b = L['out_params']
        cw_rw = L['w_rw'].shape[-1]
        if seq:
            p_hg, p_mb, rw_p = _in_proj(
                x, L['g1'], mod, [L['w_hg'], L['w_mb'], L['w_rw']], [(_split_epilogue, 1)] * 3, [],
                [4 * hgw, 2 * n_x + n_bc + LANES, cw_rw], tm, "in_proj")
            shift8 = jnp.pad(L['rw_cols'](st_shift[l])[:, None, :], ((0, 0), (SUBLANES - 1, 0), (0, 0)))
            conv8 = jnp.pad(st_conv[l], ((0, 0), (SUBLANES - st_conv[l].shape[1], 0), (0, 0)))
            a_hg, s_hg = _hg_chunked(p_hg, (lb, jnp.log(lb), jnp.log1p(-lb), hg_g), st_hg[l])
            y_mb, s_ssm, conv_tail = _ssd_chunked(p_mb, conv8, L['mb_params'] + (d_exp, mb_g), st_ssm[l], n_x, n_grp)
            c_rw, s_wkv, shift_tail = _rw_chunked(rw_p, shift8, L['rw_params'] + (ln_g, ln_b), st_wkv[l])
            conv_new = conv_tail[:, SUBLANES - st_conv[l].shape[1]:, :]
            shift_new = shift_tail[:, SUBLANES - 1, :]
        else:
            q, w_hg, k_hg, i_hg, g_hg, xbc, z, dtr, rw_p = _in_proj(
                x, L['g1'], mod, [L['w_hg'], L['w_mb'], L['w_rw']],
                [(_hg_epilogue, 5), (_split_epilogue, 3), (_split_epilogue, 1)], [lb],
                [hgw] * 5 + [n_x + n_bc, n_x, LANES, cw_rw], tm, "in_proj_step")
            conv_new = jnp.concatenate([st_conv[l][:, 1:, :], xbc[0][:, None, :]], axis=1)
            shift_new = rw_p[0]
            r_rw, w_rw, k_rw, v_rw, nkk, kka, g_rw, bv = _rw_pre_step(
                rw_p, L['rw_cols'](st_shift[l])[None], L['rw_params'], tm)
            w_mb, v_mb, k_mb, r_mb, mx, zs = _mb_pre_step(
                xbc, z, dtr, tuple(st_conv[l][None, :, 2 - j, :] for j in range(3)), L['mb_params'], tm)
            o_hg, st_hg = _recurrence((w_hg, k_hg, q), i_hg, st_hg, l, (1, 1, 1))
            o_mb, st_ssm = _recurrence((w_mb, k_mb, r_mb), v_mb, st_ssm, l, (1, 2, 2))
            o_rw, st_wkv = _recurrence((w_rw, k_rw, r_rw, nkk, kka), v_rw, st_wkv, l, (1, 1, 1))
            s_hg, s_ssm, s_wkv = st_hg, st_ssm, st_wkv
            a_hg, y_mb, c_rw = _mix_finish((o_hg, g_hg, o_mb, mx, zs, o_rw, g_rw, bv), L['out_params'], tm, n_grp)
        x = _out_ffn(x, (a_hg, y_mb, c_rw), mod, L['wo'], L['g2'], L['wg'], L['wu'], L['wd'], g_final, tm, tf,
                     final_norm=(l == depth - 1))
        o3 = 3 * rww
        shift_new = jnp.concatenate([shift_new[:, :o3], shift_new[:, o3:o3 + lw],
                                     shift_new[:, o3 + LANES:o3 + LANES + la],
                                     shift_new[:, o3 + 2 * LANES:o3 + 2 * LANES + lg]], axis=-1)
        for lst, s in zip(outs, (s_hg, s_ssm, conv_new, s_wkv, shift_new)):
            lst.append(s)
    if seq:
        hg_new, ssm_new, conv_new, wkv_new, shift_new = [jnp.stack(lst) for lst in outs]
        return x, [jnp.swapaxes(hg_new, -1, -2), ssm_new, conv_new, wkv_new, shift_new]
    return x, [jnp.transpose(s_hg, (0, 4, 1, 3, 2)), jnp.transpose(s_ssm, (0, 4, 1, 2, 3)), jnp.stack(outs[2]),
               jnp.transpose(s_wkv, (0, 4, 1, 2, 3)), jnp.stack(outs[4])]


def kernel(x_prompt, x_sample, state_hgrn, state_ssm, state_conv, state_wkv, state_shift, c_prompt, c_sample,
           w_ada, b_ada, g_norm1, w_in, hg_lb_logits, hg_norm_g, mb_conv_w, mb_conv_b, mb_dt_bias, mb_A_log,
           mb_D, mb_norm_g, rw_mu, rw_w0, rw_w2, rw_a0, rw_a2, rw_g2, rw_k_k, rw_k_a, rw_r_k, rw_ln_g, rw_ln_b,
           w_out, g_norm2, w_gate, w_up, w_down, g_final):
    P = dict(g_norm1=g_norm1, w_in=w_in, hg_norm_g=hg_norm_g, mb_conv_w=mb_conv_w, mb_conv_b=mb_conv_b,
             mb_dt_bias=mb_dt_bias, mb_A_log=mb_A_log, mb_D=mb_D, mb_norm_g=mb_norm_g, rw_mu=rw_mu, rw_w0=rw_w0,
             rw_w2=rw_w2, rw_a0=rw_a0, rw_a2=rw_a2, rw_g2=rw_g2, rw_k_k=rw_k_k, rw_k_a=rw_k_a,
             rw_r_k=rw_r_k.reshape(rw_r_k.shape[0], -1), rw_ln_g=rw_ln_g, rw_ln_b=rw_ln_b, w_out=w_out,
             g_norm2=g_norm2, w_gate=w_gate, w_up=w_up, w_down=w_down)
    depth = w_in.shape[0]
    bsz, t_len, d = x_prompt.shape
    dec = x_sample.shape[0]
    hgw = hg_lb_logits.shape[-1]
    n_x = mb_norm_g.shape[-1]
    n_bc = mb_conv_w.shape[-1] - n_x
    nh_mb = mb_dt_bias.shape[-1]
    rww = rw_w0.shape[-1]
    lw, la, lg = rw_w2.shape[1], rw_a2.shape[1], rw_g2.shape[1]
    dims = (hgw, n_x, n_bc, nh_mb, rww, lw, la, lg)
    layers = [_prep_layer(l, P, dims) for l in range(depth)]
    gf = g_final.reshape(1, d)

    cs = jnp.cumsum(jax.nn.softmax(hg_lb_logits.astype(F32), axis=0), axis=0)
    lb_all = cs - cs[:1]

    n_c = bsz + dec
    pad_c = -n_c % (2 * SUBLANES)
    mods_p, mods_s = _ada(jnp.pad(jnp.concatenate([c_prompt, c_sample], axis=0), ((0, pad_c), (0, 0))), bsz, dec,
                          w_ada, b_ada)
    mods_p = mods_p[:, :, None, :]
    mods_s = mods_s[:, None, :, :]

    dt = x_prompt.dtype
    zeros = lambda a: jnp.zeros((depth, bsz) + a.shape[2:], dt)
    y_p, st_p = _trunk(x_prompt, mods_p, zeros(state_hgrn), zeros(state_ssm), zeros(state_conv), zeros(state_wkv),
                       zeros(state_shift), lb_all, layers, gf, dims, True, min(512, t_len), 256)
    y_s, st_s = _trunk(x_sample.reshape(1, dec, d), mods_s, state_hgrn, state_ssm, state_conv, state_wkv,
                       state_shift, lb_all, layers, gf, dims, False, dec, 256)
    hg_p, ssm_p, conv_p, wkv_p, shift_p = st_p
    hg_s, ssm_s, conv_s, wkv_s, shift_s = st_s
    return (y_p, y_s.reshape(x_sample.shape), hg_p, hg_s, ssm_p, ssm_s, conv_p, conv_s, wkv_p, wkv_s,
            shift_p, shift_s)
```
